```python
import math
import jax, jax.numpy as jnp
from jax import lax
import numpy as np

D_MODEL = 1024
BATCH = 8
SEQ = 2048
DEPTH = 2
DEC_BATCH = 128
DEC_SEQ = 8
PAST_LEN = 16384
PAGE_SIZE = 128

N_MIXERS = 4
G = D_MODEL // N_MIXERS
N_BLOCKS = 10
D_IN = N_BLOCKS * G
SC_WIDTH = 3
POOL_WINDOWS = (2, 4, 8, 16)
N_POOL = len(POOL_WINDOWS)
POOL_CH = G // N_POOL
POOL_BUF = max(POOL_WINDOWS) - 1
HGRN_HEADS = 4
HGRN_DK = G // HGRN_HEADS
HGRN_DV = G // HGRN_HEADS
HGRN_CHUNK = 64
CONF_WIDTH = 31
D_FF = -(-8 * D_MODEL // (3 * 256)) * 256
EPS = 1e-6
F_MIN = 1e-20

kernel_name = 'hybrid_shortconv_pool_hgrn2_conformer_step'


def rmsnorm(x, g):
    xf = x.astype(jnp.float32)
    y = xf * lax.rsqrt(jnp.mean(xf * xf, axis=-1, keepdims=True) + EPS)
    return (y * g.astype(jnp.float32)).astype(x.dtype)


def layernorm(x, g, b):
    xf = x.astype(jnp.float32)
    mu = jnp.mean(xf, axis=-1, keepdims=True)
    xc = xf - mu
    y = xc * lax.rsqrt(jnp.mean(xc * xc, axis=-1, keepdims=True) + EPS)
    return (y * g.astype(jnp.float32) + b.astype(jnp.float32)).astype(x.dtype)


def causal_dwconv(u, buf, w):
    width = w.shape[0]
    xx = jnp.concatenate([buf.astype(u.dtype), u], axis=1)
    y = lax.conv_general_dilated(xx, w.astype(u.dtype)[:, None, :], window_strides=(1,), padding='VALID',
                                 dimension_numbers=('NWC', 'WIO', 'NWC'), feature_group_count=u.shape[-1])
    return y, xx[:, xx.shape[1] - (width - 1):]


def multiscale_pool(p, buf, start_pos):
    n, t, _ = p.shape
    xx = jnp.concatenate([buf.astype(p.dtype), p], axis=1)
    cs = jnp.cumsum(xx.astype(jnp.float32), axis=1)
    cs = jnp.pad(cs, ((0, 0), (1, 0), (0, 0)))
    pos = start_pos + jnp.arange(t)
    outs = []
    for gi, w in enumerate(POOL_WINDOWS):
        sl = slice(gi * POOL_CH, (gi + 1) * POOL_CH)
        hi = cs[:, POOL_BUF + 1:POOL_BUF + 1 + t, sl]
        lo = cs[:, POOL_BUF + 1 - w:POOL_BUF + 1 - w + t, sl]
        cnt = jnp.minimum(pos + 1, w).astype(jnp.float32)
        outs.append((hi - lo) / cnt[None, :, None])
    mean = jnp.concatenate(outs, axis=-1)
    return (mean - p.astype(jnp.float32)).astype(p.dtype), xx[:, xx.shape[1] - POOL_BUF:]


def hgrn2_scan(q, k, v, logf, s0):
    n, t = q.shape[:2]
    c = min(HGRN_CHUNK, t)
    pad = (-t) % c
    if pad:
        pw = ((0, 0), (0, pad), (0, 0), (0, 0))
        q, k, v, logf = [jnp.pad(a, pw) for a in (q, k, v, logf)]
    nc = (t + pad) // c

    def to_chunks(a):
        return a.reshape(n, nc, c, *a.shape[2:]).swapaxes(0, 1)

    mask = jnp.tril(jnp.ones((c, c), dtype=bool))[None, :, :, None, None]

    def step(s, inp):
        qc, kc, vc, lc = inp
        b = jnp.cumsum(lc, axis=1)
        diff = b[:, :, None] - b[:, None]
        decay = jnp.where(mask, jnp.exp(jnp.minimum(diff, 0.0)), 0.0)
        attn = jnp.einsum('ntshk,nshk->nths', qc[:, :, None] * decay, kc)
        o = (jnp.einsum('nths,nshv->nthv', attn, vc)
             + jnp.einsum('nthk,nhkv->nthv', qc * jnp.exp(b), s))
        b_end = b[:, -1]
        s_new = (jnp.exp(b_end)[..., None] * s
                 + jnp.einsum('nshk,nshv->nhkv', kc * jnp.exp(b_end[:, None] - b), vc))
        return s_new, o

    s_fin, o = lax.scan(step, s0, (to_chunks(q), to_chunks(k), to_chunks(v), to_chunks(logf)))
    o = o.swapaxes(0, 1).reshape(n, nc * c, HGRN_HEADS, HGRN_DV)[:, :t]
    return o, s_fin


def hgrn2(zq, zf, zi, zg, lower, s0, norm_g):
    n, t, _ = zq.shape
    q = jax.nn.silu(zq.astype(jnp.float32).reshape(n, t, HGRN_HEADS, HGRN_DK))
    lb = lower.reshape(HGRN_HEADS, HGRN_DK)
    zf4 = zf.astype(jnp.float32).reshape(n, t, HGRN_HEADS, HGRN_DK)
    f = lb + (1.0 - lb) * jax.nn.sigmoid(zf4)
    logf = jnp.log(jnp.maximum(f, F_MIN))
    k = 1.0 - f
    v = zi.astype(jnp.float32).reshape(n, t, HGRN_HEADS, HGRN_DV)
    o, s = hgrn2_scan(q, k, v, logf, s0.astype(jnp.float32))
    o = o * lax.rsqrt(jnp.mean(o * o, axis=-1, keepdims=True) + EPS)
    o = o * norm_g.astype(jnp.float32).reshape(HGRN_HEADS, HGRN_DV)
    y = o.reshape(n, t, G) * jax.nn.silu(zg.astype(jnp.float32))
    return y.astype(zq.dtype), s.astype(s0.dtype)


def trunk(x, s_conv, s_pool, s_hgrn, s_conf, start_pos, wts):
    (norm_mix_pre, norm_mix_post, w_in, conv_w, pool_w, pool_scale, hgrn_lb, hgrn_norm,
     conf_dw, conf_b, conf_ln_g, conf_ln_b, w_out, norm_ffn_pre, norm_ffn_post,
     w_gate, w_up, w_down) = wts
    n, t, _ = x.shape
    sm = jax.nn.softmax(hgrn_lb.astype(jnp.float32), axis=0)
    lower = jnp.cumsum(sm, axis=0) - sm[0:1]
    new_conv, new_pool, new_hgrn, new_conf = [], [], [], []
    for l in range(DEPTH):
        h = rmsnorm(x, norm_mix_pre[l])
        (a_b, a_c, a_u, b_p, c_q, c_f, c_i, c_g, d_a, d_g) = jnp.split(h @ w_in[l], N_BLOCKS, axis=-1)
        ya, buf = causal_dwconv(a_c * a_u, s_conv[:, l], conv_w[l])
        ya = a_b * ya
        new_conv.append(buf)
        pooled, buf = multiscale_pool(b_p, s_pool[:, l], start_pos)
        yb = jnp.einsum('ntgc,gcd->ntgd', pooled.reshape(n, t, N_POOL, POOL_CH), pool_w[l])
        yb = yb.reshape(n, t, G) * pool_scale[l]
        new_pool.append(buf)
        yc, st = hgrn2(c_q, c_f, c_i, c_g, lower[l], s_hgrn[:, l], hgrn_norm[l])
        new_hgrn.append(st)
        u = d_a * jax.nn.sigmoid(d_g)
        yd, buf = causal_dwconv(u, s_conf[:, l], conf_dw[l])
        yd = jax.nn.silu(layernorm(yd + conf_b[l], conf_ln_g[l], conf_ln_b[l]))
        new_conf.append(buf)
        mix = jnp.concatenate([ya, yb, yc, yd], axis=-1) @ w_out[l]
        x = x + rmsnorm(mix, norm_mix_post[l])
        h = rmsnorm(x, norm_ffn_pre[l])
        ff = (jax.nn.silu(h @ w_gate[l]) * (h @ w_up[l])) @ w_down[l]
        x = x + rmsnorm(ff, norm_ffn_post[l])
    return (x, jnp.stack(new_conv, axis=1), jnp.stack(new_pool, axis=1),
            jnp.stack(new_hgrn, axis=1), jnp.stack(new_conf, axis=1))


def setup_inputs(seed: int = 0) -> dict:
    key = jax.random.key(seed)
    ks = jax.random.split(key, 32)

    def nrm(k, shape, s):
        return jax.random.normal(k, shape, jnp.float32) * s

    def gain(k, shape):
        return 1.0 + 0.05 * jax.random.normal(k, shape, jnp.float32)

    return {
        'x_prompt': nrm(ks[0], (BATCH, SEQ, D_MODEL), 1.0),
        'x_sample': nrm(ks[1], (DEC_BATCH, DEC_SEQ, D_MODEL), 1.0),
        'state_conv': nrm(ks[2], (DEC_BATCH, DEPTH, SC_WIDTH - 1, G), 1.0),
        'state_pool': nrm(ks[3], (DEC_BATCH, DEPTH, POOL_BUF, G), 1.0),
        'state_hgrn': nrm(ks[4], (DEC_BATCH, DEPTH, HGRN_HEADS, HGRN_DK, HGRN_DV), 0.5),
        'state_conf': nrm(ks[5], (DEC_BATCH, DEPTH, CONF_WIDTH - 1, G), 0.5),
        'norm_mix_pre': gain(ks[6], (DEPTH, D_MODEL)),
        'norm_mix_post': gain(ks[7], (DEPTH, D_MODEL)),
        'w_in': nrm(ks[8], (DEPTH, D_MODEL, D_IN), D_MODEL ** -0.5),
        'conv_w': nrm(ks[9], (DEPTH, SC_WIDTH, G), SC_WIDTH ** -0.5),
        'pool_w': nrm(ks[10], (DEPTH, N_POOL, POOL_CH, POOL_CH), POOL_CH ** -0.5),
        'pool_scale': gain(ks[11], (DEPTH, G)),
        'hgrn_lb': nrm(ks[12], (DEPTH, G), 0.5),
        'hgrn_norm': gain(ks[13], (DEPTH, G)),
        'conf_dw': nrm(ks[14], (DEPTH, CONF_WIDTH, G), CONF_WIDTH ** -0.5),
        'conf_b': nrm(ks[15], (DEPTH, G), 0.02),
        'conf_ln_g': gain(ks[16], (DEPTH, G)),
        'conf_ln_b': nrm(ks[17], (DEPTH, G), 0.02),
        'w_out': nrm(ks[18], (DEPTH, N_MIXERS * G, D_MODEL), (N_MIXERS * G) ** -0.5),
        'norm_ffn_pre': gain(ks[19], (DEPTH, D_MODEL)),
        'norm_ffn_post': gain(ks[20], (DEPTH, D_MODEL)),
        'w_gate': nrm(ks[21], (DEPTH, D_MODEL, D_FF), D_MODEL ** -0.5),
        'w_up': nrm(ks[22], (DEPTH, D_MODEL, D_FF), D_MODEL ** -0.5),
        'w_down': nrm(ks[23], (DEPTH, D_FF, D_MODEL), D_FF ** -0.5),
    }


def reference(x_prompt, x_sample, state_conv, state_pool, state_hgrn, state_conf,
              norm_mix_pre, norm_mix_post, w_in, conv_w, pool_w, pool_scale, hgrn_lb, hgrn_norm,
              conf_dw, conf_b, conf_ln_g, conf_ln_b, w_out, norm_ffn_pre, norm_ffn_post,
              w_gate, w_up, w_down):
    wts = (norm_mix_pre, norm_mix_post, w_in, conv_w, pool_w, pool_scale, hgrn_lb, hgrn_norm,
           conf_dw, conf_b, conf_ln_g, conf_ln_b, w_out, norm_ffn_pre, norm_ffn_post,
           w_gate, w_up, w_down)
    dt = x_prompt.dtype
    p_conv = jnp.zeros((BATCH, DEPTH, SC_WIDTH - 1, G), dt)
    p_pool = jnp.zeros((BATCH, DEPTH, POOL_BUF, G), dt)
    p_hgrn = jnp.zeros((BATCH, DEPTH, HGRN_HEADS, HGRN_DK, HGRN_DV), state_hgrn.dtype)
    p_conf = jnp.zeros((BATCH, DEPTH, CONF_WIDTH - 1, G), dt)
    y_prompt, conv_p, pool_p, hgrn_p, conf_p = trunk(x_prompt, p_conv, p_pool, p_hgrn, p_conf, 0, wts)
    y_sample, conv_s, pool_s, hgrn_s, conf_s = trunk(x_sample, state_conv, state_pool, state_hgrn,
                                                     state_conf, PAST_LEN, wts)
    return (y_prompt, y_sample, conv_p, pool_p, hgrn_p, conf_p, conv_s, pool_s, hgrn_s, conf_s)
```

```python
import functools

import jax
import jax.numpy as jnp
from jax import lax
from jax.experimental import pallas as pl
from jax.experimental.pallas import tpu as pltpu

F32 = jnp.float32
BF16 = jnp.bfloat16

D_MODEL = 1024
DEPTH = 2
PAST_LEN = 16384
G = 256
N_BLOCKS = 10
D_IN = N_BLOCKS * G
SC_WIDTH = 3
POOL_BUF = 15
POOL_CH = 64
HEADS = 4
DK = 64
CONF_WIDTH = 31
D_FF = 2816
EPS = 1e-6
F_MIN = 1e-20

HGRN_CHUNK = 64
FAST_DECAY_LIMIT = 80.0
PROMPT_TILE = 256
SAMPLE_SEQS = 32
FFN_TILE = 256
CONV_ROWS = 64
VMEM_LIMIT = 56 * 1024 * 1024


def _sigmoid(x):
    return jax.nn.sigmoid(x)


def _silu(x):
    return x * jax.nn.sigmoid(x)


def _rmsnorm(x, g):
    ms = jnp.mean(x * x, axis=-1, keepdims=True)
    return x * lax.rsqrt(ms + EPS) * g


def _head_block_mask(rows, cols, row_block, col_block):
    r = lax.broadcasted_iota(jnp.int32, (rows, cols), 0) // row_block
    c = lax.broadcasted_iota(jnp.int32, (rows, cols), 1) // col_block
    return r == c


def _cumsum_rows(x):
    n = x.shape[0]
    row = lax.broadcasted_iota(jnp.int32, x.shape, 0)
    s = 1
    while s < n:
        x = x + jnp.where(row >= s, pltpu.roll(x, s, 0), 0.0)
        s *= 2
    return x


def _lower_bound(lb_all, layer):
    m = jnp.max(lb_all, axis=0, keepdims=True)
    e = jnp.exp(lb_all - m)
    sm = e / jnp.sum(e, axis=0, keepdims=True)
    cs = sm[0:1]
    for i in range(1, layer + 1):
        cs = cs + sm[i:i + 1]
    return cs - sm[0:1]


def _hgrn_chunk(zq, zf, zi, lower, st_ref, hb_ref, hq_ref, o_ref, row0, C):
    q = _silu(zq)
    f = lower + (1.0 - lower) * _sigmoid(zf)
    logf = jnp.log(jnp.maximum(f, F_MIN))
    kk = 1.0 - f
    v = zi
    b = _cumsum_rows(logf)
    b_end = b[C - 1:C, :]
    mid = C // 2 - 1
    r = b[mid:mid + 1, :]
    st = st_ref[...]

    qs = (q * jnp.exp(b)).astype(BF16)
    o_inter = lax.dot_general(qs, st.astype(BF16), (((1,), (1,)), ((), ())),
                              preferred_element_type=F32)
    kh = (kk * jnp.exp(b_end - b)).astype(BF16)
    upd = lax.dot_general(v.astype(BF16), kh, (((0,), (0,)), ((), ())),
                          preferred_element_type=F32)
    bd = _head_block_mask(G, G, DK, DK)
    st_ref[...] = st * jnp.exp(b_end) + jnp.where(bd, upd, 0.0)

    rows = pl.ds(row0, C)
    span = jnp.max(jnp.maximum(b[0:1, :] - r, r - b_end))
    rows_mask = _head_block_mask(HEADS * C, G, C, DK)

    @pl.when(span < FAST_DECAY_LIMIT)
    def _fast():
        qe = (q * jnp.exp(b - r)).astype(BF16)
        ke = kk * jnp.exp(r - b)
        kebd = jnp.where(rows_mask, jnp.concatenate([ke] * HEADS, axis=0), 0.0).astype(BF16)
        attn = lax.dot_general(qe, kebd, (((1,), (1,)), ((), ())),
                               preferred_element_type=F32)
        t_idx = lax.broadcasted_iota(jnp.int32, (C, HEADS * C), 0)
        s_idx = lax.broadcasted_iota(jnp.int32, (C, HEADS * C), 1) % C
        attn = jnp.where(t_idx >= s_idx, attn, 0.0).astype(BF16)
        vbd = jnp.where(rows_mask, jnp.concatenate([v] * HEADS, axis=0), 0.0).astype(BF16)
        o_ref[rows, :] = o_inter + jnp.dot(attn, vbd, preferred_element_type=F32)

    @pl.when(span >= FAST_DECAY_LIMIT)
    def _exact():
        hb_ref[0:C, :] = b
        hq_ref[0:C, :] = q
        ones_bd = jnp.where(bd, 1.0, 0.0).astype(BF16)
        s_row = lax.broadcasted_iota(jnp.int32, (C, G), 0)

        def body(t, carry):
            bt = hb_ref[pl.ds(t, 1), :]
            qt = hq_ref[pl.ds(t, 1), :]
            e = jnp.where(s_row <= t, qt * kk * jnp.exp(jnp.minimum(bt - b, 0.0)), 0.0)
            a = jnp.dot(e.astype(BF16), ones_bd, preferred_element_type=F32)
            o_ref[pl.ds(row0 + t, 1), :] = jnp.sum(a * v, axis=0, keepdims=True)
            return carry

        lax.fori_loop(0, C, body, 0)
        o_ref[rows, :] = o_ref[rows, :] + o_inter


def _head_norm_gate(o, zg, hnorm):
    ones_bd = jnp.where(_head_block_mask(G, G, DK, DK), 1.0, 0.0).astype(BF16)
    o2 = o * o
    hi = o2.astype(BF16)
    lo = (o2 - hi.astype(F32)).astype(BF16)
    ssq = (jnp.dot(hi, ones_bd, preferred_element_type=F32)
           + jnp.dot(lo, ones_bd, preferred_element_type=F32))
    return o * lax.rsqrt(ssq * (1.0 / DK) + EPS) * hnorm * _silu(zg)


def _pool_select(s2, s4, s8, s16, pos):
    shape = s2.shape
    grp = lax.broadcasted_iota(jnp.int32, shape, len(shape) - 1) // POOL_CH
    ssum = jnp.where(grp == 0, s2, jnp.where(grp == 1, s4, jnp.where(grp == 2, s8, s16)))
    win = jnp.where(grp == 0, 2, jnp.where(grp == 1, 4, jnp.where(grp == 2, 8, 16)))
    cnt = jnp.minimum(pos + 1, win).astype(F32)
    return ssum / cnt


def _conf_tail(z, cb, lng, lnb):
    z = z + cb
    mu = jnp.mean(z, axis=-1, keepdims=True)
    zc = z - mu
    var = jnp.mean(zc * zc, axis=-1, keepdims=True)
    return _silu(zc * lax.rsqrt(var + EPS) * lng + lnb)


def _mix_prompt_kernel(layer, TT,
                       x_ref, npre_ref, win_ref, convw_ref, poolbd_ref, pscale_ref, lb_ref,
                       hnorm_ref, cdw_ref, cb_ref, lng_ref, lnb_ref, wout_ref, npost_ref,
                       y_ref, oconv_ref, opool_ref, ohgrn_ref, oconf_ref,
                       p_ref, ea_ref, eb_ref, ed_ref, st_ref, hb_ref, hq_ref, o_ref, cat_ref):
    t = pl.program_id(1)

    @pl.when(t == 0)
    def _init():
        ea_ref[0:8, :] = jnp.zeros((8, G), F32)
        eb_ref[0:16, :] = jnp.zeros((16, G), F32)
        ed_ref[0:32, :] = jnp.zeros((32, G), F32)
        st_ref[...] = jnp.zeros((G, G), F32)

    x = x_ref[...]
    h = _rmsnorm(x, npre_ref[...]).astype(BF16)
    p_ref[...] = jnp.dot(h, win_ref[...], preferred_element_type=F32)

    cu = p_ref[:, G:2 * G] * p_ref[:, 2 * G:3 * G]
    ea_ref[8:8 + TT, :] = cu
    ya = (convw_ref[2:3, :] * cu + convw_ref[1:2, :] * ea_ref[7:7 + TT, :]
          + convw_ref[0:1, :] * ea_ref[6:6 + TT, :])
    cat_ref[:, 0:G] = (p_ref[:, 0:G] * ya).astype(BF16)
    oconv_ref[...] = ea_ref[TT + 6:TT + 8, :]
    ea_ref[0:8, :] = ea_ref[TT:TT + 8, :]

    pp = p_ref[:, 3 * G:4 * G]
    eb_ref[16:16 + TT, :] = pp
    e = eb_ref[...]
    s2 = e + pltpu.roll(e, 1, 0)
    s4 = s2 + pltpu.roll(s2, 2, 0)
    s8 = s4 + pltpu.roll(s4, 4, 0)
    s16 = s8 + pltpu.roll(s8, 8, 0)
    pos = t * TT + lax.broadcasted_iota(jnp.int32, (TT, G), 0)
    mean = _pool_select(s2[16:], s4[16:], s8[16:], s16[16:], pos)
    yb = jnp.dot((mean - pp).astype(BF16), poolbd_ref[...], preferred_element_type=F32)
    cat_ref[:, G:2 * G] = (yb * pscale_ref[...]).astype(BF16)
    opool_ref[...] = eb_ref[TT + 1:TT + 16, :]
    eb_ref[0:16, :] = eb_ref[TT:TT + 16, :]

    lower = _lower_bound(lb_ref[...], layer)
    for c in range(TT // HGRN_CHUNK):
        rows = slice(c * HGRN_CHUNK, (c + 1) * HGRN_CHUNK)
        _hgrn_chunk(p_ref[rows, 4 * G:5 * G], p_ref[rows, 5 * G:6 * G], p_ref[rows, 6 * G:7 * G],
                    lower, st_ref, hb_ref, hq_ref, o_ref, c * HGRN_CHUNK, HGRN_CHUNK)
    yc = _head_norm_gate(o_ref[...], p_ref[:, 7 * G:8 * G], hnorm_ref[...])
    cat_ref[:, 2 * G:3 * G] = yc.astype(BF16)

    @pl.when(t == pl.num_programs(1) - 1)
    def _state_out():
        s = st_ref[...].T
        for hh in range(HEADS):
            ohgrn_ref[hh] = s[hh * DK:(hh + 1) * DK, hh * DK:(hh + 1) * DK]

    u = p_ref[:, 8 * G:9 * G] * _sigmoid(p_ref[:, 9 * G:10 * G])
    ed_ref[32:32 + TT, :] = u
    for rb in range(TT // CONV_ROWS):
        base = rb * CONV_ROWS
        acc = cdw_ref[0:1, :] * ed_ref[base + 2:base + 2 + CONV_ROWS, :]
        for j in range(1, CONF_WIDTH):
            acc = acc + cdw_ref[j:j + 1, :] * ed_ref[base + 2 + j:base + 2 + j + CONV_ROWS, :]
        yd = _conf_tail(acc, cb_ref[...], lng_ref[...], lnb_ref[...])
        cat_ref[base:base + CONV_ROWS, 3 * G:4 * G] = yd.astype(BF16)
    oconf_ref[...] = ed_ref[TT + 2:TT + 32, :]
    ed_ref[0:32, :] = ed_ref[TT:TT + 32, :]

    mix = jnp.dot(cat_ref[...], wout_ref[...], preferred_element_type=F32)
    y_ref[...] = x + _rmsnorm(mix, npost_ref[...])


def _mix_sample_kernel(layer, NB, T, start_pos,
                       x_ref, sconv_ref, spool_ref, shgrn_ref, sconf_ref,
                       npre_ref, win_ref, convw_ref, poolbd_ref, pscale_ref, lb_ref,
                       hnorm_ref, cdw_ref, cb_ref, lng_ref, lnb_ref, wout_ref, npost_ref,
                       y_ref, oconv_ref, opool_ref, ohgrn_ref, oconf_ref,
                       p_ref, ea_ref, eb_ref, ed_ref, st_ref, hb_ref, hq_ref, o_ref, cat_ref):
    M = NB * T
    x = x_ref[...].reshape(M, D_MODEL)
    h = _rmsnorm(x, npre_ref[...]).astype(BF16)
    p_ref[...] = jnp.dot(h, win_ref[...], preferred_element_type=F32)

    cu = (p_ref[:, G:2 * G] * p_ref[:, 2 * G:3 * G]).reshape(NB, T, G)
    ea_ref[:, 6:8, :] = sconv_ref[...]
    ea_ref[:, 8:16, :] = cu
    ya = (convw_ref[2:3, :] * cu + convw_ref[1:2, :] * ea_ref[:, 7:15, :]
          + convw_ref[0:1, :] * ea_ref[:, 6:14, :])
    cat_ref[:, 0:G] = (p_ref[:, 0:G] * ya.reshape(M, G)).astype(BF16)
    oconv_ref[...] = ea_ref[:, 14:16, :]

    pp = p_ref[:, 3 * G:4 * G].reshape(NB, T, G)
    eb_ref[:, 1:16, :] = spool_ref[...]
    eb_ref[:, 16:24, :] = pp
    run = pp
    sums = {}
    for j in range(1, 16):
        run = run + eb_ref[:, 16 - j:24 - j, :]
        if j + 1 in (2, 4, 8, 16):
            sums[j + 1] = run
    pos = start_pos + lax.broadcasted_iota(jnp.int32, (NB, T, G), 1)
    mean = _pool_select(sums[2], sums[4], sums[8], sums[16], pos)
    yb = jnp.dot((mean - pp).reshape(M, G).astype(BF16), poolbd_ref[...],
                 preferred_element_type=F32)
    cat_ref[:, G:2 * G] = (yb * pscale_ref[...]).astype(BF16)
    opool_ref[...] = eb_ref[:, 9:24, :]

    lower = _lower_bound(lb_ref[...], layer)
    bd = _head_block_mask(G, G, DK, DK)

    def seq_body(n, carry):
        s0 = shgrn_ref[n]
        sbd = jnp.concatenate(
            [jnp.concatenate([s0[hh]] * HEADS, axis=1) for hh in range(HEADS)], axis=0)
        st_ref[...] = jnp.where(bd, sbd, 0.0).T
        row0 = pl.multiple_of(n * T, T)
        rows = pl.ds(row0, T)
        _hgrn_chunk(p_ref[rows, 4 * G:5 * G], p_ref[rows, 5 * G:6 * G], p_ref[rows, 6 * G:7 * G],
                    lower, st_ref, hb_ref, hq_ref, o_ref, row0, T)
        s = st_ref[...].T
        for hh in range(HEADS):
            ohgrn_ref[n, hh] = s[hh * DK:(hh + 1) * DK, hh * DK:(hh + 1) * DK]
        return carry

    lax.fori_loop(0, NB, seq_body, 0)
    yc = _head_norm_gate(o_ref[...], p_ref[:, 7 * G:8 * G], hnorm_ref[...])
    cat_ref[:, 2 * G:3 * G] = yc.astype(BF16)

    u = (p_ref[:, 8 * G:9 * G] * _sigmoid(p_ref[:, 9 * G:10 * G])).reshape(NB, T, G)
    ed_ref[:, 2:32, :] = sconf_ref[...]
    ed_ref[:, 32:40, :] = u
    SB = 8
    for sb in range(NB // SB):
        sl = slice(sb * SB, (sb + 1) * SB)
        acc = cdw_ref[0:1, :] * ed_ref[sl, 2:2 + T, :]
        for j in range(1, CONF_WIDTH):
            acc = acc + cdw_ref[j:j + 1, :] * ed_ref[sl, 2 + j:2 + j + T, :]
        yd = _conf_tail(acc, cb_ref[...], lng_ref[...], lnb_ref[...])
        cat_ref[sb * SB * T:(sb + 1) * SB * T, 3 * G:4 * G] = yd.reshape(SB * T, G).astype(BF16)
    oconf_ref[...] = ed_ref[:, 10:40, :]

    mix = jnp.dot(cat_ref[...], wout_ref[...], preferred_element_type=F32)
    y_ref[...] = (x + _rmsnorm(mix, npost_ref[...])).reshape(NB, T, D_MODEL)


def _ffn_kernel(x_ref, npre_ref, wg_ref, wu_ref, wd_ref, npost_ref, y_ref):
    x = x_ref[...]
    h = _rmsnorm(x, npre_ref[...]).astype(BF16)
    g = jnp.dot(h, wg_ref[...], preferred_element_type=F32)
    u = jnp.dot(h, wu_ref[...], preferred_element_type=F32)
    a = (_silu(g) * u).astype(BF16)
    ff = jnp.dot(a, wd_ref[...], preferred_element_type=F32)
    y_ref[...] = x + _rmsnorm(ff, npost_ref[...])


def _layer_spec(shape, layer, single_buffer=False):
    nd = len(shape)

    def imap(*_):
        return (layer,) + (0,) * nd

    if single_buffer:
        return pl.BlockSpec((None,) + tuple(shape), imap, pipeline_mode=pl.Buffered(1))
    return pl.BlockSpec((None,) + tuple(shape), imap)


def _mixer_weight_specs(layer):
    return [
        _layer_spec((1, D_MODEL), layer),
        _layer_spec((D_MODEL, D_IN), layer, True),
        _layer_spec((SC_WIDTH, G), layer),
        _layer_spec((G, G), layer),
        _layer_spec((1, G), layer),
        pl.BlockSpec((DEPTH, G), lambda *_: (0, 0)),
        _layer_spec((1, G), layer),
        _layer_spec((CONF_WIDTH, G), layer),
        _layer_spec((1, G), layer),
        _layer_spec((1, G), layer),
        _layer_spec((1, G), layer),
        _layer_spec((D_MODEL, D_MODEL), layer, True),
        _layer_spec((1, D_MODEL), layer),
    ]


def _mix_prompt(layer, x, wts):
    n, seq, _ = x.shape
    TT = PROMPT_TILE
    grid = (n, seq // TT)
    out_shape = (
        jax.ShapeDtypeStruct((n, seq, D_MODEL), F32),
        jax.ShapeDtypeStruct((n, SC_WIDTH - 1, G), F32),
        jax.ShapeDtypeStruct((n, POOL_BUF, G), F32),
        jax.ShapeDtypeStruct((n, HEADS, DK, DK), F32),
        jax.ShapeDtypeStruct((n, CONF_WIDTH - 1, G), F32),
    )
    out_specs = (
        pl.BlockSpec((None, TT, D_MODEL), lambda b, t: (b, t, 0)),
        pl.BlockSpec((None, SC_WIDTH - 1, G), lambda b, t: (b, 0, 0)),
        pl.BlockSpec((None, POOL_BUF, G), lambda b, t: (b, 0, 0)),
        pl.BlockSpec((None, HEADS, DK, DK), lambda b, t: (b, 0, 0, 0)),
        pl.BlockSpec((None, CONF_WIDTH - 1, G), lambda b, t: (b, 0, 0)),
    )
    scratch = [
        pltpu.VMEM((TT, D_IN), F32),
        pltpu.VMEM((8 + TT, G), F32),
        pltpu.VMEM((16 + TT, G), F32),
        pltpu.VMEM((32 + TT, G), F32),
        pltpu.VMEM((G, G), F32),
        pltpu.VMEM((HGRN_CHUNK, G), F32),
        pltpu.VMEM((HGRN_CHUNK, G), F32),
        pltpu.VMEM((TT, G), F32),
        pltpu.VMEM((TT, D_MODEL), BF16),
    ]
    return pl.pallas_call(
        functools.partial(_mix_prompt_kernel, layer, TT),
        grid=grid,
        in_specs=[pl.BlockSpec((None, TT, D_MODEL), lambda b, t: (b, t, 0))]
        + _mixer_weight_specs(layer),
        out_specs=out_specs,
        out_shape=out_shape,
        scratch_shapes=scratch,
        compiler_params=pltpu.CompilerParams(
            dimension_semantics=("arbitrary", "arbitrary"), vmem_limit_bytes=VMEM_LIMIT),
        name=f"mix_prompt_l{layer}",
    )(x, *wts)


def _mix_sample(layer, x, s_conv, s_pool, s_hgrn, s_conf, wts):
    n, T, _ = x.shape
    NB = SAMPLE_SEQS
    grid = (n // NB,)
    out_shape = (
        jax.ShapeDtypeStruct((n, T, D_MODEL), F32),
        jax.ShapeDtypeStruct((n, SC_WIDTH - 1, G), F32),
        jax.ShapeDtypeStruct((n, POOL_BUF, G), F32),
        jax.ShapeDtypeStruct((n, HEADS, DK, DK), F32),
        jax.ShapeDtypeStruct((n, CONF_WIDTH - 1, G), F32),
    )

    def state_spec(shape):
        nd = len(shape)
        return pl.BlockSpec((NB, None) + tuple(shape), lambda i: (i, layer) + (0,) * nd)

    def out_spec(shape):
        nd = len(shape)
        return pl.BlockSpec((NB,) + tuple(shape), lambda i: (i,) + (0,) * nd)

    in_specs = [
        pl.BlockSpec((NB, T, D_MODEL), lambda i: (i, 0, 0)),
        state_spec((SC_WIDTH - 1, G)),
        state_spec((POOL_BUF, G)),
        state_spec((HEADS, DK, DK)),
        state_spec((CONF_WIDTH - 1, G)),
    ] + _mixer_weight_specs(layer)
    out_specs = (
        out_spec((T, D_MODEL)),
        out_spec((SC_WIDTH - 1, G)),
        out_spec((POOL_BUF, G)),
        out_spec((HEADS, DK, DK)),
        out_spec((CONF_WIDTH - 1, G)),
    )
    M = NB * T
    scratch = [
        pltpu.VMEM((M, D_IN), F32),
        pltpu.VMEM((NB, 16, G), F32),
        pltpu.VMEM((NB, 24, G), F32),
        pltpu.VMEM((NB, 40, G), F32),
        pltpu.VMEM((G, G), F32),
        pltpu.VMEM((T, G), F32),
        pltpu.VMEM((T, G), F32),
        pltpu.VMEM((M, G), F32),
        pltpu.VMEM((M, D_MODEL), BF16),
    ]
    return pl.pallas_call(
        functools.partial(_mix_sample_kernel, layer, NB, T, PAST_LEN),
        grid=grid,
        in_specs=in_specs,
        out_specs=out_specs,
        out_shape=out_shape,
        scratch_shapes=scratch,
        compiler_params=pltpu.CompilerParams(
            dimension_semantics=("arbitrary",), vmem_limit_bytes=VMEM_LIMIT),
        name=f"mix_sample_l{layer}",
    )(x, s_conv, s_pool, s_hgrn, s_conf, *wts)


def _ffn(layer, x2d, npre, wg, wu, wd, npost, tag):
    m = x2d.shape[0]
    TM = FFN_TILE
    return pl.pallas_call(
        _ffn_kernel,
        grid=(m // TM,),
        in_specs=[
            pl.BlockSpec((TM, D_MODEL), lambda i: (i, 0)),
            _layer_spec((1, D_MODEL), layer),
            _layer_spec((D_MODEL, D_FF), layer, True),
            _layer_spec((D_MODEL, D_FF), layer, True),
            _layer_spec((D_FF, D_MODEL), layer, True),
            _layer_spec((1, D_MODEL), layer),
        ],
        out_specs=pl.BlockSpec((TM, D_MODEL), lambda i: (i, 0)),
        out_shape=jax.ShapeDtypeStruct((m, D_MODEL), F32),
        compiler_params=pltpu.CompilerParams(
            dimension_semantics=("arbitrary",), vmem_limit_bytes=VMEM_LIMIT),
        name=f"ffn_{tag}_l{layer}",
    )(x2d, npre, wg, wu, wd, npost)


def kernel(x_prompt, x_sample, state_conv, state_pool, state_hgrn, state_conf, norm_mix_pre, norm_mix_post, w_in, conv_w, pool_w, pool_scale, hgrn_lb, hgrn_norm, conf_dw, conf_b, conf_ln_g, conf_ln_b, w_out, norm_ffn_pre, norm_ffn_post, w_gate, w_up, w_down):
    def row(a):
        return a.reshape(DEPTH, 1, a.shape[-1])

    eye = jnp.eye(G // POOL_CH, dtype=pool_w.dtype)
    pool_bd = (pool_w[:, :, :, None, :] * eye[None, :, None, :, None]).reshape(DEPTH, G, G)
    mixer_wts = (row(norm_mix_pre), w_in.astype(BF16), conv_w, pool_bd.astype(BF16),
                 row(pool_scale), hgrn_lb, row(hgrn_norm), conf_dw, row(conf_b),
                 row(conf_ln_g), row(conf_ln_b), w_out.astype(BF16), row(norm_mix_post))
    ffn_wts = (row(norm_ffn_pre), w_gate.astype(BF16), w_up.astype(BF16), w_down.astype(BF16),
               row(norm_ffn_post))

    nb, seq, _ = x_prompt.shape
    ns, ts, _ = x_sample.shape
    xp, xs = x_prompt, x_sample
    p_states, s_states = [], []
    for layer in range(DEPTH):
        xp, *stp = _mix_prompt(layer, xp, mixer_wts)
        p_states.append(stp)
        xp = _ffn(layer, xp.reshape(nb * seq, D_MODEL), *ffn_wts, "prompt").reshape(nb, seq, D_MODEL)
        xs, *sts = _mix_sample(layer, xs, state_conv, state_pool, state_hgrn, state_conf, mixer_wts)
        s_states.append(sts)
        xs = _ffn(layer, xs.reshape(ns * ts, D_MODEL), *ffn_wts, "sample").reshape(ns, ts, D_MODEL)

    def stack(states, i):
        return jnp.stack([states[layer][i] for layer in range(DEPTH)], axis=1)

    return (xp, xs,
            stack(p_states, 0), stack(p_states, 1), stack(p_states, 2), stack(p_states, 3),
            stack(s_states, 0), stack(s_states, 1), stack(s_states, 2), stack(s_states, 3))
```

```python
import functools

import jax
import jax.numpy as jnp
from jax import lax
from jax.experimental import pallas as pl
from jax.experimental.pallas import tpu as pltpu

F32 = jnp.float32
BF16 = jnp.bfloat16

D_MODEL = 1024
DEPTH = 2
PAST_LEN = 16384
G = 256
N_BLOCKS = 10
D_IN = N_BLOCKS * G
SC_WIDTH = 3
POOL_BUF = 15
POOL_CH = 64
HEADS = 4
DK = 64
CONF_WIDTH = 31
D_FF = 2816
EPS = 1e-6
F_MIN = 1e-20

HGRN_CHUNK = 64
FAST_DECAY_LIMIT = 60.0
PROMPT_TILE = 256
SAMPLE_SEQS = 32
FFN_TILE = 256
CONV_ROWS = 64
VMEM_LIMIT = 56 * 1024 * 1024


def _sigmoid(x):
    return jax.nn.sigmoid(x)


def _silu(x):
    return x * jax.nn.sigmoid(x)


def _rmsnorm(x, g):
    ms = jnp.mean(x * x, axis=-1, keepdims=True)
    return x * lax.rsqrt(ms + EPS) * g


def _head_block_mask(rows, cols, row_block, col_block):
    r = lax.broadcasted_iota(jnp.int32, (rows, cols), 0) // row_block
    c = lax.broadcasted_iota(jnp.int32, (rows, cols), 1) // col_block
    return r == c


def _cumsum_rows(x):
    n = x.shape[0]
    row = lax.broadcasted_iota(jnp.int32, x.shape, 0)
    s = 1
    while s < n:
        x = x + jnp.where(row >= s, pltpu.roll(x, s, 0), 0.0)
        s *= 2
    return x


def _lower_bound(lb_all, layer):
    m = jnp.max(lb_all, axis=0, keepdims=True)
    e = jnp.exp(lb_all - m)
    sm = e / jnp.sum(e, axis=0, keepdims=True)
    cs = sm[0:1]
    for i in range(1, layer + 1):
        cs = cs + sm[i:i + 1]
    return cs - sm[0:1]


def _cumsum_rows_mxu(x):
    n = x.shape[0]
    tri = (lax.broadcasted_iota(jnp.int32, (n, n), 0)
           >= lax.broadcasted_iota(jnp.int32, (n, n), 1))
    tri = jnp.where(tri, 1.0, 0.0).astype(BF16)
    hi = x.astype(BF16)
    r1 = x - hi.astype(F32)
    mid = r1.astype(BF16)
    lo = (r1 - mid.astype(F32)).astype(BF16)
    return (jnp.dot(tri, hi, preferred_element_type=F32)
            + jnp.dot(tri, mid, preferred_element_type=F32)
            + jnp.dot(tri, lo, preferred_element_type=F32))


def _hgrn_tile(zq, zf, zi, lower, st_ref, hb_ref, hq_ref, o_ref, row0, TT, C, mxu_cumsum):
    q = _silu(zq)
    f = lower + (1.0 - lower) * _sigmoid(zf)
    logf = jnp.log(jnp.maximum(f, F_MIN))
    kk = 1.0 - f
    v = zi
    b = _cumsum_rows_mxu(logf) if mxu_cumsum else _cumsum_rows(logf)
    b_end = b[TT - 1:TT, :]
    st = st_ref[...]
    n_chunks = TT // C

    qs = (q * jnp.exp(b)).astype(BF16)
    o_inter = lax.dot_general(qs, st.astype(BF16), (((1,), (1,)), ((), ())),
                              preferred_element_type=F32)
    kh = (kk * jnp.exp(b_end - b)).astype(BF16)
    upd = lax.dot_general(v.astype(BF16), kh, (((0,), (0,)), ((), ())),
                          preferred_element_type=F32)
    bd = _head_block_mask(G, G, DK, DK)
    st_ref[...] = st * jnp.exp(b_end) + jnp.where(bd, upd, 0.0)

    refs = []
    span = None
    for j in range(n_chunks):
        first = b[j * C:j * C + 1, :]
        last = b[(j + 1) * C - 1:(j + 1) * C, :]
        refs.append(0.5 * (first + last))
        half = jnp.max(0.5 * (first - last))
        span = half if span is None else jnp.maximum(span, half)
    rows = pl.ds(row0, TT)

    @pl.when(span < FAST_DECAY_LIMIT)
    def _fast():
        rows_mask = _head_block_mask(HEADS * C, G, C, DK)
        o_ref[rows, :] = o_inter
        for j in range(n_chunks):
            lo, hi = j * C, (j + 1) * C
            r = refs[j]
            qz = (q[lo:] * jnp.exp(b[lo:] - r)).astype(BF16)
            ke = kk[lo:hi] * jnp.exp(r - b[lo:hi])
            kebd = jnp.where(rows_mask, jnp.concatenate([ke] * HEADS, axis=0), 0.0).astype(BF16)
            attn = lax.dot_general(qz, kebd, (((1,), (1,)), ((), ())),
                                   preferred_element_type=F32)
            t_idx = lax.broadcasted_iota(jnp.int32, (TT - lo, HEADS * C), 0)
            s_idx = lax.broadcasted_iota(jnp.int32, (TT - lo, HEADS * C), 1) % C
            attn = jnp.where(t_idx >= s_idx, attn, 0.0).astype(BF16)
            vbd = jnp.where(rows_mask, jnp.concatenate([v[lo:hi]] * HEADS, axis=0),
                            0.0).astype(BF16)
            tgt = pl.ds(row0 + lo, TT - lo)
            o_ref[tgt, :] = o_ref[tgt, :] + jnp.dot(attn, vbd, preferred_element_type=F32)

    @pl.when(span >= FAST_DECAY_LIMIT)
    def _exact():
        hb_ref[0:TT, :] = b
        hq_ref[0:TT, :] = q
        ones_bd = jnp.where(bd, 1.0, 0.0).astype(BF16)
        s_row = lax.broadcasted_iota(jnp.int32, (TT, G), 0)

        def body(t, carry):
            bt = hb_ref[pl.ds(t, 1), :]
            qt = hq_ref[pl.ds(t, 1), :]
            e = jnp.where(s_row <= t, qt * kk * jnp.exp(jnp.minimum(bt - b, 0.0)), 0.0)
            a = jnp.dot(e.astype(BF16), ones_bd, preferred_element_type=F32)
            o_ref[pl.ds(row0 + t, 1), :] = jnp.sum(a * v, axis=0, keepdims=True)
            return carry

        lax.fori_loop(0, TT, body, 0)
        o_ref[rows, :] = o_ref[rows, :] + o_inter


def _head_norm_gate(o, zg, hnorm):
    ones_bd = jnp.where(_head_block_mask(G, G, DK, DK), 1.0, 0.0).astype(BF16)
    o2 = o * o
    hi = o2.astype(BF16)
    lo = (o2 - hi.astype(F32)).astype(BF16)
    ssq = (jnp.dot(hi, ones_bd, preferred_element_type=F32)
           + jnp.dot(lo, ones_bd, preferred_element_type=F32))
    return o * lax.rsqrt(ssq * (1.0 / DK) + EPS) * hnorm * _silu(zg)


def _pool_select(s2, s4, s8, s16, pos):
    shape = s2.shape
    grp = lax.broadcasted_iota(jnp.int32, shape, len(shape) - 1) // POOL_CH
    ssum = jnp.where(grp == 0, s2, jnp.where(grp == 1, s4, jnp.where(grp == 2, s8, s16)))
    win = jnp.where(grp == 0, 2, jnp.where(grp == 1, 4, jnp.where(grp == 2, 8, 16)))
    cnt = jnp.minimum(pos + 1, win).astype(F32)
    return ssum / cnt


def _conf_tail(z, cb, lng, lnb):
    z = z + cb
    mu = jnp.mean(z, axis=-1, keepdims=True)
    zc = z - mu
    var = jnp.mean(zc * zc, axis=-1, keepdims=True)
    return _silu(zc * lax.rsqrt(var + EPS) * lng + lnb)


def _mix_prompt_kernel(layer, TT,
                       x_ref, npre_ref, win_ref, convw_ref, poolbd_ref, pscale_ref, lb_ref,
                       hnorm_ref, cdw_ref, cb_ref, lng_ref, lnb_ref, wout_ref, npost_ref,
                       y_ref, oconv_ref, opool_ref, ohgrn_ref, oconf_ref,
                       p_ref, ea_ref, eb_ref, ed_ref, sh_ref, st_ref, hb_ref, hq_ref, o_ref, cat_ref):
    t = pl.program_id(1)

    @pl.when(t == 0)
    def _init():
        ea_ref[0:8, :] = jnp.zeros((8, G), F32)
        eb_ref[0:16, :] = jnp.zeros((16, G), F32)
        ed_ref[0:32, :] = jnp.zeros((32, G), F32)
        st_ref[...] = jnp.zeros((G, G), F32)

    x = x_ref[...]
    h = _rmsnorm(x, npre_ref[...]).astype(BF16)
    p_ref[...] = jnp.dot(h, win_ref[...], preferred_element_type=F32)

    cu = p_ref[:, G:2 * G] * p_ref[:, 2 * G:3 * G]
    ea_ref[8:8 + TT, :] = cu
    ya = (convw_ref[2:3, :] * cu + convw_ref[1:2, :] * ea_ref[7:7 + TT, :]
          + convw_ref[0:1, :] * ea_ref[6:6 + TT, :])
    cat_ref[:, 0:G] = (p_ref[:, 0:G] * ya).astype(BF16)
    oconv_ref[...] = ea_ref[TT + 6:TT + 8, :]
    ea_ref[0:8, :] = ea_ref[TT:TT + 8, :]

    pp = p_ref[:, 3 * G:4 * G]
    eb_ref[16:16 + TT, :] = pp
    e = eb_ref[...]
    s2 = e + pltpu.roll(e, 1, 0)
    s4 = s2 + pltpu.roll(s2, 2, 0)
    s8 = s4 + pltpu.roll(s4, 4, 0)
    s16 = s8 + pltpu.roll(s8, 8, 0)
    pos = t * TT + lax.broadcasted_iota(jnp.int32, (TT, G), 0)
    mean = _pool_select(s2[16:], s4[16:], s8[16:], s16[16:], pos)
    yb = jnp.dot((mean - pp).astype(BF16), poolbd_ref[...], preferred_element_type=F32)
    cat_ref[:, G:2 * G] = (yb * pscale_ref[...]).astype(BF16)
    opool_ref[...] = eb_ref[TT + 1:TT + 16, :]
    eb_ref[0:16, :] = eb_ref[TT:TT + 16, :]

    lower = _lower_bound(lb_ref[...], layer)
    _hgrn_tile(p_ref[:, 4 * G:5 * G], p_ref[:, 5 * G:6 * G], p_ref[:, 6 * G:7 * G],
               lower, st_ref, hb_ref, hq_ref, o_ref, 0, TT, HGRN_CHUNK, True)
    yc =_head_norm_gate(o_ref[...], p_ref[:, 7 * G:8 * G], hnorm_ref[...])
    cat_ref[:, 2 * G:3 * G] = yc.astype(BF16)

    @pl.when(t == pl.num_programs(1) - 1)
    def _state_out():
        s = st_ref[...].T
        for hh in range(HEADS):
            ohgrn_ref[hh] = s[hh * DK:(hh + 1) * DK, hh * DK:(hh + 1) * DK]

    u = p_ref[:, 8 * G:9 * G] * _sigmoid(p_ref[:, 9 * G:10 * G])
    ed_ref[32:32 + TT, :] = u
    ed = ed_ref[...]
    for r in range(1, 8):
        sh_ref[r - 1] = pltpu.roll(ed, TT + 32 - r, 0)
    for rb in range(TT // CONV_ROWS):
        base = rb * CONV_ROWS
        acc = None
        for j in range(CONF_WIDTH):
            a8, r = divmod(2 + j, 8)
            lo = base + 8 * a8
            src = ed_ref[lo:lo + CONV_ROWS, :] if r == 0 else sh_ref[r - 1, lo:lo + CONV_ROWS, :]
            term = cdw_ref[j:j + 1, :] * src
            acc = term if acc is None else acc + term
        yd =_conf_tail(acc, cb_ref[...], lng_ref[...], lnb_ref[...])
        cat_ref[base:base + CONV_ROWS, 3 * G:4 * G] = yd.astype(BF16)
    oconf_ref[...] = ed_ref[TT + 2:TT + 32, :]
    ed_ref[0:32, :] = ed_ref[TT:TT + 32, :]

    mix = jnp.dot(cat_ref[...], wout_ref[...], preferred_element_type=F32)
    y_ref[...] = x + _rmsnorm(mix, npost_ref[...])


def _mix_sample_kernel(layer, NB, T, start_pos,
                       x_ref, sconv_ref, spool_ref, shgrn_ref, sconf_ref,
                       npre_ref, win_ref, convw_ref, poolbd_ref, pscale_ref, lb_ref,
                       hnorm_ref, cdw_ref, cb_ref, lng_ref, lnb_ref, wout_ref, npost_ref,
                       y_ref, oconv_ref, opool_ref, ohgrn_ref, oconf_ref,
                       p_ref, ea_ref, eb_ref, ed_ref, st_ref, hb_ref, hq_ref, o_ref, cat_ref):
    M = NB * T
    x = x_ref[...].reshape(M, D_MODEL)
    h = _rmsnorm(x, npre_ref[...]).astype(BF16)
    p_ref[...] = jnp.dot(h, win_ref[...], preferred_element_type=F32)

    cu = (p_ref[:, G:2 * G] * p_ref[:, 2 * G:3 * G]).reshape(NB, T, G)
    ea_ref[:, 6:8, :] = sconv_ref[...]
    ea_ref[:, 8:16, :] = cu
    ya = (convw_ref[2:3, :] * cu + convw_ref[1:2, :] * ea_ref[:, 7:15, :]
          + convw_ref[0:1, :] * ea_ref[:, 6:14, :])
    cat_ref[:, 0:G] = (p_ref[:, 0:G] * ya.reshape(M, G)).astype(BF16)
    oconv_ref[...] = ea_ref[:, 14:16, :]

    pp = p_ref[:, 3 * G:4 * G].reshape(NB, T, G)
    eb_ref[:, 1:16, :] = spool_ref[...]
    eb_ref[:, 16:24, :] = pp
    run = pp
    sums = {}
    for j in range(1, 16):
        run = run + eb_ref[:, 16 - j:24 - j, :]
        if j + 1 in (2, 4, 8, 16):
            sums[j + 1] = run
    pos = start_pos + lax.broadcasted_iota(jnp.int32, (NB, T, G), 1)
    mean = _pool_select(sums[2], sums[4], sums[8], sums[16], pos)
    yb = jnp.dot((mean - pp).reshape(M, G).astype(BF16), poolbd_ref[...],
                 preferred_element_type=F32)
    cat_ref[:, G:2 * G] = (yb * pscale_ref[...]).astype(BF16)
    opool_ref[...] = eb_ref[:, 9:24, :]

    lower = _lower_bound(lb_ref[...], layer)
    bd = _head_block_mask(G, G, DK, DK)

    def seq_body(n, carry):
        s0 = shgrn_ref[n]
        sbd = jnp.concatenate(
            [jnp.concatenate([s0[hh]] * HEADS, axis=1) for hh in range(HEADS)], axis=0)
        st_ref[...] = jnp.where(bd, sbd, 0.0).T
        row0 = pl.multiple_of(n * T, T)
        rows = pl.ds(row0, T)
        _hgrn_tile(p_ref[rows, 4 * G:5 * G], p_ref[rows, 5 * G:6 * G], p_ref[rows, 6 * G:7 * G],
                   lower, st_ref, hb_ref, hq_ref, o_ref, row0, T, T, False)
        s = st_ref[...].T
        for hh in range(HEADS):
            ohgrn_ref[n, hh] = s[hh * DK:(hh + 1) * DK, hh * DK:(hh + 1) * DK]
        return carry

    lax.fori_loop(0, NB, seq_body, 0)
    yc = _head_norm_gate(o_ref[...], p_ref[:, 7 * G:8 * G], hnorm_ref[...])
    cat_ref[:, 2 * G:3 * G] = yc.astype(BF16)

    u = (p_ref[:, 8 * G:9 * G] * _sigmoid(p_ref[:, 9 * G:10 * G])).reshape(NB, T, G)
    ed_ref[:, 2:32, :] = sconf_ref[...]
    ed_ref[:, 32:40, :] = u
    SB = 8
    for sb in range(NB // SB):
        sl = slice(sb * SB, (sb + 1) * SB)
        acc = cdw_ref[0:1, :] * ed_ref[sl, 2:2 + T, :]
        for j in range(1, CONF_WIDTH):
            acc = acc + cdw_ref[j:j + 1, :] * ed_ref[sl, 2 + j:2 + j + T, :]
        yd = _conf_tail(acc, cb_ref[...], lng_ref[...], lnb_ref[...])
        cat_ref[sb * SB * T:(sb + 1) * SB * T, 3 * G:4 * G] = yd.reshape(SB * T, G).astype(BF16)
    oconf_ref[...] = ed_ref[:, 10:40, :]

    mix = jnp.dot(cat_ref[...], wout_ref[...], preferred_element_type=F32)
    y_ref[...] = (x + _rmsnorm(mix, npost_ref[...])).reshape(NB, T, D_MODEL)


def _ffn_kernel(x_ref, npre_ref, wg_ref, wu_ref, wd_ref, npost_ref, y_ref):
    x = x_ref[...]
    h = _rmsnorm(x, npre_ref[...]).astype(BF16)
    g = jnp.dot(h, wg_ref[...], preferred_element_type=F32)
    u = jnp.dot(h, wu_ref[...], preferred_element_type=F32)
    a = (_silu(g) * u).astype(BF16)
    ff = jnp.dot(a, wd_ref[...], preferred_element_type=F32)
    y_ref[...] = x + _rmsnorm(ff, npost_ref[...])


def _layer_spec(shape, layer, single_buffer=False):
    nd = len(shape)

    def imap(*_):
        return (layer,) + (0,) * nd

    if single_buffer:
        return pl.BlockSpec((None,) + tuple(shape), imap, pipeline_mode=pl.Buffered(1))
    return pl.BlockSpec((None,) + tuple(shape), imap)


def _mixer_weight_specs(layer):
    return [
        _layer_spec((1, D_MODEL), layer),
        _layer_spec((D_MODEL, D_IN), layer, True),
        _layer_spec((SC_WIDTH, G), layer),
        _layer_spec((G, G), layer),
        _layer_spec((1, G), layer),
        pl.BlockSpec((DEPTH, G), lambda *_: (0, 0)),
        _layer_spec((1, G), layer),
        _layer_spec((CONF_WIDTH, G), layer),
        _layer_spec((1, G), layer),
        _layer_spec((1, G), layer),
        _layer_spec((1, G), layer),
        _layer_spec((D_MODEL, D_MODEL), layer, True),
        _layer_spec((1, D_MODEL), layer),
    ]


def _mix_prompt(layer, x, wts):
    n, seq, _ = x.shape
    TT = PROMPT_TILE
    grid = (n, seq // TT)
    out_shape = (
        jax.ShapeDtypeStruct((n, seq, D_MODEL), F32),
        jax.ShapeDtypeStruct((n, SC_WIDTH - 1, G), F32),
        jax.ShapeDtypeStruct((n, POOL_BUF, G), F32),
        jax.ShapeDtypeStruct((n, HEADS, DK, DK), F32),
        jax.ShapeDtypeStruct((n, CONF_WIDTH - 1, G), F32),
    )
    out_specs = (
        pl.BlockSpec((None, TT, D_MODEL), lambda b, t: (b, t, 0)),
        pl.BlockSpec((None, SC_WIDTH - 1, G), lambda b, t: (b, 0, 0)),
        pl.BlockSpec((None, POOL_BUF, G), lambda b, t: (b, 0, 0)),
        pl.BlockSpec((None, HEADS, DK, DK), lambda b, t: (b, 0, 0, 0)),
        pl.BlockSpec((None, CONF_WIDTH - 1, G), lambda b, t: (b, 0, 0)),
    )
    scratch = [
        pltpu.VMEM((TT, D_IN), F32),
        pltpu.VMEM((8 + TT, G), F32),
        pltpu.VMEM((16 + TT, G), F32),
        pltpu.VMEM((32 + TT, G), F32),
        pltpu.VMEM((7, 32 + TT, G), F32),
        pltpu.VMEM((G, G), F32),
        pltpu.VMEM((TT, G), F32),
        pltpu.VMEM((TT, G), F32),
        pltpu.VMEM((TT, G), F32),
        pltpu.VMEM((TT, D_MODEL), BF16),
    ]
    return pl.pallas_call(
        functools.partial(_mix_prompt_kernel, layer, TT),
        grid=grid,
        in_specs=[pl.BlockSpec((None, TT, D_MODEL), lambda b, t: (b, t, 0))]
        + _mixer_weight_specs(layer),
        out_specs=out_specs,
        out_shape=out_shape,
        scratch_shapes=scratch,
        compiler_params=pltpu.CompilerParams(
            dimension_semantics=("arbitrary", "arbitrary"), vmem_limit_bytes=VMEM_LIMIT),
        name=f"mix_prompt_l{layer}",
    )(x, *wts)


def _mix_sample(layer, x, s_conv, s_pool, s_hgrn, s_conf, wts):
    n, T, _ = x.shape
    NB = SAMPLE_SEQS
    grid = (n // NB,)
    out_shape = (
        jax.ShapeDtypeStruct((n, T, D_MODEL), F32),
        jax.ShapeDtypeStruct((n, SC_WIDTH - 1, G), F32),
        jax.ShapeDtypeStruct((n, POOL_BUF, G), F32),
        jax.ShapeDtypeStruct((n, HEADS, DK, DK), F32),
        jax.ShapeDtypeStruct((n, CONF_WIDTH - 1, G), F32),
    )

    def state_spec(shape):
        nd = len(shape)
        return pl.BlockSpec((NB, None) + tuple(shape), lambda i: (i, layer) + (0,) * nd)

    def out_spec(shape):
        nd = len(shape)
        return pl.BlockSpec((NB,) + tuple(shape), lambda i: (i,) + (0,) * nd)

    in_specs = [
        pl.BlockSpec((NB, T, D_MODEL), lambda i: (i, 0, 0)),
        state_spec((SC_WIDTH - 1, G)),
        state_spec((POOL_BUF, G)),
        state_spec((HEADS, DK, DK)),
        state_spec((CONF_WIDTH - 1, G)),
    ] + _mixer_weight_specs(layer)
    out_specs = (
        out_spec((T, D_MODEL)),
        out_spec((SC_WIDTH - 1, G)),
        out_spec((POOL_BUF, G)),
        out_spec((HEADS, DK, DK)),
        out_spec((CONF_WIDTH - 1, G)),
    )
    M = NB * T
    scratch = [
        pltpu.VMEM((M, D_IN), F32),
        pltpu.VMEM((NB, 16, G), F32),
        pltpu.VMEM((NB, 24, G), F32),
        pltpu.VMEM((NB, 40, G), F32),
        pltpu.VMEM((G, G), F32),
        pltpu.VMEM((T, G), F32),
        pltpu.VMEM((T, G), F32),
        pltpu.VMEM((M, G), F32),
        pltpu.VMEM((M, D_MODEL), BF16),
    ]
    return pl.pallas_call(
        functools.partial(_mix_sample_kernel, layer, NB, T, PAST_LEN),
        grid=grid,
        in_specs=in_specs,
        out_specs=out_specs,
        out_shape=out_shape,
        scratch_shapes=scratch,
        compiler_params=pltpu.CompilerParams(
            dimension_semantics=("arbitrary",), vmem_limit_bytes=VMEM_LIMIT),
        name=f"mix_sample_l{layer}",
    )(x, s_conv, s_pool, s_hgrn, s_conf, *wts)


def _ffn(layer, x2d, npre, wg, wu, wd, npost, tag):
    m = x2d.shape[0]
    TM = FFN_TILE
    return pl.pallas_call(
        _ffn_kernel,
        grid=(m // TM,),
        in_specs=[
            pl.BlockSpec((TM, D_MODEL), lambda i: (i, 0)),
            _layer_spec((1, D_MODEL), layer),
            _layer_spec((D_MODEL, D_FF), layer, True),
            _layer_spec((D_MODEL, D_FF), layer, True),
            _layer_spec((D_FF, D_MODEL), layer, True),
            _layer_spec((1, D_MODEL), layer),
        ],
        out_specs=pl.BlockSpec((TM, D_MODEL), lambda i: (i, 0)),
        out_shape=jax.ShapeDtypeStruct((m, D_MODEL), F32),
        compiler_params=pltpu.CompilerParams(
            dimension_semantics=("arbitrary",), vmem_limit_bytes=VMEM_LIMIT),
        name=f"ffn_{tag}_l{layer}",
    )(x2d, npre, wg, wu, wd, npost)


def kernel(x_prompt, x_sample, state_conv, state_pool, state_hgrn, state_conf, norm_mix_pre, norm_mix_post, w_in, conv_w, pool_w, pool_scale, hgrn_lb, hgrn_norm, conf_dw, conf_b, conf_ln_g, conf_ln_b, w_out, norm_ffn_pre, norm_ffn_post, w_gate, w_up, w_down):
    def row(a):
        return a.reshape(DEPTH, 1, a.shape[-1])

    eye = jnp.eye(G // POOL_CH, dtype=pool_w.dtype)
    pool_bd = (pool_w[:, :, :, None, :] * eye[None, :, None, :, None]).reshape(DEPTH, G, G)
    mixer_wts = (row(norm_mix_pre), w_in.astype(BF16), conv_w, pool_bd.astype(BF16),
                 row(pool_scale), hgrn_lb, row(hgrn_norm), conf_dw, row(conf_b),
                 row(conf_ln_g), row(conf_ln_b), w_out.astype(BF16), row(norm_mix_post))
    ffn_wts = (row(norm_ffn_pre), w_gate.astype(BF16), w_up.astype(BF16), w_down.astype(BF16),
               row(norm_ffn_post))

    nb, seq, _ = x_prompt.shape
    ns, ts, _ = x_sample.shape
    xp, xs = x_prompt, x_sample
    p_states, s_states = [], []
    for layer in range(DEPTH):
        xp, *stp = _mix_prompt(layer, xp, mixer_wts)
        p_states.append(stp)
        xp = _ffn(layer, xp.reshape(nb * seq, D_MODEL), *ffn_wts, "prompt").reshape(nb, seq, D_MODEL)
        xs, *sts = _mix_sample(layer, xs, state_conv, state_pool, state_hgrn, state_conf, mixer_wts)
        s_states.append(sts)
        xs = _ffn(layer, xs.reshape(ns * ts, D_MODEL), *ffn_wts, "sample").reshape(ns, ts, D_MODEL)

    def stack(states, i):
        return jnp.stack([states[layer][i] for layer in range(DEPTH)], axis=1)

    return (xp, xs,
            stack(p_states, 0), stack(p_states, 1), stack(p_states, 2), stack(p_states, 3),
            stack(s_states, 0), stack(s_states, 1), stack(s_states, 2), stack(s_states, 3))
```

```python
import functools

import jax
import jax.numpy as jnp
from jax import lax
from jax.experimental import pallas as pl
from jax.experimental.pallas import tpu as pltpu

F32 = jnp.float32
BF16 = jnp.bfloat16

D_MODEL = 1024
DEPTH = 2
PAST_LEN = 16384
G = 256
N_BLOCKS = 10
D_IN = N_BLOCKS * G
SC_WIDTH = 3
POOL_BUF = 15
POOL_CH = 64
HEADS = 4
DK = 64
CONF_WIDTH = 31
D_FF = 2816
EPS = 1e-6
F_MIN = 1e-20

HGRN_CHUNK = 64
FAST_DECAY_LIMIT = 60.0
PROMPT_TILE = 256
SAMPLE_SEQS = 32
FFN_TILE = 256
FFN_COLS = 256
CONV_ROWS = 64
VMEM_LIMIT = 56 * 1024 * 1024


def _sigmoid(x):
    return jax.nn.sigmoid(x)


def _silu(x):
    return x * jax.nn.sigmoid(x)


def _rmsnorm(x, g):
    ms = jnp.mean(x * x, axis=-1, keepdims=True)
    return x * lax.rsqrt(ms + EPS) * g


def _head_block_mask(rows, cols, row_block, col_block):
    r = lax.broadcasted_iota(jnp.int32, (rows, cols), 0) // row_block
    c = lax.broadcasted_iota(jnp.int32, (rows, cols), 1) // col_block
    return r == c


def _cumsum_rows(x):
    n = x.shape[0]
    row = lax.broadcasted_iota(jnp.int32, x.shape, 0)
    s = 1
    while s < n:
        x = x + jnp.where(row >= s, pltpu.roll(x, s, 0), 0.0)
        s *= 2
    return x


def _cumsum_rows_mxu(x):
    n = x.shape[0]
    tri = (lax.broadcasted_iota(jnp.int32, (n, n), 0)
           >= lax.broadcasted_iota(jnp.int32, (n, n), 1))
    tri = jnp.where(tri, 1.0, 0.0).astype(BF16)
    hi = x.astype(BF16)
    r1 = x - hi.astype(F32)
    mid = r1.astype(BF16)
    lo = (r1 - mid.astype(F32)).astype(BF16)
    return (jnp.dot(tri, hi, preferred_element_type=F32)
            + jnp.dot(tri, mid, preferred_element_type=F32)
            + jnp.dot(tri, lo, preferred_element_type=F32))


def _lower_bound(lb_all, layer):
    m = jnp.max(lb_all, axis=0, keepdims=True)
    e = jnp.exp(lb_all - m)
    sm = e / jnp.sum(e, axis=0, keepdims=True)
    cs = sm[0:1]
    for i in range(1, layer + 1):
        cs = cs + sm[i:i + 1]
    return cs - sm[0:1]


def _hgrn_gates(zq, zf, zi, lower):
    q = _silu(zq)
    f = lower + (1.0 - lower) * _sigmoid(zf)
    logf = jnp.log(jnp.maximum(f, F_MIN))
    return q, 1.0 - f, zi, logf


def _hgrn_state_terms(q, kk, v, b, st_ref):
    TT = q.shape[0]
    b_end = b[TT - 1:TT, :]
    st = st_ref[...]
    qs = (q * jnp.exp(b)).astype(BF16)
    o_inter = lax.dot_general(qs, st.astype(BF16), (((1,), (1,)), ((), ())),
                              preferred_element_type=F32)
    kh = (kk * jnp.exp(b_end - b)).astype(BF16)
    upd = lax.dot_general(v.astype(BF16), kh, (((0,), (0,)), ((), ())),
                          preferred_element_type=F32)
    bd = _head_block_mask(G, G, DK, DK)
    st_ref[...] = st * jnp.exp(b_end) + jnp.where(bd, upd, 0.0)
    return o_inter


def _hgrn_refs(b, C):
    refs = []
    span = None
    for j in range(b.shape[0] // C):
        first = b[j * C:j * C + 1, :]
        last = b[(j + 1) * C - 1:(j + 1) * C, :]
        refs.append(0.5 * (first + last))
        half = jnp.max(0.5 * (first - last))
        span = half if span is None else jnp.maximum(span, half)
    return refs, span


def _hgrn_fast_chunk(q_tgt, b_tgt, kk_src, v_src, r, C):
    nt = q_tgt.shape[0]
    rows_mask = _head_block_mask(HEADS * C, G, C, DK)
    qz = (q_tgt * jnp.exp(b_tgt - r)).astype(BF16)
    ke = kk_src * jnp.exp(r - b_tgt[0:C])
    kebd = jnp.where(rows_mask, jnp.concatenate([ke] * HEADS, axis=0), 0.0).astype(BF16)
    attn = lax.dot_general(qz, kebd, (((1,), (1,)), ((), ())),
                           preferred_element_type=F32)
    t_idx = lax.broadcasted_iota(jnp.int32, (nt, HEADS * C), 0)
    s_idx = lax.broadcasted_iota(jnp.int32, (nt, HEADS * C), 1) % C
    attn = jnp.where(t_idx >= s_idx, attn, 0.0).astype(BF16)
    vbd = jnp.where(rows_mask, jnp.concatenate([v_src] * HEADS, axis=0), 0.0).astype(BF16)
    return jnp.dot(attn, vbd, preferred_element_type=F32)


def _hgrn_fast_attn(q, kk, v, b, refs, C):
    TT = q.shape[0]
    n_chunks = TT // C
    parts = [[] for _ in range(n_chunks)]
    for j in range(n_chunks):
        lo, hi = j * C, (j + 1) * C
        contrib = _hgrn_fast_chunk(q[lo:], b[lo:], kk[lo:hi], v[lo:hi], refs[j], C)
        for i in range(j, n_chunks):
            parts[i].append(contrib[(i - j) * C:(i - j + 1) * C])
    blocks = []
    for i in range(n_chunks):
        acc = parts[i][0]
        for extra in parts[i][1:]:
            acc = acc + extra
        blocks.append(acc)
    return blocks[0] if n_chunks == 1 else jnp.concatenate(blocks, axis=0)


def _hgrn_exact_attn(hb_ref, hq_ref, kk, v, o_ref, row0, TT):
    b = hb_ref[0:TT, :]
    ones_bd = jnp.where(_head_block_mask(G, G, DK, DK), 1.0, 0.0).astype(BF16)
    s_row = lax.broadcasted_iota(jnp.int32, (TT, G), 0)

    def body(t, carry):
        bt = hb_ref[pl.ds(t, 1), :]
        qt = hq_ref[pl.ds(t, 1), :]
        e = jnp.where(s_row <= t, qt * kk * jnp.exp(jnp.minimum(bt - b, 0.0)), 0.0)
        a = jnp.dot(e.astype(BF16), ones_bd, preferred_element_type=F32)
        o_ref[pl.ds(row0 + t, 1), :] = jnp.sum(a * v, axis=0, keepdims=True)
        return carry

    lax.fori_loop(0, TT, body, 0)


def _hgrn_tile(zq, zf, zi, lower, st_ref, hb_ref, hq_ref, o_ref, row0, TT):
    q, kk, v, logf = _hgrn_gates(zq, zf, zi, lower)
    b = _cumsum_rows(logf)
    o_inter = _hgrn_state_terms(q, kk, v, b, st_ref)
    refs, span = _hgrn_refs(b, TT)
    rows = pl.ds(row0, TT)

    @pl.when(span < FAST_DECAY_LIMIT)
    def _fast():
        o_ref[rows, :] = o_inter + _hgrn_fast_attn(q, kk, v, b, refs, TT)

    @pl.when(span >= FAST_DECAY_LIMIT)
    def _exact():
        hb_ref[0:TT, :] = b
        hq_ref[0:TT, :] = q
        _hgrn_exact_attn(hb_ref, hq_ref, kk, v, o_ref, row0, TT)
        o_ref[rows, :] = o_ref[rows, :] + o_inter


def _head_norm_gate(o, zg, hnorm):
    ones_bd = jnp.where(_head_block_mask(G, G, DK, DK), 1.0, 0.0).astype(BF16)
    o2 = o * o
    hi = o2.astype(BF16)
    lo = (o2 - hi.astype(F32)).astype(BF16)
    ssq = (jnp.dot(hi, ones_bd, preferred_element_type=F32)
           + jnp.dot(lo, ones_bd, preferred_element_type=F32))
    return o * lax.rsqrt(ssq * (1.0 / DK) + EPS) * hnorm * _silu(zg)


def _pool_select(s2, s4, s8, s16, pos):
    shape = s2.shape
    grp = lax.broadcasted_iota(jnp.int32, shape, len(shape) - 1) // POOL_CH
    ssum = jnp.where(grp == 0, s2, jnp.where(grp == 1, s4, jnp.where(grp == 2, s8, s16)))
    win = jnp.where(grp == 0, 2, jnp.where(grp == 1, 4, jnp.where(grp == 2, 8, 16)))
    cnt = jnp.minimum(pos + 1, win).astype(F32)
    return ssum / cnt


def _conf_tail(z, cb, lng, lnb):
    z = z + cb
    mu = jnp.mean(z, axis=-1, keepdims=True)
    zc = z - mu
    var = jnp.mean(zc * zc, axis=-1, keepdims=True)
    return _silu(zc * lax.rsqrt(var + EPS) * lng + lnb)


def _ffn_block(x, npre, wg, wu, wd, npost):
    h = _rmsnorm(x, npre).astype(BF16)
    g = jnp.dot(h, wg, preferred_element_type=F32)
    u = jnp.dot(h, wu, preferred_element_type=F32)
    a = (_silu(g) * u).astype(BF16)
    ff = jnp.dot(a, wd, preferred_element_type=F32)
    return x + _rmsnorm(ff, npost)


def _layer_prompt_kernel(layer, TT, nt,
                         x_ref, npre_ref, win_ref, convw_ref, poolbd_ref, pscale_ref, lb_ref,
                         hnorm_ref, cdw_ref, cb_ref, lng_ref, lnb_ref, wout_ref, npost_ref,
                         fpre_ref, wg_ref, wu_ref, wd_ref, fpost_ref,
                         y_ref, oconv_ref, opool_ref, ohgrn_ref, oconf_ref,
                         p_ref, ea_ref, eb_ref, ed_ref, sh_ref, st_ref, hb_ref, hq_ref, hk_ref, oi_ref,
                         o_ref, cat_ref, x1_ref, hm_ref, hf_ref, a_ref, ff_ref):
    i = pl.program_id(0)
    t = i % nt
    slot = i % 2

    @pl.when(i == 0)
    def _first():
        x1_ref[1] = jnp.zeros((TT, D_MODEL), F32)

    @pl.when(t == 0)
    def _new_sequence():
        ea_ref[0:8, :] = jnp.zeros((8, G), F32)
        eb_ref[0:16, :] = jnp.zeros((16, G), F32)
        ed_ref[0:32, :] = jnp.zeros((32, G), F32)
        st_ref[...] = jnp.zeros((G, G), F32)

    lower = _lower_bound(lb_ref[...], layer)
    n_chunks = TT // HGRN_CHUNK

    def f_norm():
        hf_ref[...] = _rmsnorm(x1_ref[1 - slot], fpre_ref[...]).astype(BF16)

    def f_gate_up(j):
        cols = slice(j * FFN_COLS, (j + 1) * FFN_COLS)
        hf = hf_ref[...]
        g = jnp.dot(hf, wg_ref[:, cols], preferred_element_type=F32)
        u = jnp.dot(hf, wu_ref[:, cols], preferred_element_type=F32)
        a_ref[:, cols] = (_silu(g) * u).astype(BF16)

    def f_down(k):
        cols = slice(k * G, (k + 1) * G)
        ff_ref[:, cols] = jnp.dot(a_ref[...], wd_ref[:, cols], preferred_element_type=F32)

    def f_out():
        y_ref[...] = x1_ref[1 - slot] + _rmsnorm(ff_ref[...], fpost_ref[...])

    def m_norm():
        hm_ref[...] = _rmsnorm(x_ref[...], npre_ref[...]).astype(BF16)

    def m_proj(blk):
        cols = slice(blk * G, (blk + 1) * G)
        p_ref[:, cols] = jnp.dot(hm_ref[...], win_ref[:, cols], preferred_element_type=F32)

    def m_conv():
        cu = p_ref[:, G:2 * G] * p_ref[:, 2 * G:3 * G]
        ea_ref[8:8 + TT, :] = cu
        ya = (convw_ref[2:3, :] * cu + convw_ref[1:2, :] * ea_ref[7:7 + TT, :]
              + convw_ref[0:1, :] * ea_ref[6:6 + TT, :])
        cat_ref[:, 0:G] = (p_ref[:, 0:G] * ya).astype(BF16)
        oconv_ref[...] = ea_ref[TT + 6:TT + 8, :]
        ea_ref[0:8, :] = ea_ref[TT:TT + 8, :]

    def m_pool():
        pp = p_ref[:, 3 * G:4 * G]
        eb_ref[16:16 + TT, :] = pp
        e = eb_ref[...]
        s2 = e + pltpu.roll(e, 1, 0)
        s4 = s2 + pltpu.roll(s2, 2, 0)
        s8 = s4 + pltpu.roll(s4, 4, 0)
        s16 = s8 + pltpu.roll(s8, 8, 0)
        pos = t * TT + lax.broadcasted_iota(jnp.int32, (TT, G), 0)
        mean = _pool_select(s2[16:], s4[16:], s8[16:], s16[16:], pos)
        yb = jnp.dot((mean - pp).astype(BF16), poolbd_ref[...], preferred_element_type=F32)
        cat_ref[:, G:2 * G] = (yb * pscale_ref[...]).astype(BF16)
        opool_ref[...] = eb_ref[TT + 1:TT + 16, :]
        eb_ref[0:16, :] = eb_ref[TT:TT + 16, :]

    hg = {}

    def m_hgrn_gates():
        q, kk, _, logf = _hgrn_gates(p_ref[:, 4 * G:5 * G], p_ref[:, 5 * G:6 * G],
                                     p_ref[:, 6 * G:7 * G], lower)
        hq_ref[...] = q
        hk_ref[...] = kk
        hb_ref[...] = _cumsum_rows_mxu(logf)

    def m_hgrn_state():
        b = hb_ref[...]
        o_inter = _hgrn_state_terms(hq_ref[...], hk_ref[...], p_ref[:, 6 * G:7 * G], b, st_ref)
        oi_ref[...] = o_inter
        o_ref[...] = o_inter
        hg["refs"], hg["span"] = _hgrn_refs(b, HGRN_CHUNK)

    def m_hgrn_chunk(j):
        lo, hi = j * HGRN_CHUNK, (j + 1) * HGRN_CHUNK
        contrib = _hgrn_fast_chunk(hq_ref[lo:TT, :], hb_ref[lo:TT, :], hk_ref[lo:hi, :],
                                   p_ref[lo:hi, 6 * G:7 * G], hg["refs"][j], HGRN_CHUNK)
        o_ref[lo:TT, :] = o_ref[lo:TT, :] + contrib

    def m_hgrn_out():
        yc = _head_norm_gate(o_ref[...], p_ref[:, 7 * G:8 * G], hnorm_ref[...])
        cat_ref[:, 2 * G:3 * G] = yc.astype(BF16)

    def m_glu():
        ed_ref[32:32 + TT, :] = p_ref[:, 8 * G:9 * G] * _sigmoid(p_ref[:, 9 * G:10 * G])
        ed = ed_ref[...]
        for r in range(1, 8):
            sh_ref[r - 1] = pltpu.roll(ed, TT + 32 - r, 0)

    def m_conf(rb):
        base = rb * CONV_ROWS
        acc = None
        for j in range(CONF_WIDTH):
            a8, r = divmod(2 + j, 8)
            lo = base + 8 * a8
            src = ed_ref[lo:lo + CONV_ROWS, :] if r == 0 else sh_ref[r - 1, lo:lo + CONV_ROWS, :]
            term = cdw_ref[j:j + 1, :] * src
            acc = term if acc is None else acc + term
        yd = _conf_tail(acc, cb_ref[...], lng_ref[...], lnb_ref[...])
        cat_ref[base:base + CONV_ROWS, 3 * G:4 * G] = yd.astype(BF16)

    def m_conf_tail():
        oconf_ref[...] = ed_ref[TT + 2:TT + 32, :]
        ed_ref[0:32, :] = ed_ref[TT:TT + 32, :]

    def m_out():
        mix = jnp.dot(cat_ref[...], wout_ref[...], preferred_element_type=F32)
        x1_ref[slot] = x_ref[...] + _rmsnorm(mix, npost_ref[...])

    n_gu = D_FF // FFN_COLS
    gate_up = [functools.partial(f_gate_up, j) for j in range(n_gu)]
    schedule = [
        f_norm, gate_up[0], m_norm, functools.partial(m_proj, 8), functools.partial(m_proj, 9),
        gate_up[1], m_glu, functools.partial(m_proj, 3),
        gate_up[2], functools.partial(m_conf, 0), functools.partial(m_proj, 0),
        functools.partial(m_proj, 1), functools.partial(m_proj, 2),
        gate_up[3], functools.partial(m_conf, 1), functools.partial(m_proj, 4),
        functools.partial(m_proj, 5), functools.partial(m_proj, 6),
        gate_up[4], functools.partial(m_conf, 2), functools.partial(m_proj, 7),
        gate_up[5], functools.partial(m_conf, 3), m_conf_tail,
        gate_up[6], m_conv, m_pool,
        gate_up[7], m_hgrn_gates,
        gate_up[8], m_hgrn_state,
        gate_up[9], functools.partial(m_hgrn_chunk, 0),
        gate_up[10], functools.partial(m_hgrn_chunk, 1),
        functools.partial(f_down, 0), functools.partial(m_hgrn_chunk, 2),
        functools.partial(f_down, 1), functools.partial(m_hgrn_chunk, 3),
        functools.partial(f_down, 2), m_hgrn_out,
        functools.partial(f_down, 3), m_out, f_out,
    ]
    assert n_gu == 11 and n_chunks == 4 and TT // CONV_ROWS == 4
    for piece in schedule:
        piece()
    span = hg["span"]

    @pl.when(span >= FAST_DECAY_LIMIT)
    def _redo_exact():
        f = lower + (1.0 - lower) * _sigmoid(p_ref[:, 5 * G:6 * G])
        _hgrn_exact_attn(hb_ref, hq_ref, 1.0 - f, p_ref[:, 6 * G:7 * G], o_ref, 0, TT)
        yce = _head_norm_gate(o_ref[...] + oi_ref[...], p_ref[:, 7 * G:8 * G], hnorm_ref[...])
        cat_ref[:, 2 * G:3 * G] = yce.astype(BF16)
        mixe = jnp.dot(cat_ref[...], wout_ref[...], preferred_element_type=F32)
        x1_ref[slot] = x_ref[...] + _rmsnorm(mixe, npost_ref[...])

    @pl.when(t == nt - 1)
    def _state_out():
        s = st_ref[...].T
        for hh in range(HEADS):
            ohgrn_ref[hh] = s[hh * DK:(hh + 1) * DK, hh * DK:(hh + 1) * DK]


def _mix_sample_kernel(layer, NB, T, start_pos,
                       x_ref, sconv_ref, spool_ref, shgrn_ref, sconf_ref,
                       npre_ref, win_ref, convw_ref, poolbd_ref, pscale_ref, lb_ref,
                       hnorm_ref, cdw_ref, cb_ref, lng_ref, lnb_ref, wout_ref, npost_ref,
                       y_ref, oconv_ref, opool_ref, ohgrn_ref, oconf_ref,
                       p_ref, ea_ref, eb_ref, ed_ref, st_ref, hb_ref, hq_ref, o_ref, cat_ref):
    M = NB * T
    x = x_ref[...].reshape(M, D_MODEL)
    h = _rmsnorm(x, npre_ref[...]).astype(BF16)
    p_ref[...] = jnp.dot(h, win_ref[...], preferred_element_type=F32)

    cu = (p_ref[:, G:2 * G] * p_ref[:, 2 * G:3 * G]).reshape(NB, T, G)
    ea_ref[:, 6:8, :] = sconv_ref[...]
    ea_ref[:, 8:16, :] = cu
    ya = (convw_ref[2:3, :] * cu + convw_ref[1:2, :] * ea_ref[:, 7:15, :]
          + convw_ref[0:1, :] * ea_ref[:, 6:14, :])
    cat_ref[:, 0:G] = (p_ref[:, 0:G] * ya.reshape(M, G)).astype(BF16)
    oconv_ref[...] = ea_ref[:, 14:16, :]

    pp = p_ref[:, 3 * G:4 * G].reshape(NB, T, G)
    eb_ref[:, 1:16, :] = spool_ref[...]
    eb_ref[:, 16:24, :] = pp
    run = pp
    sums = {}
    for j in range(1, 16):
        run = run + eb_ref[:, 16 - j:24 - j, :]
        if j + 1 in (2, 4, 8, 16):
            sums[j + 1] = run
    pos = start_pos + lax.broadcasted_iota(jnp.int32, (NB, T, G), 1)
    mean = _pool_select(sums[2], sums[4], sums[8], sums[16], pos)
    yb = jnp.dot((mean - pp).reshape(M, G).astype(BF16), poolbd_ref[...],
                 preferred_element_type=F32)
    cat_ref[:, G:2 * G] = (yb * pscale_ref[...]).astype(BF16)
    opool_ref[...] = eb_ref[:, 9:24, :]

    lower = _lower_bound(lb_ref[...], layer)
    bd = _head_block_mask(G, G, DK, DK)

    def seq_body(n, carry):
        s0 = shgrn_ref[n]
        sbd = jnp.concatenate(
            [jnp.concatenate([s0[hh]] * HEADS, axis=1) for hh in range(HEADS)], axis=0)
        st_ref[...] = jnp.where(bd, sbd, 0.0).T
        row0 = pl.multiple_of(n * T, T)
        rows = pl.ds(row0, T)
        _hgrn_tile(p_ref[rows, 4 * G:5 * G], p_ref[rows, 5 * G:6 * G], p_ref[rows, 6 * G:7 * G],
                   lower, st_ref, hb_ref, hq_ref, o_ref, row0, T)
        s = st_ref[...].T
        for hh in range(HEADS):
            ohgrn_ref[n, hh] = s[hh * DK:(hh + 1) * DK, hh * DK:(hh + 1) * DK]
        return carry

    lax.fori_loop(0, NB, seq_body, 0)
    yc = _head_norm_gate(o_ref[...], p_ref[:, 7 * G:8 * G], hnorm_ref[...])
    cat_ref[:, 2 * G:3 * G] = yc.astype(BF16)

    u = (p_ref[:, 8 * G:9 * G] * _sigmoid(p_ref[:, 9 * G:10 * G])).reshape(NB, T, G)
    ed_ref[:, 2:32, :] = sconf_ref[...]
    ed_ref[:, 32:40, :] = u
    SB = 8
    for sb in range(NB // SB):
        sl = slice(sb * SB, (sb + 1) * SB)
        acc = cdw_ref[0:1, :] * ed_ref[sl, 2:2 + T, :]
        for j in range(1, CONF_WIDTH):
            acc = acc + cdw_ref[j:j + 1, :] * ed_ref[sl, 2 + j:2 + j + T, :]
        yd = _conf_tail(acc, cb_ref[...], lng_ref[...], lnb_ref[...])
        cat_ref[sb * SB * T:(sb + 1) * SB * T, 3 * G:4 * G] = yd.reshape(SB * T, G).astype(BF16)
    oconf_ref[...] = ed_ref[:, 10:40, :]

    mix = jnp.dot(cat_ref[...], wout_ref[...], preferred_element_type=F32)
    y_ref[...] = (x + _rmsnorm(mix, npost_ref[...])).reshape(NB, T, D_MODEL)


def _ffn_kernel(x_ref, npre_ref, wg_ref, wu_ref, wd_ref, npost_ref, y_ref):
    y_ref[...] = _ffn_block(x_ref[...], npre_ref[...], wg_ref[...], wu_ref[...], wd_ref[...],
                            npost_ref[...])


def _layer_spec(shape, layer, single_buffer=False):
    nd = len(shape)

    def imap(*_):
        return (layer,) + (0,) * nd

    if single_buffer:
        return pl.BlockSpec((None,) + tuple(shape), imap, pipeline_mode=pl.Buffered(1))
    return pl.BlockSpec((None,) + tuple(shape), imap)


def _mixer_weight_specs(layer):
    return [
        _layer_spec((1, D_MODEL), layer),
        _layer_spec((D_MODEL, D_IN), layer, True),
        _layer_spec((SC_WIDTH, G), layer),
        _layer_spec((G, G), layer),
        _layer_spec((1, G), layer),
        pl.BlockSpec((DEPTH, G), lambda *_: (0, 0)),
        _layer_spec((1, G), layer),
        _layer_spec((CONF_WIDTH, G), layer),
        _layer_spec((1, G), layer),
        _layer_spec((1, G), layer),
        _layer_spec((1, G), layer),
        _layer_spec((D_MODEL, D_MODEL), layer, True),
        _layer_spec((1, D_MODEL), layer),
    ]


def _ffn_weight_specs(layer):
    return [
        _layer_spec((1, D_MODEL), layer),
        _layer_spec((D_MODEL, D_FF), layer, True),
        _layer_spec((D_MODEL, D_FF), layer, True),
        _layer_spec((D_FF, D_MODEL), layer, True),
        _layer_spec((1, D_MODEL), layer),
    ]


def _layer_prompt(layer, x, mixer_wts, ffn_wts):
    n, seq, _ = x.shape
    TT = PROMPT_TILE
    nt = seq // TT
    last = n * nt - 1

    def mix_tile(i):
        return jnp.minimum(i, last)

    def ffn_tile(i):
        return jnp.maximum(i - 1, 0)

    out_shape = (
        jax.ShapeDtypeStruct((n, seq, D_MODEL), F32),
        jax.ShapeDtypeStruct((n, SC_WIDTH - 1, G), F32),
        jax.ShapeDtypeStruct((n, POOL_BUF, G), F32),
        jax.ShapeDtypeStruct((n, HEADS, DK, DK), F32),
        jax.ShapeDtypeStruct((n, CONF_WIDTH - 1, G), F32),
    )
    out_specs = (
        pl.BlockSpec((None, TT, D_MODEL), lambda i: (ffn_tile(i) // nt, ffn_tile(i) % nt, 0)),
        pl.BlockSpec((None, SC_WIDTH - 1, G), lambda i: (mix_tile(i) // nt, 0, 0)),
        pl.BlockSpec((None, POOL_BUF, G), lambda i: (mix_tile(i) // nt, 0, 0)),
        pl.BlockSpec((None, HEADS, DK, DK), lambda i: (mix_tile(i) // nt, 0, 0, 0)),
        pl.BlockSpec((None, CONF_WIDTH - 1, G), lambda i: (mix_tile(i) // nt, 0, 0)),
    )
    scratch = [
        pltpu.VMEM((TT, D_IN), F32),
        pltpu.VMEM((8 + TT, G), F32),
        pltpu.VMEM((16 + TT, G), F32),
        pltpu.VMEM((32 + TT, G), F32),
        pltpu.VMEM((7, 32 + TT, G), F32),
        pltpu.VMEM((G, G), F32),
        pltpu.VMEM((TT, G), F32),
        pltpu.VMEM((TT, G), F32),
        pltpu.VMEM((TT, G), F32),
        pltpu.VMEM((TT, G), F32),
        pltpu.VMEM((TT, G), F32),
        pltpu.VMEM((TT, D_MODEL), BF16),
        pltpu.VMEM((2, TT, D_MODEL), F32),
        pltpu.VMEM((TT, D_MODEL), BF16),
        pltpu.VMEM((TT, D_MODEL), BF16),
        pltpu.VMEM((TT, D_FF), BF16),
        pltpu.VMEM((TT, D_MODEL), F32),
    ]
    return pl.pallas_call(
        functools.partial(_layer_prompt_kernel, layer, TT, nt),
        grid=(n * nt + 1,),
        in_specs=[pl.BlockSpec((None, TT, D_MODEL),
                               lambda i: (mix_tile(i) // nt, mix_tile(i) % nt, 0))]
        + _mixer_weight_specs(layer) + _ffn_weight_specs(layer),
        out_specs=out_specs,
        out_shape=out_shape,
        scratch_shapes=scratch,
        compiler_params=pltpu.CompilerParams(
            dimension_semantics=("arbitrary",), vmem_limit_bytes=VMEM_LIMIT),
        name=f"layer_prompt_l{layer}",
    )(x, *mixer_wts, *ffn_wts)


def _mix_sample(layer, x, s_conv, s_pool, s_hgrn, s_conf, wts):
    n, T, _ = x.shape
    NB = SAMPLE_SEQS
    grid = (n // NB,)
    out_shape = (
        jax.ShapeDtypeStruct((n, T, D_MODEL), F32),
        jax.ShapeDtypeStruct((n, SC_WIDTH - 1, G), F32),
        jax.ShapeDtypeStruct((n, POOL_BUF, G), F32),
        jax.ShapeDtypeStruct((n, HEADS, DK, DK), F32),
        jax.ShapeDtypeStruct((n, CONF_WIDTH - 1, G), F32),
    )

    def state_spec(shape):
        nd = len(shape)
        return pl.BlockSpec((NB, None) + tuple(shape), lambda i: (i, layer) + (0,) * nd)

    def out_spec(shape):
        nd = len(shape)
        return pl.BlockSpec((NB,) + tuple(shape), lambda i: (i,) + (0,) * nd)

    in_specs = [
        pl.BlockSpec((NB, T, D_MODEL), lambda i: (i, 0, 0)),
        state_spec((SC_WIDTH - 1, G)),
        state_spec((POOL_BUF, G)),
        state_spec((HEADS, DK, DK)),
        state_spec((CONF_WIDTH - 1, G)),
    ] + _mixer_weight_specs(layer)
    out_specs = (
        out_spec((T, D_MODEL)),
        out_spec((SC_WIDTH - 1, G)),
        out_spec((POOL_BUF, G)),
        out_spec((HEADS, DK, DK)),
        out_spec((CONF_WIDTH - 1, G)),
    )
    M = NB * T
    scratch = [
        pltpu.VMEM((M, D_IN), F32),
        pltpu.VMEM((NB, 16, G), F32),
        pltpu.VMEM((NB, 24, G), F32),
        pltpu.VMEM((NB, 40, G), F32),
        pltpu.VMEM((G, G), F32),
        pltpu.VMEM((T, G), F32),
        pltpu.VMEM((T, G), F32),
        pltpu.VMEM((M, G), F32),
        pltpu.VMEM((M, D_MODEL), BF16),
    ]
    return pl.pallas_call(
        functools.partial(_mix_sample_kernel, layer, NB, T, PAST_LEN),
        grid=grid,
        in_specs=in_specs,
        out_specs=out_specs,
        out_shape=out_shape,
        scratch_shapes=scratch,
        compiler_params=pltpu.CompilerParams(
            dimension_semantics=("arbitrary",), vmem_limit_bytes=VMEM_LIMIT),
        name=f"mix_sample_l{layer}",
    )(x, s_conv, s_pool, s_hgrn, s_conf, *wts)


def _ffn(layer, x2d, ffn_wts, tag):
    m = x2d.shape[0]
    TM = FFN_TILE
    return pl.pallas_call(
        _ffn_kernel,
        grid=(m // TM,),
        in_specs=[pl.BlockSpec((TM, D_MODEL), lambda i: (i, 0))] + _ffn_weight_specs(layer),
        out_specs=pl.BlockSpec((TM, D_MODEL), lambda i: (i, 0)),
        out_shape=jax.ShapeDtypeStruct((m, D_MODEL), F32),
        compiler_params=pltpu.CompilerParams(
            dimension_semantics=("arbitrary",), vmem_limit_bytes=VMEM_LIMIT),
        name=f"ffn_{tag}_l{layer}",
    )(x2d, *ffn_wts)


def kernel(x_prompt, x_sample, state_conv, state_pool, state_hgrn, state_conf, norm_mix_pre, norm_mix_post, w_in, conv_w, pool_w, pool_scale, hgrn_lb, hgrn_norm, conf_dw, conf_b, conf_ln_g, conf_ln_b, w_out, norm_ffn_pre, norm_ffn_post, w_gate, w_up, w_down):
    def row(a):
        return a.reshape(DEPTH, 1, a.shape[-1])

    eye = jnp.eye(G // POOL_CH, dtype=pool_w.dtype)
    pool_bd = (pool_w[:, :, :, None, :] * eye[None, :, None, :, None]).reshape(DEPTH, G, G)
    mixer_wts = (row(norm_mix_pre), w_in.astype(BF16), conv_w, pool_bd.astype(BF16),
                 row(pool_scale), hgrn_lb, row(hgrn_norm), conf_dw, row(conf_b),
                 row(conf_ln_g), row(conf_ln_b), w_out.astype(BF16), row(norm_mix_post))
    ffn_wts = (row(norm_ffn_pre), w_gate.astype(BF16), w_up.astype(BF16), w_down.astype(BF16),
               row(norm_ffn_post))

    ns, ts, _ = x_sample.shape
    xp, xs = x_prompt, x_sample
    p_states, s_states = [], []
    for layer in range(DEPTH):
        xp, *stp = _layer_prompt(layer, xp, mixer_wts, ffn_wts)
        p_states.append(stp)
        xs, *sts = _mix_sample(layer, xs, state_conv, state_pool, state_hgrn, state_conf, mixer_wts)
        s_states.append(sts)
        xs = _ffn(layer, xs.reshape(ns * ts, D_MODEL), ffn_wts, "sample").reshape(ns, ts, D_MODEL)

    def stack(states, i):
        return jnp.stack([states[layer][i] for layer in range(DEPTH)], axis=1)

    return (xp, xs,
            stack(p_states, 0), stack(p_states, 1), stack(p_states, 2), stack(p_states, 3),
            stack(s_states, 0), stack(s_states, 1), stack(s_states, 2), stack(s_states, 3))
```

```python
import functools

import jax
import jax.numpy as jnp
from jax import lax
from jax.experimental import pallas as pl
from jax.experimental.pallas import tpu as pltpu

F32 = jnp.float32
BF16 = jnp.bfloat16

D_MODEL = 1024
DEPTH = 2
PAST_LEN = 16384
G = 256
N_BLOCKS = 10
D_IN = N_BLOCKS * G
SC_WIDTH = 3
POOL_BUF = 15
POOL_CH = 64
HEADS = 4
DK = 64
CONF_WIDTH = 31
D_FF = 2816
EPS = 1e-6
F_MIN = 1e-20

HGRN_CHUNK = 64
FAST_DECAY_LIMIT = 60.0
PROMPT_TILE = 256
SAMPLE_STEPS = 2
FFN_TILE = 256
FFN_COLS = 256
CONV_ROWS = 64
VMEM_LIMIT = 56 * 1024 * 1024


def _sigmoid(x):
    return jax.nn.sigmoid(x)


def _silu(x):
    return x * jax.nn.sigmoid(x)


def _rmsnorm(x, g):
    ms = jnp.mean(x * x, axis=-1, keepdims=True)
    return x * lax.rsqrt(ms + EPS) * g


def _head_block_mask(rows, cols, row_block, col_block):
    r = lax.broadcasted_iota(jnp.int32, (rows, cols), 0) // row_block
    c = lax.broadcasted_iota(jnp.int32, (rows, cols), 1) // col_block
    return r == c


def _cumsum_rows_mxu(x):
    n = x.shape[0]
    tri = (lax.broadcasted_iota(jnp.int32, (n, n), 0)
           >= lax.broadcasted_iota(jnp.int32, (n, n), 1))
    tri = jnp.where(tri, 1.0, 0.0).astype(BF16)
    hi = x.astype(BF16)
    r1 = x - hi.astype(F32)
    mid = r1.astype(BF16)
    lo = (r1 - mid.astype(F32)).astype(BF16)
    return (jnp.dot(tri, hi, preferred_element_type=F32)
            + jnp.dot(tri, mid, preferred_element_type=F32)
            + jnp.dot(tri, lo, preferred_element_type=F32))


def _lower_bound(lb_all, layer):
    m = jnp.max(lb_all, axis=0, keepdims=True)
    e = jnp.exp(lb_all - m)
    sm = e / jnp.sum(e, axis=0, keepdims=True)
    cs = sm[0:1]
    for i in range(1, layer + 1):
        cs = cs + sm[i:i + 1]
    return cs - sm[0:1]


def _hgrn_gates(zq, zf, zi, lower):
    q = _silu(zq)
    f = lower + (1.0 - lower) * _sigmoid(zf)
    logf = jnp.log(jnp.maximum(f, F_MIN))
    return q, 1.0 - f, zi, logf


def _hgrn_state_terms(q, kk, v, b, st_ref):
    TT = q.shape[0]
    b_end = b[TT - 1:TT, :]
    st = st_ref[...]
    qs = (q * jnp.exp(b)).astype(BF16)
    o_inter = lax.dot_general(qs, st.astype(BF16), (((1,), (1,)), ((), ())),
                              preferred_element_type=F32)
    kh = (kk * jnp.exp(b_end - b)).astype(BF16)
    upd = lax.dot_general(v.astype(BF16), kh, (((0,), (0,)), ((), ())),
                          preferred_element_type=F32)
    bd = _head_block_mask(G, G, DK, DK)
    st_ref[...] = st * jnp.exp(b_end) + jnp.where(bd, upd, 0.0)
    return o_inter


def _hgrn_refs(b, C):
    refs = []
    span = None
    for j in range(b.shape[0] // C):
        first = b[j * C:j * C + 1, :]
        last = b[(j + 1) * C - 1:(j + 1) * C, :]
        refs.append(0.5 * (first + last))
        half = jnp.max(0.5 * (first - last))
        span = half if span is None else jnp.maximum(span, half)
    return refs, span


def _hgrn_fast_chunk(q_tgt, b_tgt, kk_src, v_src, r, C):
    nt = q_tgt.shape[0]
    rows_mask = _head_block_mask(HEADS * C, G, C, DK)
    qz = (q_tgt * jnp.exp(b_tgt - r)).astype(BF16)
    ke = kk_src * jnp.exp(r - b_tgt[0:C])
    kebd = jnp.where(rows_mask, jnp.concatenate([ke] * HEADS, axis=0), 0.0).astype(BF16)
    attn = lax.dot_general(qz, kebd, (((1,), (1,)), ((), ())),
                           preferred_element_type=F32)
    t_idx = lax.broadcasted_iota(jnp.int32, (nt, HEADS * C), 0)
    s_idx = lax.broadcasted_iota(jnp.int32, (nt, HEADS * C), 1) % C
    attn = jnp.where(t_idx >= s_idx, attn, 0.0).astype(BF16)
    vbd = jnp.where(rows_mask, jnp.concatenate([v_src] * HEADS, axis=0), 0.0).astype(BF16)
    return jnp.dot(attn, vbd, preferred_element_type=F32)


def _hgrn_exact_attn(hb_ref, hq_ref, kk, v, o_ref, row0, TT):
    b = hb_ref[0:TT, :]
    ones_bd = jnp.where(_head_block_mask(G, G, DK, DK), 1.0, 0.0).astype(BF16)
    s_row = lax.broadcasted_iota(jnp.int32, (TT, G), 0)

    def body(t, carry):
        bt = hb_ref[pl.ds(t, 1), :]
        qt = hq_ref[pl.ds(t, 1), :]
        e = jnp.where(s_row <= t, qt * kk * jnp.exp(jnp.minimum(bt - b, 0.0)), 0.0)
        a = jnp.dot(e.astype(BF16), ones_bd, preferred_element_type=F32)
        o_ref[pl.ds(row0 + t, 1), :] = jnp.sum(a * v, axis=0, keepdims=True)
        return carry

    lax.fori_loop(0, TT, body, 0)


def _head_norm_gate(o, zg, hnorm):
    ones_bd = jnp.where(_head_block_mask(G, G, DK, DK), 1.0, 0.0).astype(BF16)
    o2 = o * o
    hi = o2.astype(BF16)
    lo = (o2 - hi.astype(F32)).astype(BF16)
    ssq = (jnp.dot(hi, ones_bd, preferred_element_type=F32)
           + jnp.dot(lo, ones_bd, preferred_element_type=F32))
    return o * lax.rsqrt(ssq * (1.0 / DK) + EPS) * hnorm * _silu(zg)


def _pool_select(s2, s4, s8, s16, pos):
    shape = s2.shape
    grp = lax.broadcasted_iota(jnp.int32, shape, len(shape) - 1) // POOL_CH
    ssum = jnp.where(grp == 0, s2, jnp.where(grp == 1, s4, jnp.where(grp == 2, s8, s16)))
    win = jnp.where(grp == 0, 2, jnp.where(grp == 1, 4, jnp.where(grp == 2, 8, 16)))
    cnt = jnp.minimum(pos + 1, win).astype(F32)
    return ssum / cnt


def _conf_tail(z, cb, lng, lnb):
    z = z + cb
    mu = jnp.mean(z, axis=-1, keepdims=True)
    zc = z - mu
    var = jnp.mean(zc * zc, axis=-1, keepdims=True)
    return _silu(zc * lax.rsqrt(var + EPS) * lng + lnb)


def _ffn_block(x, npre, wg, wu, wd, npost):
    h = _rmsnorm(x, npre).astype(BF16)
    g = jnp.dot(h, wg, preferred_element_type=F32)
    u = jnp.dot(h, wu, preferred_element_type=F32)
    a = (_silu(g) * u).astype(BF16)
    ff = jnp.dot(a, wd, preferred_element_type=F32)
    return x + _rmsnorm(ff, npost)


def _layer_prompt_kernel(layer, TT, nt,
                         x_ref, npre_ref, win_ref, convw_ref, poolbd_ref, pscale_ref, lb_ref,
                         hnorm_ref, cdw_ref, cb_ref, lng_ref, lnb_ref, wout_ref, npost_ref,
                         fpre_ref, wg_ref, wu_ref, wd_ref, fpost_ref,
                         y_ref, oconv_ref, opool_ref, ohgrn_ref, oconf_ref,
                         p_ref, ea_ref, eb_ref, ed_ref, sh_ref, st_ref, hb_ref, hq_ref, hk_ref, oi_ref,
                         o_ref, cat_ref, x1_ref, hm_ref, hf_ref, a_ref, ff_ref):
    i = pl.program_id(0)
    t = i % nt
    slot = i % 2

    @pl.when(i == 0)
    def _first():
        x1_ref[1] = jnp.zeros((TT, D_MODEL), F32)

    @pl.when(t == 0)
    def _new_sequence():
        ea_ref[0:8, :] = jnp.zeros((8, G), F32)
        eb_ref[0:16, :] = jnp.zeros((16, G), F32)
        ed_ref[0:32, :] = jnp.zeros((32, G), F32)
        st_ref[...] = jnp.zeros((G, G), F32)

    lower = _lower_bound(lb_ref[...], layer)
    n_chunks = TT // HGRN_CHUNK

    def f_norm():
        hf_ref[...] = _rmsnorm(x1_ref[1 - slot], fpre_ref[...]).astype(BF16)

    def f_gate_up(j):
        cols = slice(j * FFN_COLS, (j + 1) * FFN_COLS)
        hf = hf_ref[...]
        g = jnp.dot(hf, wg_ref[:, cols], preferred_element_type=F32)
        u = jnp.dot(hf, wu_ref[:, cols], preferred_element_type=F32)
        a_ref[:, cols] = (_silu(g) * u).astype(BF16)

    def f_down(k):
        cols = slice(k * G, (k + 1) * G)
        ff_ref[:, cols] = jnp.dot(a_ref[...], wd_ref[:, cols], preferred_element_type=F32)

    def f_out():
        y_ref[...] = x1_ref[1 - slot] + _rmsnorm(ff_ref[...], fpost_ref[...])

    def m_norm():
        hm_ref[...] = _rmsnorm(x_ref[...], npre_ref[...]).astype(BF16)

    def m_proj(blk):
        cols = slice(blk * G, (blk + 1) * G)
        p_ref[:, cols] = jnp.dot(hm_ref[...], win_ref[:, cols], preferred_element_type=F32)

    def m_conv():
        cu = p_ref[:, G:2 * G] * p_ref[:, 2 * G:3 * G]
        ea_ref[8:8 + TT, :] = cu
        ya = (convw_ref[2:3, :] * cu + convw_ref[1:2, :] * ea_ref[7:7 + TT, :]
              + convw_ref[0:1, :] * ea_ref[6:6 + TT, :])
        cat_ref[:, 0:G] = (p_ref[:, 0:G] * ya).astype(BF16)
        oconv_ref[...] = ea_ref[TT + 6:TT + 8, :]
        ea_ref[0:8, :] = ea_ref[TT:TT + 8, :]

    def m_pool():
        pp = p_ref[:, 3 * G:4 * G]
        eb_ref[16:16 + TT, :] = pp
        e = eb_ref[...]
        s2 = e + pltpu.roll(e, 1, 0)
        s4 = s2 + pltpu.roll(s2, 2, 0)
        s8 = s4 + pltpu.roll(s4, 4, 0)
        s16 = s8 + pltpu.roll(s8, 8, 0)
        pos = t * TT + lax.broadcasted_iota(jnp.int32, (TT, G), 0)
        mean = _pool_select(s2[16:], s4[16:], s8[16:], s16[16:], pos)
        yb = jnp.dot((mean - pp).astype(BF16), poolbd_ref[...], preferred_element_type=F32)
        cat_ref[:, G:2 * G] = (yb * pscale_ref[...]).astype(BF16)
        opool_ref[...] = eb_ref[TT + 1:TT + 16, :]
        eb_ref[0:16, :] = eb_ref[TT:TT + 16, :]

    hg = {}

    def m_hgrn_gates():
        q, kk, _, logf = _hgrn_gates(p_ref[:, 4 * G:5 * G], p_ref[:, 5 * G:6 * G],
                                     p_ref[:, 6 * G:7 * G], lower)
        hq_ref[...] = q
        hk_ref[...] = kk
        hb_ref[...] = _cumsum_rows_mxu(logf)

    def m_hgrn_state():
        b = hb_ref[...]
        o_inter = _hgrn_state_terms(hq_ref[...], hk_ref[...], p_ref[:, 6 * G:7 * G], b, st_ref)
        oi_ref[...] = o_inter
        o_ref[...] = o_inter
        hg["refs"], hg["span"] = _hgrn_refs(b, HGRN_CHUNK)

    def m_hgrn_chunk(j):
        lo, hi = j * HGRN_CHUNK, (j + 1) * HGRN_CHUNK
        contrib = _hgrn_fast_chunk(hq_ref[lo:TT, :], hb_ref[lo:TT, :], hk_ref[lo:hi, :],
                                   p_ref[lo:hi, 6 * G:7 * G], hg["refs"][j], HGRN_CHUNK)
        o_ref[lo:TT, :] = o_ref[lo:TT, :] + contrib

    def m_hgrn_out():
        yc = _head_norm_gate(o_ref[...], p_ref[:, 7 * G:8 * G], hnorm_ref[...])
        cat_ref[:, 2 * G:3 * G] = yc.astype(BF16)

    def m_glu():
        ed_ref[32:32 + TT, :] = p_ref[:, 8 * G:9 * G] * _sigmoid(p_ref[:, 9 * G:10 * G])
        ed = ed_ref[...]
        for r in range(1, 8):
            sh_ref[r - 1] = pltpu.roll(ed, TT + 32 - r, 0)

    def m_conf(rb):
        base = rb * CONV_ROWS
        acc = None
        for j in range(CONF_WIDTH):
            a8, r = divmod(2 + j, 8)
            lo = base + 8 * a8
            src = ed_ref[lo:lo + CONV_ROWS, :] if r == 0 else sh_ref[r - 1, lo:lo + CONV_ROWS, :]
            term = cdw_ref[j:j + 1, :] * src
            acc = term if acc is None else acc + term
        yd = _conf_tail(acc, cb_ref[...], lng_ref[...], lnb_ref[...])
        cat_ref[base:base + CONV_ROWS, 3 * G:4 * G] = yd.astype(BF16)

    def m_conf_tail():
        oconf_ref[...] = ed_ref[TT + 2:TT + 32, :]
        ed_ref[0:32, :] = ed_ref[TT:TT + 32, :]

    def m_out():
        mix = jnp.dot(cat_ref[...], wout_ref[...], preferred_element_type=F32)
        x1_ref[slot] = x_ref[...] + _rmsnorm(mix, npost_ref[...])

    n_gu = D_FF // FFN_COLS
    gate_up = [functools.partial(f_gate_up, j) for j in range(n_gu)]
    schedule = [
        f_norm, gate_up[0], m_norm, functools.partial(m_proj, 8), functools.partial(m_proj, 9),
        gate_up[1], m_glu, functools.partial(m_proj, 3),
        gate_up[2], functools.partial(m_conf, 0), functools.partial(m_proj, 0),
        functools.partial(m_proj, 1), functools.partial(m_proj, 2),
        gate_up[3], functools.partial(m_conf, 1), functools.partial(m_proj, 4),
        functools.partial(m_proj, 5), functools.partial(m_proj, 6),
        gate_up[4], functools.partial(m_conf, 2), functools.partial(m_proj, 7),
        gate_up[5], functools.partial(m_conf, 3), m_conf_tail,
        gate_up[6], m_conv, m_pool,
        gate_up[7], m_hgrn_gates,
        gate_up[8], m_hgrn_state,
        gate_up[9], functools.partial(m_hgrn_chunk, 0),
        gate_up[10], functools.partial(m_hgrn_chunk, 1),
        functools.partial(f_down, 0), functools.partial(m_hgrn_chunk, 2),
        functools.partial(f_down, 1), functools.partial(m_hgrn_chunk, 3),
        functools.partial(f_down, 2), m_hgrn_out,
        functools.partial(f_down, 3), m_out, f_out,
    ]
    assert n_gu == 11 and n_chunks == 4 and TT // CONV_ROWS == 4
    for piece in schedule:
        piece()
    span = hg["span"]

    @pl.when(span >= FAST_DECAY_LIMIT)
    def _redo_exact():
        f = lower + (1.0 - lower) * _sigmoid(p_ref[:, 5 * G:6 * G])
        _hgrn_exact_attn(hb_ref, hq_ref, 1.0 - f, p_ref[:, 6 * G:7 * G], o_ref, 0, TT)
        yce = _head_norm_gate(o_ref[...] + oi_ref[...], p_ref[:, 7 * G:8 * G], hnorm_ref[...])
        cat_ref[:, 2 * G:3 * G] = yce.astype(BF16)
        mixe = jnp.dot(cat_ref[...], wout_ref[...], preferred_element_type=F32)
        x1_ref[slot] = x_ref[...] + _rmsnorm(mixe, npost_ref[...])

    @pl.when(t == nt - 1)
    def _state_out():
        s = st_ref[...].T
        for hh in range(HEADS):
            ohgrn_ref[hh] = s[hh * DK:(hh + 1) * DK, hh * DK:(hh + 1) * DK]


def _mix_sample_kernel(layer, NS, TS, start_pos,
                       x_ref, sconv_ref, spool_ref, shgrn_ref, sconf_ref,
                       npre_ref, win_ref, convw_ref, poolbd_ref, pscale_ref, lb_ref,
                       hnorm_ref, cdw_ref, cb_ref, lng_ref, lnb_ref, wout_ref, npost_ref,
                       y_ref, oconv_ref, opool_ref, ohgrn_ref, oconf_ref,
                       p_ref, u_ref, pool_ref, qT_ref, fT_ref, kT_ref, vT_ref, oT_ref, cat_ref):
    i = pl.program_id(0)

    @pl.when(i == 0)
    def _load_states():
        oconv_ref[...] = sconv_ref[...]
        opool_ref[...] = spool_ref[...]
        oconf_ref[...] = sconf_ref[...]
        ohgrn_ref[...] = shgrn_ref[...]

    def slab(t):
        return slice(t * NS, (t + 1) * NS)

    x = x_ref[...]
    h = _rmsnorm(x, npre_ref[...]).astype(BF16)
    p_ref[...] = jnp.dot(h, win_ref[...], preferred_element_type=F32)

    def conv_in(j):
        if j < SC_WIDTH - 1:
            return oconv_ref[j]
        rows = slab(j - (SC_WIDTH - 1))
        return p_ref[rows, G:2 * G] * p_ref[rows, 2 * G:3 * G]

    for t in range(TS):
        ya = (convw_ref[0:1, :] * conv_in(t) + convw_ref[1:2, :] * conv_in(t + 1)
              + convw_ref[2:3, :] * conv_in(t + 2))
        cat_ref[slab(t), 0:G] = (p_ref[slab(t), 0:G] * ya).astype(BF16)
    for j in range(SC_WIDTH - 1):
        oconv_ref[j] = conv_in(j + TS)

    def pool_in(j):
        if j < POOL_BUF:
            return opool_ref[j]
        return p_ref[slab(j - POOL_BUF), 3 * G:4 * G]

    for t in range(TS):
        idx = POOL_BUF + t
        run = pool_in(idx)
        sums = {}
        for j in range(1, 16):
            run = run + pool_in(idx - j)
            if j + 1 in (2, 4, 8, 16):
                sums[j + 1] = run
        pos = jnp.full((NS, G), start_pos + i * TS + t, jnp.int32)
        mean = _pool_select(sums[2], sums[4], sums[8], sums[16], pos)
        pool_ref[slab(t), :] = mean - pool_in(idx)
    yb = jnp.dot(pool_ref[...].astype(BF16), poolbd_ref[...], preferred_element_type=F32)
    cat_ref[:, G:2 * G] = (yb * pscale_ref[...]).astype(BF16)
    for j in range(POOL_BUF):
        opool_ref[j] = pool_in(j + TS)

    lower = _lower_bound(lb_ref[...], layer)
    q = _silu(p_ref[:, 4 * G:5 * G])
    f = lower + (1.0 - lower) * _sigmoid(p_ref[:, 5 * G:6 * G])
    for t in range(TS):
        qT_ref[t] = q[slab(t)].T
        fT_ref[t] = jnp.maximum(f[slab(t)], F_MIN).T
        kT_ref[t] = (1.0 - f[slab(t)]).T
        vT_ref[t] = p_ref[slab(t), 6 * G:7 * G].T
    for hh in range(HEADS):
        head = slice(hh * DK, (hh + 1) * DK)
        vts = [vT_ref[t, head, :] for t in range(TS)]

        def body(k, accs, hh=hh, vts=vts):
            c = hh * DK + k
            s = ohgrn_ref[c]
            out = []
            for t in range(TS):
                s = fT_ref[t, pl.ds(c, 1), :] * s + kT_ref[t, pl.ds(c, 1), :] * vts[t]
                out.append(accs[t] + qT_ref[t, pl.ds(c, 1), :] * s)
            ohgrn_ref[c] = s
            return tuple(out)

        accs = lax.fori_loop(0, DK, body, tuple(jnp.zeros((DK, NS), F32) for _ in range(TS)),
                             unroll=2)
        for t in range(TS):
            oT_ref[t, head, :] = accs[t]
    o = jnp.concatenate([oT_ref[t].T for t in range(TS)], axis=0)
    yc = _head_norm_gate(o, p_ref[:, 7 * G:8 * G], hnorm_ref[...])
    cat_ref[:, 2 * G:3 * G] = yc.astype(BF16)

    u_ref[...] = p_ref[:, 8 * G:9 * G] * _sigmoid(p_ref[:, 9 * G:10 * G])
    HALF = NS // 2

    def conf_in(j, rows):
        if j < CONF_WIDTH - 1:
            return oconf_ref[j, rows, :]
        base = (j - (CONF_WIDTH - 1)) * NS
        return u_ref[base + rows.start:base + rows.stop, :]

    for t in range(TS):
        for hf in range(2):
            rows = slice(hf * HALF, (hf + 1) * HALF)
            acc = None
            for j in range(CONF_WIDTH):
                term = cdw_ref[j:j + 1, :] * conf_in(t + j, rows)
                acc = term if acc is None else acc + term
            yd = _conf_tail(acc, cb_ref[...], lng_ref[...], lnb_ref[...])
            cat_ref[t * NS + hf * HALF:t * NS + (hf + 1) * HALF, 3 * G:4 * G] = yd.astype(BF16)
    for j in range(CONF_WIDTH - 1):
        oconf_ref[j] = conf_in(j + TS, slice(0, NS))

    mix = jnp.dot(cat_ref[...], wout_ref[...], preferred_element_type=F32)
    y_ref[...] = x + _rmsnorm(mix, npost_ref[...])


def _ffn_kernel(x_ref, npre_ref, wg_ref, wu_ref, wd_ref, npost_ref, y_ref):
    y_ref[...] = _ffn_block(x_ref[...], npre_ref[...], wg_ref[...], wu_ref[...], wd_ref[...],
                            npost_ref[...])


def _layer_spec(shape, layer, single_buffer=False):
    nd = len(shape)

    def imap(*_):
        return (layer,) + (0,) * nd

    if single_buffer:
        return pl.BlockSpec((None,) + tuple(shape), imap, pipeline_mode=pl.Buffered(1))
    return pl.BlockSpec((None,) + tuple(shape), imap)


def _mixer_weight_specs(layer):
    return [
        _layer_spec((1, D_MODEL), layer),
        _layer_spec((D_MODEL, D_IN), layer, True),
        _layer_spec((SC_WIDTH, G), layer),
        _layer_spec((G, G), layer),
        _layer_spec((1, G), layer),
        pl.BlockSpec((DEPTH, G), lambda *_: (0, 0)),
        _layer_spec((1, G), layer),
        _layer_spec((CONF_WIDTH, G), layer),
        _layer_spec((1, G), layer),
        _layer_spec((1, G), layer),
        _layer_spec((1, G), layer),
        _layer_spec((D_MODEL, D_MODEL), layer, True),
        _layer_spec((1, D_MODEL), layer),
    ]


def _ffn_weight_specs(layer):
    return [
        _layer_spec((1, D_MODEL), layer),
        _layer_spec((D_MODEL, D_FF), layer, True),
        _layer_spec((D_MODEL, D_FF), layer, True),
        _layer_spec((D_FF, D_MODEL), layer, True),
        _layer_spec((1, D_MODEL), layer),
    ]


def _layer_prompt(layer, x, mixer_wts, ffn_wts):
    n, seq, _ = x.shape
    TT = PROMPT_TILE
    nt = seq // TT
    last = n * nt - 1

    def mix_tile(i):
        return jnp.minimum(i, last)

    def ffn_tile(i):
        return jnp.maximum(i - 1, 0)

    out_shape = (
        jax.ShapeDtypeStruct((n, seq, D_MODEL), F32),
        jax.ShapeDtypeStruct((n, SC_WIDTH - 1, G), F32),
        jax.ShapeDtypeStruct((n, POOL_BUF, G), F32),
        jax.ShapeDtypeStruct((n, HEADS, DK, DK), F32),
        jax.ShapeDtypeStruct((n, CONF_WIDTH - 1, G), F32),
    )
    out_specs = (
        pl.BlockSpec((None, TT, D_MODEL), lambda i: (ffn_tile(i) // nt, ffn_tile(i) % nt, 0)),
        pl.BlockSpec((None, SC_WIDTH - 1, G), lambda i: (mix_tile(i) // nt, 0, 0)),
        pl.BlockSpec((None, POOL_BUF, G), lambda i: (mix_tile(i) // nt, 0, 0)),
        pl.BlockSpec((None, HEADS, DK, DK), lambda i: (mix_tile(i) // nt, 0, 0, 0)),
        pl.BlockSpec((None, CONF_WIDTH - 1, G), lambda i: (mix_tile(i) // nt, 0, 0)),
    )
    scratch = [
        pltpu.VMEM((TT, D_IN), F32),
        pltpu.VMEM((8 + TT, G), F32),
        pltpu.VMEM((16 + TT, G), F32),
        pltpu.VMEM((32 + TT, G), F32),
        pltpu.VMEM((7, 32 + TT, G), F32),
        pltpu.VMEM((G, G), F32),
        pltpu.VMEM((TT, G), F32),
        pltpu.VMEM((TT, G), F32),
        pltpu.VMEM((TT, G), F32),
        pltpu.VMEM((TT, G), F32),
        pltpu.VMEM((TT, G), F32),
        pltpu.VMEM((TT, D_MODEL), BF16),
        pltpu.VMEM((2, TT, D_MODEL), F32),
        pltpu.VMEM((TT, D_MODEL), BF16),
        pltpu.VMEM((TT, D_MODEL), BF16),
        pltpu.VMEM((TT, D_FF), BF16),
        pltpu.VMEM((TT, D_MODEL), F32),
    ]
    return pl.pallas_call(
        functools.partial(_layer_prompt_kernel, layer, TT, nt),
        grid=(n * nt + 1,),
        in_specs=[pl.BlockSpec((None, TT, D_MODEL),
                               lambda i: (mix_tile(i) // nt, mix_tile(i) % nt, 0))]
        + _mixer_weight_specs(layer) + _ffn_weight_specs(layer),
        out_specs=out_specs,
        out_shape=out_shape,
        scratch_shapes=scratch,
        compiler_params=pltpu.CompilerParams(
            dimension_semantics=("arbitrary",), vmem_limit_bytes=VMEM_LIMIT),
        name=f"layer_prompt_l{layer}",
    )(x, *mixer_wts, *ffn_wts)


def _mix_sample(layer, x2d, s_conv, s_pool, s_hgrn, s_conf, wts, n_seq):
    m = x2d.shape[0]
    NS = n_seq
    TS = SAMPLE_STEPS
    M = TS * NS

    def state_spec(shape):
        nd = len(shape)
        return pl.BlockSpec((None,) + tuple(shape), lambda i: (layer,) + (0,) * nd,
                            pipeline_mode=pl.Buffered(1))

    def out_spec(shape):
        nd = len(shape)
        return pl.BlockSpec(tuple(shape), lambda i: (0,) * nd, pipeline_mode=pl.Buffered(1))

    state_shapes = [(SC_WIDTH - 1, NS, G), (POOL_BUF, NS, G), (G, DK, NS), (CONF_WIDTH - 1, NS, G)]
    in_specs = ([pl.BlockSpec((M, D_MODEL), lambda i: (i, 0))]
                + [state_spec(s) for s in state_shapes] + _mixer_weight_specs(layer))
    out_specs = tuple([pl.BlockSpec((M, D_MODEL), lambda i: (i, 0))]
                      + [out_spec(s) for s in state_shapes])
    out_shape = tuple([jax.ShapeDtypeStruct((m, D_MODEL), F32)]
                      + [jax.ShapeDtypeStruct(s, F32) for s in state_shapes])
    scratch = [
        pltpu.VMEM((M, D_IN), F32),
        pltpu.VMEM((M, G), F32),
        pltpu.VMEM((M, G), F32),
        pltpu.VMEM((TS, G, NS), F32),
        pltpu.VMEM((TS, G, NS), F32),
        pltpu.VMEM((TS, G, NS), F32),
        pltpu.VMEM((TS, G, NS), F32),
        pltpu.VMEM((TS, G, NS), F32),
        pltpu.VMEM((M, D_MODEL), BF16),
    ]
    return pl.pallas_call(
        functools.partial(_mix_sample_kernel, layer, NS, TS, PAST_LEN),
        grid=(m // M,),
        in_specs=in_specs,
        out_specs=out_specs,
        out_shape=out_shape,
        scratch_shapes=scratch,
        compiler_params=pltpu.CompilerParams(
            dimension_semantics=("arbitrary",), vmem_limit_bytes=VMEM_LIMIT),
        name=f"mix_sample_l{layer}",
    )(x2d, s_conv, s_pool, s_hgrn, s_conf, *wts)


def _ffn(layer, x2d, ffn_wts, tag):
    m = x2d.shape[0]
    TM = FFN_TILE
    return pl.pallas_call(
        _ffn_kernel,
        grid=(m // TM,),
        in_specs=[pl.BlockSpec((TM, D_MODEL), lambda i: (i, 0))] + _ffn_weight_specs(layer),
        out_specs=pl.BlockSpec((TM, D_MODEL), lambda i: (i, 0)),
        out_shape=jax.ShapeDtypeStruct((m, D_MODEL), F32),
        compiler_params=pltpu.CompilerParams(
            dimension_semantics=("arbitrary",), vmem_limit_bytes=VMEM_LIMIT),
        name=f"ffn_{tag}_l{layer}",
    )(x2d, *ffn_wts)


def kernel(x_prompt, x_sample, state_conv, state_pool, state_hgrn, state_conf, norm_mix_pre, norm_mix_post, w_in, conv_w, pool_w, pool_scale, hgrn_lb, hgrn_norm, conf_dw, conf_b, conf_ln_g, conf_ln_b, w_out, norm_ffn_pre, norm_ffn_post, w_gate, w_up, w_down):
    def row(a):
        return a.reshape(DEPTH, 1, a.shape[-1])

    eye = jnp.eye(G // POOL_CH, dtype=pool_w.dtype)
    pool_bd = (pool_w[:, :, :, None, :] * eye[None, :, None, :, None]).reshape(DEPTH, G, G)
    mixer_wts = (row(norm_mix_pre), w_in.astype(BF16), conv_w, pool_bd.astype(BF16),
                 row(pool_scale), hgrn_lb, row(hgrn_norm), conf_dw, row(conf_b),
                 row(conf_ln_g), row(conf_ln_b), w_out.astype(BF16), row(norm_mix_post))
    ffn_wts = (row(norm_ffn_pre), w_gate.astype(BF16), w_up.astype(BF16), w_down.astype(BF16),
               row(norm_ffn_post))

    ns, ts, _ = x_sample.shape
    xs = x_sample.transpose(1, 0, 2).reshape(ts * ns, D_MODEL)
    sc_t = state_conv.transpose(1, 2, 0, 3)
    sp_t = state_pool.transpose(1, 2, 0, 3)
    sf_t = state_conf.transpose(1, 2, 0, 3)
    sh_t = state_hgrn.transpose(1, 2, 3, 4, 0).reshape(DEPTH, G, DK, ns)
    xp = x_prompt
    p_states, s_states = [], []
    for layer in range(DEPTH):
        xp, *stp = _layer_prompt(layer, xp, mixer_wts, ffn_wts)
        p_states.append(stp)
        xs, *sts = _mix_sample(layer, xs, sc_t, sp_t, sh_t, sf_t, mixer_wts, ns)
        s_states.append(sts)
        xs = _ffn(layer, xs, ffn_wts, "sample")
    xs = xs.reshape(ts, ns, D_MODEL).transpose(1, 0, 2)

    def stack(states, i):
        return jnp.stack([states[layer][i] for layer in range(DEPTH)], axis=1)

    def stack_t(i):
        return jnp.stack([s_states[layer][i] for layer in range(DEPTH)], axis=0).transpose(2, 0, 1, 3)

    hgrn_s = jnp.stack([s_states[layer][2] for layer in range(DEPTH)], axis=0)
    hgrn_s = hgrn_s.reshape(DEPTH, HEADS, DK, DK, ns).transpose(4, 0, 1, 2, 3)

    return (xp, xs,
            stack(p_states, 0), stack(p_states, 1), stack(p_states, 2), stack(p_states, 3),
            stack_t(0), stack_t(1), hgrn_s, stack_t(3))
```

```python
import functools

import jax
import jax.numpy as jnp
from jax import lax
from jax.experimental import pallas as pl
from jax.experimental.pallas import tpu as pltpu

F32 = jnp.float32
BF16 = jnp.bfloat16

D_MODEL = 1024
DEPTH = 2
PAST_LEN = 16384
G = 256
N_BLOCKS = 10
D_IN = N_BLOCKS * G
SC_WIDTH = 3
POOL_BUF = 15
POOL_CH = 64
HEADS = 4
DK = 64
CONF_WIDTH = 31
D_FF = 2816
EPS = 1e-6
F_MIN = 1e-20

HGRN_CHUNK = 64
FAST_DECAY_LIMIT = 60.0
PROMPT_TILE = 256
SAMPLE_STEPS = 2
FFN_TILE = 256
FFN_COLS = 256
CONV_ROWS = 64
VMEM_LIMIT = 56 * 1024 * 1024


def _sigmoid(x):
    return jax.nn.sigmoid(x)


def _silu(x):
    return x * jax.nn.sigmoid(x)


def _rmsnorm(x, g):
    ms = jnp.mean(x * x, axis=-1, keepdims=True)
    return x * lax.rsqrt(ms + EPS) * g


def _head_block_mask(rows, cols, row_block, col_block):
    r = lax.broadcasted_iota(jnp.int32, (rows, cols), 0) // row_block
    c = lax.broadcasted_iota(jnp.int32, (rows, cols), 1) // col_block
    return r == c


def _cumsum_rows_mxu(x):
    n = x.shape[0]
    tri = (lax.broadcasted_iota(jnp.int32, (n, n), 0)
           >= lax.broadcasted_iota(jnp.int32, (n, n), 1))
    tri = jnp.where(tri, 1.0, 0.0).astype(BF16)
    hi = x.astype(BF16)
    r1 = x - hi.astype(F32)
    mid = r1.astype(BF16)
    lo = (r1 - mid.astype(F32)).astype(BF16)
    return (jnp.dot(tri, hi, preferred_element_type=F32)
            + jnp.dot(tri, mid, preferred_element_type=F32)
            + jnp.dot(tri, lo, preferred_element_type=F32))


def _lower_bound(lb_all, layer):
    m = jnp.max(lb_all, axis=0, keepdims=True)
    e = jnp.exp(lb_all - m)
    sm = e / jnp.sum(e, axis=0, keepdims=True)
    cs = sm[0:1]
    for i in range(1, layer + 1):
        cs = cs + sm[i:i + 1]
    return cs - sm[0:1]


def _hgrn_gates(zq, zf, zi, lower):
    q = _silu(zq)
    f = lower + (1.0 - lower) * _sigmoid(zf)
    logf = jnp.log(jnp.maximum(f, F_MIN))
    return q, 1.0 - f, zi, logf


def _hgrn_state_terms(q, kk, v, b, st_ref):
    TT = q.shape[0]
    b_end = b[TT - 1:TT, :]
    st = st_ref[...]
    qs = (q * jnp.exp(b)).astype(BF16)
    o_inter = lax.dot_general(qs, st.astype(BF16), (((1,), (1,)), ((), ())),
                              preferred_element_type=F32)
    kh = (kk * jnp.exp(b_end - b)).astype(BF16)
    upd = lax.dot_general(v.astype(BF16), kh, (((0,), (0,)), ((), ())),
                          preferred_element_type=F32)
    bd = _head_block_mask(G, G, DK, DK)
    st_ref[...] = st * jnp.exp(b_end) + jnp.where(bd, upd, 0.0)
    return o_inter


def _hgrn_refs(b, C):
    refs = []
    span = None
    for j in range(b.shape[0] // C):
        first = b[j * C:j * C + 1, :]
        last = b[(j + 1) * C - 1:(j + 1) * C, :]
        refs.append(0.5 * (first + last))
        half = jnp.max(0.5 * (first - last))
        span = half if span is None else jnp.maximum(span, half)
    return refs, span


def _hgrn_fast_chunk(q_tgt, b_tgt, kk_src, v_src, r, C):
    nt = q_tgt.shape[0]
    rows_mask = _head_block_mask(HEADS * C, G, C, DK)
    qz = (q_tgt * jnp.exp(b_tgt - r)).astype(BF16)
    ke = kk_src * jnp.exp(r - b_tgt[0:C])
    kebd = jnp.where(rows_mask, jnp.concatenate([ke] * HEADS, axis=0), 0.0).astype(BF16)
    attn = lax.dot_general(qz, kebd, (((1,), (1,)), ((), ())),
                           preferred_element_type=F32)
    t_idx = lax.broadcasted_iota(jnp.int32, (nt, HEADS * C), 0)
    s_idx = lax.broadcasted_iota(jnp.int32, (nt, HEADS * C), 1) % C
    attn = jnp.where(t_idx >= s_idx, attn, 0.0).astype(BF16)
    vbd = jnp.where(rows_mask, jnp.concatenate([v_src] * HEADS, axis=0), 0.0).astype(BF16)
    return jnp.dot(attn, vbd, preferred_element_type=F32)


def _hgrn_exact_attn(hb_ref, hq_ref, kk, v, o_ref, row0, TT):
    b = hb_ref[0:TT, :]
    ones_bd = jnp.where(_head_block_mask(G, G, DK, DK), 1.0, 0.0).astype(BF16)
    s_row = lax.broadcasted_iota(jnp.int32, (TT, G), 0)

    def body(t, carry):
        bt = hb_ref[pl.ds(t, 1), :]
        qt = hq_ref[pl.ds(t, 1), :]
        e = jnp.where(s_row <= t, qt * kk * jnp.exp(jnp.minimum(bt - b, 0.0)), 0.0)
        a = jnp.dot(e.astype(BF16), ones_bd, preferred_element_type=F32)
        o_ref[pl.ds(row0 + t, 1), :] = jnp.sum(a * v, axis=0, keepdims=True)
        return carry

    lax.fori_loop(0, TT, body, 0)


def _head_norm_gate(o, zg, hnorm):
    ones_bd = jnp.where(_head_block_mask(G, G, DK, DK), 1.0, 0.0).astype(BF16)
    o2 = o * o
    hi = o2.astype(BF16)
    lo = (o2 - hi.astype(F32)).astype(BF16)
    ssq = (jnp.dot(hi, ones_bd, preferred_element_type=F32)
           + jnp.dot(lo, ones_bd, preferred_element_type=F32))
    return o * lax.rsqrt(ssq * (1.0 / DK) + EPS) * hnorm * _silu(zg)


def _pool_select(s2, s4, s8, s16, pos):
    shape = s2.shape
    grp = lax.broadcasted_iota(jnp.int32, shape, len(shape) - 1) // POOL_CH
    ssum = jnp.where(grp == 0, s2, jnp.where(grp == 1, s4, jnp.where(grp == 2, s8, s16)))
    win = jnp.where(grp == 0, 2, jnp.where(grp == 1, 4, jnp.where(grp == 2, 8, 16)))
    cnt = jnp.minimum(pos + 1, win).astype(F32)
    return ssum / cnt


def _conf_tail(z, cb, lng, lnb):
    z = z + cb
    mu = jnp.mean(z, axis=-1, keepdims=True)
    zc = z - mu
    var = jnp.mean(zc * zc, axis=-1, keepdims=True)
    return _silu(zc * lax.rsqrt(var + EPS) * lng + lnb)


def _ffn_block(x, npre, wg, wu, wd, npost):
    h = _rmsnorm(x, npre).astype(BF16)
    g = jnp.dot(h, wg, preferred_element_type=F32)
    u = jnp.dot(h, wu, preferred_element_type=F32)
    a = (_silu(g) * u).astype(BF16)
    ff = jnp.dot(a, wd, preferred_element_type=F32)
    return x + _rmsnorm(ff, npost)


def _layer_prompt_kernel(layer, TT, nt,
                         x_ref, npre_ref, win_ref, convw_ref, poolbd_ref, pscale_ref, lb_ref,
                         hnorm_ref, cdw_ref, cb_ref, lng_ref, lnb_ref, wout_ref, npost_ref,
                         fpre_ref, wg_ref, wu_ref, wd_ref, fpost_ref,
                         y_ref, oconv_ref, opool_ref, ohgrn_ref, oconf_ref,
                         p_ref, ea_ref, eb_ref, ed_ref, sh_ref, st_ref, hb_ref, hq_ref, hk_ref, oi_ref,
                         o_ref, cat_ref, x1_ref, hm_ref, hf_ref, a_ref, ff_ref, mix_ref):
    i = pl.program_id(0)
    t = i % nt
    slot = i % 3

    @pl.when(i == 0)
    def _first():
        x1_ref[1] = jnp.zeros((TT, D_MODEL), F32)
        x1_ref[2] = jnp.zeros((TT, D_MODEL), F32)
        a_ref[0] = jnp.zeros((TT, D_FF), BF16)

    @pl.when(t == 0)
    def _new_sequence():
        ea_ref[0:8, :] = jnp.zeros((8, G), F32)
        eb_ref[0:16, :] = jnp.zeros((16, G), F32)
        ed_ref[0:32, :] = jnp.zeros((32, G), F32)
        st_ref[...] = jnp.zeros((G, G), F32)

    lower = _lower_bound(lb_ref[...], layer)
    n_chunks = TT // HGRN_CHUNK
    slot_up = (i + 2) % 3
    slot_down = (i + 1) % 3
    a_new = (i + 1) % 2
    a_old = i % 2

    def f_norm():
        hf_ref[...] = _rmsnorm(x1_ref[slot_up], fpre_ref[...]).astype(BF16)

    def f_gate_up(j):
        cols = slice(j * FFN_COLS, (j + 1) * FFN_COLS)
        hf = hf_ref[...]
        g = jnp.dot(hf, wg_ref[:, cols], preferred_element_type=F32)
        u = jnp.dot(hf, wu_ref[:, cols], preferred_element_type=F32)
        a_ref[a_new, :, cols] = (_silu(g) * u).astype(BF16)

    def f_down(k):
        cols = slice(k * G, (k + 1) * G)
        ff_ref[:, cols] = jnp.dot(a_ref[a_old], wd_ref[:, cols], preferred_element_type=F32)

    def f_out():
        y_ref[...] = x1_ref[slot_down] + _rmsnorm(ff_ref[...], fpost_ref[...])

    def m_norm():
        hm_ref[...] = _rmsnorm(x_ref[...], npre_ref[...]).astype(BF16)

    def m_proj(blk):
        cols = slice(blk * G, (blk + 1) * G)
        p_ref[:, cols] = jnp.dot(hm_ref[...], win_ref[:, cols], preferred_element_type=F32)

    def m_conv():
        cu = p_ref[:, G:2 * G] * p_ref[:, 2 * G:3 * G]
        ea_ref[8:8 + TT, :] = cu
        ya = (convw_ref[2:3, :] * cu + convw_ref[1:2, :] * ea_ref[7:7 + TT, :]
              + convw_ref[0:1, :] * ea_ref[6:6 + TT, :])
        cat_ref[:, 0:G] = (p_ref[:, 0:G] * ya).astype(BF16)
        oconv_ref[...] = ea_ref[TT + 6:TT + 8, :]
        ea_ref[0:8, :] = ea_ref[TT:TT + 8, :]

    def m_pool():
        pp = p_ref[:, 3 * G:4 * G]
        eb_ref[16:16 + TT, :] = pp
        e = eb_ref[...]
        s2 = e + pltpu.roll(e, 1, 0)
        s4 = s2 + pltpu.roll(s2, 2, 0)
        s8 = s4 + pltpu.roll(s4, 4, 0)
        s16 = s8 + pltpu.roll(s8, 8, 0)
        pos = t * TT + lax.broadcasted_iota(jnp.int32, (TT, G), 0)
        mean = _pool_select(s2[16:], s4[16:], s8[16:], s16[16:], pos)
        yb = jnp.dot((mean - pp).astype(BF16), poolbd_ref[...], preferred_element_type=F32)
        cat_ref[:, G:2 * G] = (yb * pscale_ref[...]).astype(BF16)
        opool_ref[...] = eb_ref[TT + 1:TT + 16, :]
        eb_ref[0:16, :] = eb_ref[TT:TT + 16, :]

    hg = {}

    def m_hgrn_gates():
        q, kk, _, logf = _hgrn_gates(p_ref[:, 4 * G:5 * G], p_ref[:, 5 * G:6 * G],
                                     p_ref[:, 6 * G:7 * G], lower)
        hq_ref[...] = q
        hk_ref[...] = kk
        hb_ref[...] = _cumsum_rows_mxu(logf)

    def m_hgrn_state():
        b = hb_ref[...]
        o_inter = _hgrn_state_terms(hq_ref[...], hk_ref[...], p_ref[:, 6 * G:7 * G], b, st_ref)
        oi_ref[...] = o_inter
        o_ref[...] = o_inter
        hg["refs"], hg["span"] = _hgrn_refs(b, HGRN_CHUNK)

    def m_hgrn_chunk(j):
        lo, hi = j * HGRN_CHUNK, (j + 1) * HGRN_CHUNK
        contrib = _hgrn_fast_chunk(hq_ref[lo:TT, :], hb_ref[lo:TT, :], hk_ref[lo:hi, :],
                                   p_ref[lo:hi, 6 * G:7 * G], hg["refs"][j], HGRN_CHUNK)
        o_ref[lo:TT, :] = o_ref[lo:TT, :] + contrib

    def m_hgrn_out():
        yc = _head_norm_gate(o_ref[...], p_ref[:, 7 * G:8 * G], hnorm_ref[...])
        cat_ref[:, 2 * G:3 * G] = yc.astype(BF16)

    def m_glu():
        ed_ref[32:32 + TT, :] = p_ref[:, 8 * G:9 * G] * _sigmoid(p_ref[:, 9 * G:10 * G])
        ed = ed_ref[...]
        for r in range(1, 8):
            sh_ref[r - 1] = pltpu.roll(ed, TT + 32 - r, 0)

    def m_conf(rb):
        base = rb * CONV_ROWS
        acc = None
        for j in range(CONF_WIDTH):
            a8, r = divmod(2 + j, 8)
            lo = base + 8 * a8
            src = ed_ref[lo:lo + CONV_ROWS, :] if r == 0 else sh_ref[r - 1, lo:lo + CONV_ROWS, :]
            term = cdw_ref[j:j + 1, :] * src
            acc = term if acc is None else acc + term
        yd = _conf_tail(acc, cb_ref[...], lng_ref[...], lnb_ref[...])
        cat_ref[base:base + CONV_ROWS, 3 * G:4 * G] = yd.astype(BF16)

    def m_conf_tail():
        oconf_ref[...] = ed_ref[TT + 2:TT + 32, :]
        ed_ref[0:32, :] = ed_ref[TT:TT + 32, :]

    def m_out():
        mix_ref[...] = jnp.dot(cat_ref[...], wout_ref[...], preferred_element_type=F32)

    def m_out_norm():
        x1_ref[slot] = x_ref[...] + _rmsnorm(mix_ref[...], npost_ref[...])

    n_gu = D_FF // FFN_COLS
    gate_up = [functools.partial(f_gate_up, j) for j in range(n_gu)]
    schedule = [
        functools.partial(f_down, 0), f_norm, functools.partial(f_down, 1), m_norm,
        functools.partial(f_down, 2), functools.partial(f_down, 3),
        functools.partial(m_proj, 8), functools.partial(m_proj, 9), f_out,
        gate_up[0], m_glu, functools.partial(m_proj, 3),
        gate_up[1], functools.partial(m_conf, 0), functools.partial(m_proj, 0),
        functools.partial(m_proj, 1), functools.partial(m_proj, 2),
        gate_up[2], functools.partial(m_conf, 1), functools.partial(m_proj, 4),
        functools.partial(m_proj, 5), functools.partial(m_proj, 6),
        gate_up[3], functools.partial(m_conf, 2), functools.partial(m_proj, 7),
        gate_up[4], functools.partial(m_conf, 3), m_conf_tail,
        gate_up[5], m_conv, m_pool,
        gate_up[6], m_hgrn_gates,
        gate_up[7], m_hgrn_state,
        gate_up[8], functools.partial(m_hgrn_chunk, 0), functools.partial(m_hgrn_chunk, 1),
        gate_up[9], functools.partial(m_hgrn_chunk, 2), functools.partial(m_hgrn_chunk, 3),
        m_hgrn_out, m_out, gate_up[10], m_out_norm,
    ]
    assert n_gu == 11 and n_chunks == 4 and TT // CONV_ROWS == 4
    for piece in schedule:
        piece()
    span = hg["span"]

    @pl.when(span >= FAST_DECAY_LIMIT)
    def _redo_exact():
        f = lower + (1.0 - lower) * _sigmoid(p_ref[:, 5 * G:6 * G])
        _hgrn_exact_attn(hb_ref, hq_ref, 1.0 - f, p_ref[:, 6 * G:7 * G], o_ref, 0, TT)
        yce = _head_norm_gate(o_ref[...] + oi_ref[...], p_ref[:, 7 * G:8 * G], hnorm_ref[...])
        cat_ref[:, 2 * G:3 * G] = yce.astype(BF16)
        mixe = jnp.dot(cat_ref[...], wout_ref[...], preferred_element_type=F32)
        x1_ref[slot] = x_ref[...] + _rmsnorm(mixe, npost_ref[...])

    @pl.when(t == nt - 1)
    def _state_out():
        s = st_ref[...].T
        for hh in range(HEADS):
            ohgrn_ref[hh] = s[hh * DK:(hh + 1) * DK, hh * DK:(hh + 1) * DK]


def _mix_sample_kernel(layer, NS, TS, start_pos,
                       x_ref, sconv_ref, spool_ref, shgrn_ref, sconf_ref,
                       npre_ref, win_ref, convw_ref, poolbd_ref, pscale_ref, lb_ref,
                       hnorm_ref, cdw_ref, cb_ref, lng_ref, lnb_ref, wout_ref, npost_ref,
                       y_ref, oconv_ref, opool_ref, ohgrn_ref, oconf_ref,
                       p_ref, u_ref, pool_ref, qT_ref, fT_ref, kT_ref, vT_ref, oT_ref, cat_ref):
    i = pl.program_id(0)

    @pl.when(i == 0)
    def _load_states():
        oconv_ref[...] = sconv_ref[...]
        opool_ref[...] = spool_ref[...]
        oconf_ref[...] = sconf_ref[...]
        ohgrn_ref[...] = shgrn_ref[...]

    def slab(t):
        return slice(t * NS, (t + 1) * NS)

    x = x_ref[...]
    h = _rmsnorm(x, npre_ref[...]).astype(BF16)
    p_ref[...] = jnp.dot(h, win_ref[...], preferred_element_type=F32)

    def conv_in(j):
        if j < SC_WIDTH - 1:
            return oconv_ref[j]
        rows = slab(j - (SC_WIDTH - 1))
        return p_ref[rows, G:2 * G] * p_ref[rows, 2 * G:3 * G]

    for t in range(TS):
        ya = (convw_ref[0:1, :] * conv_in(t) + convw_ref[1:2, :] * conv_in(t + 1)
              + convw_ref[2:3, :] * conv_in(t + 2))
        cat_ref[slab(t), 0:G] = (p_ref[slab(t), 0:G] * ya).astype(BF16)
    for j in range(SC_WIDTH - 1):
        oconv_ref[j] = conv_in(j + TS)

    def pool_in(j):
        if j < POOL_BUF:
            return opool_ref[j]
        return p_ref[slab(j - POOL_BUF), 3 * G:4 * G]

    for t in range(TS):
        idx = POOL_BUF + t
        run = pool_in(idx)
        sums = {}
        for j in range(1, 16):
            run = run + pool_in(idx - j)
            if j + 1 in (2, 4, 8, 16):
                sums[j + 1] = run
        pos = jnp.full((NS, G), start_pos + i * TS + t, jnp.int32)
        mean = _pool_select(sums[2], sums[4], sums[8], sums[16], pos)
        pool_ref[slab(t), :] = mean - pool_in(idx)
    yb = jnp.dot(pool_ref[...].astype(BF16), poolbd_ref[...], preferred_element_type=F32)
    cat_ref[:, G:2 * G] = (yb * pscale_ref[...]).astype(BF16)
    for j in range(POOL_BUF):
        opool_ref[j] = pool_in(j + TS)

    lower = _lower_bound(lb_ref[...], layer)
    q = _silu(p_ref[:, 4 * G:5 * G])
    f = lower + (1.0 - lower) * _sigmoid(p_ref[:, 5 * G:6 * G])
    for t in range(TS):
        qT_ref[t] = q[slab(t)].T
        fT_ref[t] = jnp.maximum(f[slab(t)], F_MIN).T
        kT_ref[t] = (1.0 - f[slab(t)]).T
        vT_ref[t] = p_ref[slab(t), 6 * G:7 * G].T
    for hh in range(HEADS):
        head = slice(hh * DK, (hh + 1) * DK)
        vts = [vT_ref[t, head, :] for t in range(TS)]

        def body(k, accs, hh=hh, vts=vts):
            c = hh * DK + k
            s = ohgrn_ref[c]
            out = []
            for t in range(TS):
                s = fT_ref[t, pl.ds(c, 1), :] * s + kT_ref[t, pl.ds(c, 1), :] * vts[t]
                out.append(accs[t] + qT_ref[t, pl.ds(c, 1), :] * s)
            ohgrn_ref[c] = s
            return tuple(out)

        accs = lax.fori_loop(0, DK, body, tuple(jnp.zeros((DK, NS), F32) for _ in range(TS)),
                             unroll=2)
        for t in range(TS):
            oT_ref[t, head, :] = accs[t]
    o = jnp.concatenate([oT_ref[t].T for t in range(TS)], axis=0)
    yc = _head_norm_gate(o, p_ref[:, 7 * G:8 * G], hnorm_ref[...])
    cat_ref[:, 2 * G:3 * G] = yc.astype(BF16)

    u_ref[...] = p_ref[:, 8 * G:9 * G] * _sigmoid(p_ref[:, 9 * G:10 * G])
    HALF = NS // 2

    def conf_in(j, rows):
        if j < CONF_WIDTH - 1:
            return oconf_ref[j, rows, :]
        base = (j - (CONF_WIDTH - 1)) * NS
        return u_ref[base + rows.start:base + rows.stop, :]

    for t in range(TS):
        for hf in range(2):
            rows = slice(hf * HALF, (hf + 1) * HALF)
            acc = None
            for j in range(CONF_WIDTH):
                term = cdw_ref[j:j + 1, :] * conf_in(t + j, rows)
                acc = term if acc is None else acc + term
            yd = _conf_tail(acc, cb_ref[...], lng_ref[...], lnb_ref[...])
            cat_ref[t * NS + hf * HALF:t * NS + (hf + 1) * HALF, 3 * G:4 * G] = yd.astype(BF16)
    for j in range(CONF_WIDTH - 1):
        oconf_ref[j] = conf_in(j + TS, slice(0, NS))

    mix = jnp.dot(cat_ref[...], wout_ref[...], preferred_element_type=F32)
    y_ref[...] = x + _rmsnorm(mix, npost_ref[...])


def _ffn_kernel(x_ref, npre_ref, wg_ref, wu_ref, wd_ref, npost_ref, y_ref):
    y_ref[...] = _ffn_block(x_ref[...], npre_ref[...], wg_ref[...], wu_ref[...], wd_ref[...],
                            npost_ref[...])


def _layer_spec(shape, layer, single_buffer=False):
    nd = len(shape)

    def imap(*_):
        return (layer,) + (0,) * nd

    if single_buffer:
        return pl.BlockSpec((None,) + tuple(shape), imap, pipeline_mode=pl.Buffered(1))
    return pl.BlockSpec((None,) + tuple(shape), imap)


def _mixer_weight_specs(layer):
    return [
        _layer_spec((1, D_MODEL), layer),
        _layer_spec((D_MODEL, D_IN), layer, True),
        _layer_spec((SC_WIDTH, G), layer),
        _layer_spec((G, G), layer),
        _layer_spec((1, G), layer),
        pl.BlockSpec((DEPTH, G), lambda *_: (0, 0)),
        _layer_spec((1, G), layer),
        _layer_spec((CONF_WIDTH, G), layer),
        _layer_spec((1, G), layer),
        _layer_spec((1, G), layer),
        _layer_spec((1, G), layer),
        _layer_spec((D_MODEL, D_MODEL), layer, True),
        _layer_spec((1, D_MODEL), layer),
    ]


def _ffn_weight_specs(layer):
    return [
        _layer_spec((1, D_MODEL), layer),
        _layer_spec((D_MODEL, D_FF), layer, True),
        _layer_spec((D_MODEL, D_FF), layer, True),
        _layer_spec((D_FF, D_MODEL), layer, True),
        _layer_spec((1, D_MODEL), layer),
    ]


def _layer_prompt(layer, x, mixer_wts, ffn_wts):
    n, seq, _ = x.shape
    TT = PROMPT_TILE
    nt = seq // TT
    last = n * nt - 1

    def mix_tile(i):
        return jnp.minimum(i, last)

    def ffn_tile(i):
        return jnp.maximum(i - 2, 0)

    out_shape = (
        jax.ShapeDtypeStruct((n, seq, D_MODEL), F32),
        jax.ShapeDtypeStruct((n, SC_WIDTH - 1, G), F32),
        jax.ShapeDtypeStruct((n, POOL_BUF, G), F32),
        jax.ShapeDtypeStruct((n, HEADS, DK, DK), F32),
        jax.ShapeDtypeStruct((n, CONF_WIDTH - 1, G), F32),
    )
    out_specs = (
        pl.BlockSpec((None, TT, D_MODEL), lambda i: (ffn_tile(i) // nt, ffn_tile(i) % nt, 0)),
        pl.BlockSpec((None, SC_WIDTH - 1, G), lambda i: (mix_tile(i) // nt, 0, 0)),
        pl.BlockSpec((None, POOL_BUF, G), lambda i: (mix_tile(i) // nt, 0, 0)),
        pl.BlockSpec((None, HEADS, DK, DK), lambda i: (mix_tile(i) // nt, 0, 0, 0)),
        pl.BlockSpec((None, CONF_WIDTH - 1, G), lambda i: (mix_tile(i) // nt, 0, 0)),
    )
    scratch = [
        pltpu.VMEM((TT, D_IN), F32),
        pltpu.VMEM((8 + TT, G), F32),
        pltpu.VMEM((16 + TT, G), F32),
        pltpu.VMEM((32 + TT, G), F32),
        pltpu.VMEM((7, 32 + TT, G), F32),
        pltpu.VMEM((G, G), F32),
        pltpu.VMEM((TT, G), F32),
        pltpu.VMEM((TT, G), F32),
        pltpu.VMEM((TT, G), F32),
        pltpu.VMEM((TT, G), F32),
        pltpu.VMEM((TT, G), F32),
        pltpu.VMEM((TT, D_MODEL), BF16),
        pltpu.VMEM((3, TT, D_MODEL), F32),
        pltpu.VMEM((TT, D_MODEL), BF16),
        pltpu.VMEM((TT, D_MODEL), BF16),
        pltpu.VMEM((2, TT, D_FF), BF16),
        pltpu.VMEM((TT, D_MODEL), F32),
        pltpu.VMEM((TT, D_MODEL), F32),
    ]
    return pl.pallas_call(
        functools.partial(_layer_prompt_kernel, layer, TT, nt),
        grid=(n * nt + 2,),
        in_specs=[pl.BlockSpec((None, TT, D_MODEL),
                               lambda i: (mix_tile(i) // nt, mix_tile(i) % nt, 0))]
        + _mixer_weight_specs(layer) + _ffn_weight_specs(layer),
        out_specs=out_specs,
        out_shape=out_shape,
        scratch_shapes=scratch,
        compiler_params=pltpu.CompilerParams(
            dimension_semantics=("arbitrary",), vmem_limit_bytes=VMEM_LIMIT),
        name=f"layer_prompt_l{layer}",
    )(x, *mixer_wts, *ffn_wts)


def _mix_sample(layer, x2d, s_conv, s_pool, s_hgrn, s_conf, wts, n_seq):
    m = x2d.shape[0]
    NS = n_seq
    TS = SAMPLE_STEPS
    M = TS * NS

    def state_spec(shape):
        nd = len(shape)
        return pl.BlockSpec((None,) + tuple(shape), lambda i: (layer,) + (0,) * nd,
                            pipeline_mode=pl.Buffered(1))

    def out_spec(shape):
        nd = len(shape)
        return pl.BlockSpec(tuple(shape), lambda i: (0,) * nd, pipeline_mode=pl.Buffered(1))

    state_shapes = [(SC_WIDTH - 1, NS, G), (POOL_BUF, NS, G), (G, DK, NS), (CONF_WIDTH - 1, NS, G)]
    in_specs = ([pl.BlockSpec((M, D_MODEL), lambda i: (i, 0))]
                + [state_spec(s) for s in state_shapes] + _mixer_weight_specs(layer))
    out_specs = tuple([pl.BlockSpec((M, D_MODEL), lambda i: (i, 0))]
                      + [out_spec(s) for s in state_shapes])
    out_shape = tuple([jax.ShapeDtypeStruct((m, D_MODEL), F32)]
                      + [jax.ShapeDtypeStruct(s, F32) for s in state_shapes])
    scratch = [
        pltpu.VMEM((M, D_IN), F32),
        pltpu.VMEM((M, G), F32),
        pltpu.VMEM((M, G), F32),
        pltpu.VMEM((TS, G, NS), F32),
        pltpu.VMEM((TS, G, NS), F32),
        pltpu.VMEM((TS, G, NS), F32),
        pltpu.VMEM((TS, G, NS), F32),
        pltpu.VMEM((TS, G, NS), F32),
        pltpu.VMEM((M, D_MODEL), BF16),
    ]
    return pl.pallas_call(
        functools.partial(_mix_sample_kernel, layer, NS, TS, PAST_LEN),
        grid=(m // M,),
        in_specs=in_specs,
        out_specs=out_specs,
        out_shape=out_shape,
        scratch_shapes=scratch,
        compiler_params=pltpu.CompilerParams(
            dimension_semantics=("arbitrary",), vmem_limit_bytes=VMEM_LIMIT),
        name=f"mix_sample_l{layer}",
    )(x2d, s_conv, s_pool, s_hgrn, s_conf, *wts)


def _ffn(layer, x2d, ffn_wts, tag):
    m = x2d.shape[0]
    TM = FFN_TILE
    return pl.pallas_call(
        _ffn_kernel,
        grid=(m // TM,),
        in_specs=[pl.BlockSpec((TM, D_MODEL), lambda i: (i, 0))] + _ffn_weight_specs(layer),
        out_specs=pl.BlockSpec((TM, D_MODEL), lambda i: (i, 0)),
        out_shape=jax.ShapeDtypeStruct((m, D_MODEL), F32),
        compiler_params=pltpu.CompilerParams(
            dimension_semantics=("arbitrary",), vmem_limit_bytes=VMEM_LIMIT),
        name=f"ffn_{tag}_l{layer}",
    )(x2d, *ffn_wts)


def kernel(x_prompt, x_sample, state_conv, state_pool, state_hgrn, state_conf, norm_mix_pre, norm_mix_post, w_in, conv_w, pool_w, pool_scale, hgrn_lb, hgrn_norm, conf_dw, conf_b, conf_ln_g, conf_ln_b, w_out, norm_ffn_pre, norm_ffn_post, w_gate, w_up, w_down):
    def row(a):
        return a.reshape(DEPTH, 1, a.shape[-1])

    eye = jnp.eye(G // POOL_CH, dtype=pool_w.dtype)
    pool_bd = (pool_w[:, :, :, None, :] * eye[None, :, None, :, None]).reshape(DEPTH, G, G)
    mixer_wts = (row(norm_mix_pre), w_in.astype(BF16), conv_w, pool_bd.astype(BF16),
                 row(pool_scale), hgrn_lb, row(hgrn_norm), conf_dw, row(conf_b),
                 row(conf_ln_g), row(conf_ln_b), w_out.astype(BF16), row(norm_mix_post))
    ffn_wts = (row(norm_ffn_pre), w_gate.astype(BF16), w_up.astype(BF16), w_down.astype(BF16),
               row(norm_ffn_post))

    ns, ts, _ = x_sample.shape
    xs = x_sample.transpose(1, 0, 2).reshape(ts * ns, D_MODEL)
    sc_t = state_conv.transpose(1, 2, 0, 3)
    sp_t = state_pool.transpose(1, 2, 0, 3)
    sf_t = state_conf.transpose(1, 2, 0, 3)
    sh_t = state_hgrn.transpose(1, 2, 3, 4, 0).reshape(DEPTH, G, DK, ns)
    xp = x_prompt
    p_states, s_states = [], []
    for layer in range(DEPTH):
        xp, *stp = _layer_prompt(layer, xp, mixer_wts, ffn_wts)
        p_states.append(stp)
        xs, *sts = _mix_sample(layer, xs, sc_t, sp_t, sh_t, sf_t, mixer_wts, ns)
        s_states.append(sts)
        xs = _ffn(layer, xs, ffn_wts, "sample")
    xs = xs.reshape(ts, ns, D_MODEL).transpose(1, 0, 2)

    def stack(states, i):
        return jnp.stack([states[layer][i] for layer in range(DEPTH)], axis=1)

    def stack_t(i):
        return jnp.stack([s_states[layer][i] for layer in range(DEPTH)], axis=0).transpose(2, 0, 1, 3)

    hgrn_s = jnp.stack([s_states[layer][2] for layer in range(DEPTH)], axis=0)
    hgrn_s = hgrn_s.reshape(DEPTH, HEADS, DK, DK, ns).transpose(4, 0, 1, 2, 3)

    return (xp, xs,
            stack(p_states, 0), stack(p_states, 1), stack(p_states, 2), stack(p_states, 3),
            stack_t(0), stack_t(1), hgrn_s, stack_t(3))
```

```python
import functools

import jax
import jax.numpy as jnp
from jax import lax
from jax.experimental import pallas as pl
from jax.experimental.pallas import tpu as pltpu

F32 = jnp.float32
BF16 = jnp.bfloat16

D_MODEL = 1024
DEPTH = 2
PAST_LEN = 16384
G = 256
N_BLOCKS = 10
D_IN = N_BLOCKS * G
SC_WIDTH = 3
POOL_BUF = 15
POOL_CH = 64
HEADS = 4
DK = 64
CONF_WIDTH = 31
D_FF = 2816
EPS = 1e-6
F_MIN = 1e-20

HGRN_CHUNK = 64
FAST_DECAY_LIMIT = 60.0
PROMPT_TILE = 256
SAMPLE_STEPS = 2
FFN_TILE = 512
FFN_COLS = 256
CONV_ROWS = 64
VMEM_LIMIT = 56 * 1024 * 1024


def _sigmoid(x):
    return jax.nn.sigmoid(x)


def _silu(x):
    return x * jax.nn.sigmoid(x)


def _rmsnorm(x, g):
    ms = jnp.mean(x * x, axis=-1, keepdims=True)
    return x * lax.rsqrt(ms + EPS) * g


def _head_block_mask(rows, cols, row_block, col_block):
    r = lax.broadcasted_iota(jnp.int32, (rows, cols), 0) // row_block
    c = lax.broadcasted_iota(jnp.int32, (rows, cols), 1) // col_block
    return r == c


def _cumsum_rows_mxu(x):
    n = x.shape[0]
    tri = (lax.broadcasted_iota(jnp.int32, (n, n), 0)
           >= lax.broadcasted_iota(jnp.int32, (n, n), 1))
    tri = jnp.where(tri, 1.0, 0.0).astype(BF16)
    hi = x.astype(BF16)
    r1 = x - hi.astype(F32)
    mid = r1.astype(BF16)
    lo = (r1 - mid.astype(F32)).astype(BF16)
    return (jnp.dot(tri, hi, preferred_element_type=F32)
            + jnp.dot(tri, mid, preferred_element_type=F32)
            + jnp.dot(tri, lo, preferred_element_type=F32))


def _lower_bound(lb_all, layer):
    m = jnp.max(lb_all, axis=0, keepdims=True)
    e = jnp.exp(lb_all - m)
    sm = e / jnp.sum(e, axis=0, keepdims=True)
    cs = sm[0:1]
    for i in range(1, layer + 1):
        cs = cs + sm[i:i + 1]
    return cs - sm[0:1]


def _hgrn_gates(zq, zf, zi, lower):
    q = _silu(zq)
    f = lower + (1.0 - lower) * _sigmoid(zf)
    logf = jnp.log(jnp.maximum(f, F_MIN))
    return q, 1.0 - f, zi, logf


def _hgrn_state_terms(q, kk, v, b, st_ref):
    TT = q.shape[0]
    b_end = b[TT - 1:TT, :]
    st = st_ref[...]
    qs = (q * jnp.exp(b)).astype(BF16)
    o_inter = lax.dot_general(qs, st.astype(BF16), (((1,), (1,)), ((), ())),
                              preferred_element_type=F32)
    kh = (kk * jnp.exp(b_end - b)).astype(BF16)
    upd = lax.dot_general(v.astype(BF16), kh, (((0,), (0,)), ((), ())),
                          preferred_element_type=F32)
    bd = _head_block_mask(G, G, DK, DK)
    st_ref[...] = st * jnp.exp(b_end) + jnp.where(bd, upd, 0.0)
    return o_inter


def _hgrn_refs(b, C):
    refs = []
    span = None
    for j in range(b.shape[0] // C):
        first = b[j * C:j * C + 1, :]
        last = b[(j + 1) * C - 1:(j + 1) * C, :]
        refs.append(0.5 * (first + last))
        half = jnp.max(0.5 * (first - last))
        span = half if span is None else jnp.maximum(span, half)
    return refs, span


def _hgrn_fast_chunk(q_tgt, b_tgt, kk_src, v_src, r, C):
    nt = q_tgt.shape[0]
    rows_mask = _head_block_mask(HEADS * C, G, C, DK)
    qz = (q_tgt * jnp.exp(b_tgt - r)).astype(BF16)
    ke = kk_src * jnp.exp(r - b_tgt[0:C])
    kebd = jnp.where(rows_mask, jnp.concatenate([ke] * HEADS, axis=0), 0.0).astype(BF16)
    attn = lax.dot_general(qz, kebd, (((1,), (1,)), ((), ())),
                           preferred_element_type=F32)
    t_idx = lax.broadcasted_iota(jnp.int32, (nt, HEADS * C), 0)
    s_idx = lax.broadcasted_iota(jnp.int32, (nt, HEADS * C), 1) % C
    attn = jnp.where(t_idx >= s_idx, attn, 0.0).astype(BF16)
    vbd = jnp.where(rows_mask, jnp.concatenate([v_src] * HEADS, axis=0), 0.0).astype(BF16)
    return jnp.dot(attn, vbd, preferred_element_type=F32)


def _hgrn_exact_attn(hb_ref, hq_ref, kk, v, o_ref, row0, TT):
    b = hb_ref[0:TT, :]
    ones_bd = jnp.where(_head_block_mask(G, G, DK, DK), 1.0, 0.0).astype(BF16)
    s_row = lax.broadcasted_iota(jnp.int32, (TT, G), 0)

    def body(t, carry):
        bt = hb_ref[pl.ds(t, 1), :]
        qt = hq_ref[pl.ds(t, 1), :]
        e = jnp.where(s_row <= t, qt * kk * jnp.exp(jnp.minimum(bt - b, 0.0)), 0.0)
        a = jnp.dot(e.astype(BF16), ones_bd, preferred_element_type=F32)
        o_ref[pl.ds(row0 + t, 1), :] = jnp.sum(a * v, axis=0, keepdims=True)
        return carry

    lax.fori_loop(0, TT, body, 0)


def _head_norm_gate(o, zg, hnorm):
    ones_bd = jnp.where(_head_block_mask(G, G, DK, DK), 1.0, 0.0).astype(BF16)
    o2 = o * o
    hi = o2.astype(BF16)
    lo = (o2 - hi.astype(F32)).astype(BF16)
    ssq = (jnp.dot(hi, ones_bd, preferred_element_type=F32)
           + jnp.dot(lo, ones_bd, preferred_element_type=F32))
    return o * lax.rsqrt(ssq * (1.0 / DK) + EPS) * hnorm * _silu(zg)


def _pool_select(s2, s4, s8, s16, pos):
    shape = s2.shape
    grp = lax.broadcasted_iota(jnp.int32, shape, len(shape) - 1) // POOL_CH
    ssum = jnp.where(grp == 0, s2, jnp.where(grp == 1, s4, jnp.where(grp == 2, s8, s16)))
    win = jnp.where(grp == 0, 2, jnp.where(grp == 1, 4, jnp.where(grp == 2, 8, 16)))
    cnt = jnp.minimum(pos + 1, win).astype(F32)
    return ssum / cnt


def _conf_tail(z, cb, lng, lnb):
    z = z + cb
    mu = jnp.mean(z, axis=-1, keepdims=True)
    zc = z - mu
    var = jnp.mean(zc * zc, axis=-1, keepdims=True)
    return _silu(zc * lax.rsqrt(var + EPS) * lng + lnb)


def _ffn_block(x, npre, wg, wu, wd, npost):
    h = _rmsnorm(x, npre).astype(BF16)
    g = jnp.dot(h, wg, preferred_element_type=F32)
    u = jnp.dot(h, wu, preferred_element_type=F32)
    a = (_silu(g) * u).astype(BF16)
    ff = jnp.dot(a, wd, preferred_element_type=F32)
    return x + _rmsnorm(ff, npost)


def _layer_prompt_kernel(layer, TT, nt, n_tiles,
                         x_ref, npre_ref, win_ref, convw_ref, poolbd_ref, pscale_ref, lb_ref,
                         hnorm_ref, cdw_ref, cb_ref, lng_ref, lnb_ref, wout_ref, npost_ref,
                         fpre_ref, wg_ref, wu_ref, wd_ref, fpost_ref,
                         y_ref, oconv_ref, opool_ref, ohgrn_ref, oconf_ref,
                         p_ref, ea_ref, eb_ref, ed_ref, sh_ref, st_ref, hb_ref, hq_ref, hk_ref, oi_ref,
                         o_ref, cat_ref, x1_ref, hm_ref, hf_ref, a_ref, ff_ref, mix_ref):
    i = pl.program_id(0)
    t = i % nt
    slot = i % 3

    @pl.when((t == 0) & (i < n_tiles))
    def _new_sequence():
        ea_ref[0:8, :] = jnp.zeros((8, G), F32)
        eb_ref[0:16, :] = jnp.zeros((16, G), F32)
        ed_ref[0:32, :] = jnp.zeros((32, G), F32)
        st_ref[...] = jnp.zeros((G, G), F32)

    lower = _lower_bound(lb_ref[...], layer)
    n_chunks = TT // HGRN_CHUNK
    slot_up = (i + 2) % 3
    slot_down = (i + 1) % 3
    a_new = (i + 1) % 2
    a_old = i % 2

    def f_norm():
        hf_ref[...] = _rmsnorm(x1_ref[slot_up], fpre_ref[...]).astype(BF16)

    def f_gate_up(j):
        cols = slice(j * FFN_COLS, (j + 1) * FFN_COLS)
        hf = hf_ref[...]
        g = jnp.dot(hf, wg_ref[:, cols], preferred_element_type=F32)
        u = jnp.dot(hf, wu_ref[:, cols], preferred_element_type=F32)
        a_ref[a_new, :, cols] = (_silu(g) * u).astype(BF16)

    def f_down(k):
        cols = slice(k * G, (k + 1) * G)
        ff_ref[:, cols] = jnp.dot(a_ref[a_old], wd_ref[:, cols], preferred_element_type=F32)

    def f_out():
        y_ref[...] = x1_ref[slot_down] + _rmsnorm(ff_ref[...], fpost_ref[...])

    def m_norm():
        hm_ref[...] = _rmsnorm(x_ref[...], npre_ref[...]).astype(BF16)

    def m_proj(blk):
        cols = slice(blk * G, (blk + 1) * G)
        p_ref[:, cols] = jnp.dot(hm_ref[...], win_ref[:, cols], preferred_element_type=F32)

    def m_conv():
        cu = p_ref[:, G:2 * G] * p_ref[:, 2 * G:3 * G]
        ea_ref[8:8 + TT, :] = cu
        ya = (convw_ref[2:3, :] * cu + convw_ref[1:2, :] * ea_ref[7:7 + TT, :]
              + convw_ref[0:1, :] * ea_ref[6:6 + TT, :])
        cat_ref[:, 0:G] = (p_ref[:, 0:G] * ya).astype(BF16)
        oconv_ref[...] = ea_ref[TT + 6:TT + 8, :]
        ea_ref[0:8, :] = ea_ref[TT:TT + 8, :]

    def m_pool():
        pp = p_ref[:, 3 * G:4 * G]
        eb_ref[16:16 + TT, :] = pp
        e = eb_ref[...]
        s2 = e + pltpu.roll(e, 1, 0)
        s4 = s2 + pltpu.roll(s2, 2, 0)
        s8 = s4 + pltpu.roll(s4, 4, 0)
        s16 = s8 + pltpu.roll(s8, 8, 0)
        pos = t * TT + lax.broadcasted_iota(jnp.int32, (TT, G), 0)
        mean = _pool_select(s2[16:], s4[16:], s8[16:], s16[16:], pos)
        yb = jnp.dot((mean - pp).astype(BF16), poolbd_ref[...], preferred_element_type=F32)
        cat_ref[:, G:2 * G] = (yb * pscale_ref[...]).astype(BF16)
        opool_ref[...] = eb_ref[TT + 1:TT + 16, :]
        eb_ref[0:16, :] = eb_ref[TT:TT + 16, :]

    hg = {}

    def m_hgrn_gates():
        q, kk, _, logf = _hgrn_gates(p_ref[:, 4 * G:5 * G], p_ref[:, 5 * G:6 * G],
                                     p_ref[:, 6 * G:7 * G], lower)
        hq_ref[...] = q
        hk_ref[...] = kk
        hb_ref[...] = _cumsum_rows_mxu(logf)

    def m_hgrn_state():
        b = hb_ref[...]
        o_inter = _hgrn_state_terms(hq_ref[...], hk_ref[...], p_ref[:, 6 * G:7 * G], b, st_ref)
        oi_ref[...] = o_inter
        o_ref[...] = o_inter
        hg["refs"], hg["span"] = _hgrn_refs(b, HGRN_CHUNK)

    def m_hgrn_chunk(j):
        lo, hi = j * HGRN_CHUNK, (j + 1) * HGRN_CHUNK
        contrib = _hgrn_fast_chunk(hq_ref[lo:TT, :], hb_ref[lo:TT, :], hk_ref[lo:hi, :],
                                   p_ref[lo:hi, 6 * G:7 * G], hg["refs"][j], HGRN_CHUNK)
        o_ref[lo:TT, :] = o_ref[lo:TT, :] + contrib

    def m_hgrn_out():
        yc = _head_norm_gate(o_ref[...], p_ref[:, 7 * G:8 * G], hnorm_ref[...])
        cat_ref[:, 2 * G:3 * G] = yc.astype(BF16)

    def m_glu():
        ed_ref[32:32 + TT, :] = p_ref[:, 8 * G:9 * G] * _sigmoid(p_ref[:, 9 * G:10 * G])
        ed = ed_ref[...]
        for r in range(1, 8):
            sh_ref[r - 1] = pltpu.roll(ed, TT + 32 - r, 0)

    def m_conf(rb):
        base = rb * CONV_ROWS
        acc = None
        for j in range(CONF_WIDTH):
            a8, r = divmod(2 + j, 8)
            lo = base + 8 * a8
            src = ed_ref[lo:lo + CONV_ROWS, :] if r == 0 else sh_ref[r - 1, lo:lo + CONV_ROWS, :]
            term = cdw_ref[j:j + 1, :] * src
            acc = term if acc is None else acc + term
        yd = _conf_tail(acc, cb_ref[...], lng_ref[...], lnb_ref[...])
        cat_ref[base:base + CONV_ROWS, 3 * G:4 * G] = yd.astype(BF16)

    def m_conf_tail():
        oconf_ref[...] = ed_ref[TT + 2:TT + 32, :]
        ed_ref[0:32, :] = ed_ref[TT:TT + 32, :]

    def m_out():
        mix_ref[...] = jnp.dot(cat_ref[...], wout_ref[...], preferred_element_type=F32)

    def m_out_norm():
        x1_ref[slot] = x_ref[...] + _rmsnorm(mix_ref[...], npost_ref[...])

    n_gu = D_FF // FFN_COLS
    gate_up = [functools.partial(f_gate_up, j) for j in range(n_gu)]
    proj = [functools.partial(m_proj, blk) for blk in range(N_BLOCKS)]
    down = [functools.partial(f_down, k) for k in range(D_MODEL // G)]
    conf = [functools.partial(m_conf, rb) for rb in range(TT // CONV_ROWS)]
    chunk = [functools.partial(m_hgrn_chunk, j) for j in range(n_chunks)]
    g = gate_up
    schedule = [
        down[0], f_norm, down[1], m_norm, down[2], down[3],
        proj[8], proj[9], f_out,
        g[0], m_glu, proj[3],
        g[1], conf[0], proj[0], proj[1], proj[2],
        g[2], conf[1], proj[4], proj[5], proj[6],
        g[3], conf[2], proj[7],
        g[4], conf[3], m_conf_tail,
        g[5], m_conv, m_pool,
        g[6], m_hgrn_gates,
        g[7], m_hgrn_state,
        g[8], chunk[0], chunk[1],
        g[9], chunk[2], chunk[3],
        m_hgrn_out, m_out, g[10], m_out_norm,
    ]
    assert n_gu == 11 and n_chunks == 4 and len(conf) == 4 and len(down) == 4
    up_chain = {id(p) for p in [f_norm] + gate_up}
    down_chain = {id(p) for p in down + [f_out]}

    def run(with_mixer, with_up, with_down):
        for piece in schedule:
            if id(piece) in up_chain:
                wanted = with_up
            elif id(piece) in down_chain:
                wanted = with_down
            else:
                wanted = with_mixer
            if wanted:
                piece()
        if not with_mixer:
            return

        @pl.when(hg["span"] >= FAST_DECAY_LIMIT)
        def _redo_exact():
            f = lower + (1.0 - lower) * _sigmoid(p_ref[:, 5 * G:6 * G])
            _hgrn_exact_attn(hb_ref, hq_ref, 1.0 - f, p_ref[:, 6 * G:7 * G], o_ref, 0, TT)
            yce = _head_norm_gate(o_ref[...] + oi_ref[...], p_ref[:, 7 * G:8 * G],
                                  hnorm_ref[...])
            cat_ref[:, 2 * G:3 * G] = yce.astype(BF16)
            mixe = jnp.dot(cat_ref[...], wout_ref[...], preferred_element_type=F32)
            x1_ref[slot] = x_ref[...] + _rmsnorm(mixe, npost_ref[...])

    pl.when(i == 0)(functools.partial(run, True, False, False))
    pl.when(i == 1)(functools.partial(run, True, True, False))
    pl.when((i >= 2) & (i < n_tiles))(functools.partial(run, True, True, True))
    pl.when(i == n_tiles)(functools.partial(run, False, True, True))
    pl.when(i == n_tiles + 1)(functools.partial(run, False, False, True))

    @pl.when((t == nt - 1) & (i < n_tiles))
    def _state_out():
        s = st_ref[...].T
        for hh in range(HEADS):
            ohgrn_ref[hh] = s[hh * DK:(hh + 1) * DK, hh * DK:(hh + 1) * DK]


def _mix_sample_kernel(layer, NS, TS, start_pos,
                       x_ref, sconv_ref, spool_ref, shgrn_ref, sconf_ref,
                       npre_ref, win_ref, convw_ref, poolbd_ref, pscale_ref, lb_ref,
                       hnorm_ref, cdw_ref, cb_ref, lng_ref, lnb_ref, wout_ref, npost_ref,
                       y_ref, oconv_ref, opool_ref, ohgrn_ref, oconf_ref,
                       p_ref, u_ref, pool_ref, qT_ref, fT_ref, kT_ref, vT_ref, oT_ref, cat_ref):
    i = pl.program_id(0)

    @pl.when(i == 0)
    def _load_states():
        oconv_ref[...] = sconv_ref[...]
        opool_ref[...] = spool_ref[...]
        oconf_ref[...] = sconf_ref[...]
        ohgrn_ref[...] = shgrn_ref[...]

    def slab(t):
        return slice(t * NS, (t + 1) * NS)

    x = x_ref[...]
    h = _rmsnorm(x, npre_ref[...]).astype(BF16)
    p_ref[...] = jnp.dot(h, win_ref[...], preferred_element_type=F32)

    def conv_in(j):
        if j < SC_WIDTH - 1:
            return oconv_ref[j]
        rows = slab(j - (SC_WIDTH - 1))
        return p_ref[rows, G:2 * G] * p_ref[rows, 2 * G:3 * G]

    for t in range(TS):
        ya = (convw_ref[0:1, :] * conv_in(t) + convw_ref[1:2, :] * conv_in(t + 1)
              + convw_ref[2:3, :] * conv_in(t + 2))
        cat_ref[slab(t), 0:G] = (p_ref[slab(t), 0:G] * ya).astype(BF16)
    for j in range(SC_WIDTH - 1):
        oconv_ref[j] = conv_in(j + TS)

    def pool_in(j):
        if j < POOL_BUF:
            return opool_ref[j]
        return p_ref[slab(j - POOL_BUF), 3 * G:4 * G]

    for t in range(TS):
        idx = POOL_BUF + t
        run = pool_in(idx)
        sums = {}
        for j in range(1, 16):
            run = run + pool_in(idx - j)
            if j + 1 in (2, 4, 8, 16):
                sums[j + 1] = run
        pos = jnp.full((NS, G), start_pos + i * TS + t, jnp.int32)
        mean = _pool_select(sums[2], sums[4], sums[8], sums[16], pos)
        pool_ref[slab(t), :] = mean - pool_in(idx)
    yb = jnp.dot(pool_ref[...].astype(BF16), poolbd_ref[...], preferred_element_type=F32)
    cat_ref[:, G:2 * G] = (yb * pscale_ref[...]).astype(BF16)
    for j in range(POOL_BUF):
        opool_ref[j] = pool_in(j + TS)

    lower = _lower_bound(lb_ref[...], layer)
    q = _silu(p_ref[:, 4 * G:5 * G])
    f = lower + (1.0 - lower) * _sigmoid(p_ref[:, 5 * G:6 * G])
    for t in range(TS):
        qT_ref[t] = q[slab(t)].T
        fT_ref[t] = jnp.maximum(f[slab(t)], F_MIN).T
        kT_ref[t] = (1.0 - f[slab(t)]).T
        vT_ref[t] = p_ref[slab(t), 6 * G:7 * G].T
    for hh in range(HEADS):
        head = slice(hh * DK, (hh + 1) * DK)
        vts = [vT_ref[t, head, :] for t in range(TS)]

        def body(k, accs, hh=hh, vts=vts):
            c = hh * DK + k
            s = ohgrn_ref[c]
            out = []
            for t in range(TS):
                s = fT_ref[t, pl.ds(c, 1), :] * s + kT_ref[t, pl.ds(c, 1), :] * vts[t]
                out.append(accs[t] + qT_ref[t, pl.ds(c, 1), :] * s)
            ohgrn_ref[c] = s
            return tuple(out)

        accs = lax.fori_loop(0, DK, body, tuple(jnp.zeros((DK, NS), F32) for _ in range(TS)),
                             unroll=2)
        for t in range(TS):
            oT_ref[t, head, :] = accs[t]
    o = jnp.concatenate([oT_ref[t].T for t in range(TS)], axis=0)
    yc = _head_norm_gate(o, p_ref[:, 7 * G:8 * G], hnorm_ref[...])
    cat_ref[:, 2 * G:3 * G] = yc.astype(BF16)

    u_ref[...] = p_ref[:, 8 * G:9 * G] * _sigmoid(p_ref[:, 9 * G:10 * G])
    HALF = NS // 2

    def conf_in(j, rows):
        if j < CONF_WIDTH - 1:
            return oconf_ref[j, rows, :]
        base = (j - (CONF_WIDTH - 1)) * NS
        return u_ref[base + rows.start:base + rows.stop, :]

    for t in range(TS):
        for hf in range(2):
            rows = slice(hf * HALF, (hf + 1) * HALF)
            acc = None
            for j in range(CONF_WIDTH):
                term = cdw_ref[j:j + 1, :] * conf_in(t + j, rows)
                acc = term if acc is None else acc + term
            yd = _conf_tail(acc, cb_ref[...], lng_ref[...], lnb_ref[...])
            cat_ref[t * NS + hf * HALF:t * NS + (hf + 1) * HALF, 3 * G:4 * G] = yd.astype(BF16)
    for j in range(CONF_WIDTH - 1):
        oconf_ref[j] = conf_in(j + TS, slice(0, NS))

    mix = jnp.dot(cat_ref[...], wout_ref[...], preferred_element_type=F32)
    y_ref[...] = x + _rmsnorm(mix, npost_ref[...])


def _ffn_kernel(x_ref, npre_ref, wg_ref, wu_ref, wd_ref, npost_ref, y_ref):
    y_ref[...] = _ffn_block(x_ref[...], npre_ref[...], wg_ref[...], wu_ref[...], wd_ref[...],
                            npost_ref[...])


def _layer_spec(shape, layer, single_buffer=False):
    nd = len(shape)

    def imap(*_):
        return (layer,) + (0,) * nd

    if single_buffer:
        return pl.BlockSpec((None,) + tuple(shape), imap, pipeline_mode=pl.Buffered(1))
    return pl.BlockSpec((None,) + tuple(shape), imap)


def _mixer_weight_specs(layer):
    return [
        _layer_spec((1, D_MODEL), layer),
        _layer_spec((D_MODEL, D_IN), layer, True),
        _layer_spec((SC_WIDTH, G), layer),
        _layer_spec((G, G), layer),
        _layer_spec((1, G), layer),
        pl.BlockSpec((DEPTH, G), lambda *_: (0, 0)),
        _layer_spec((1, G), layer),
        _layer_spec((CONF_WIDTH, G), layer),
        _layer_spec((1, G), layer),
        _layer_spec((1, G), layer),
        _layer_spec((1, G), layer),
        _layer_spec((D_MODEL, D_MODEL), layer, True),
        _layer_spec((1, D_MODEL), layer),
    ]


def _ffn_weight_specs(layer):
    return [
        _layer_spec((1, D_MODEL), layer),
        _layer_spec((D_MODEL, D_FF), layer, True),
        _layer_spec((D_MODEL, D_FF), layer, True),
        _layer_spec((D_FF, D_MODEL), layer, True),
        _layer_spec((1, D_MODEL), layer),
    ]


def _layer_prompt(layer, x, mixer_wts, ffn_wts):
    n, seq, _ = x.shape
    TT = PROMPT_TILE
    nt = seq // TT
    last = n * nt - 1

    def mix_tile(i):
        return jnp.minimum(i, last)

    def ffn_tile(i):
        return jnp.maximum(i - 2, 0)

    out_shape = (
        jax.ShapeDtypeStruct((n, seq, D_MODEL), F32),
        jax.ShapeDtypeStruct((n, SC_WIDTH - 1, G), F32),
        jax.ShapeDtypeStruct((n, POOL_BUF, G), F32),
        jax.ShapeDtypeStruct((n, HEADS, DK, DK), F32),
        jax.ShapeDtypeStruct((n, CONF_WIDTH - 1, G), F32),
    )
    out_specs = (
        pl.BlockSpec((None, TT, D_MODEL), lambda i: (ffn_tile(i) // nt, ffn_tile(i) % nt, 0)),
        pl.BlockSpec((None, SC_WIDTH - 1, G), lambda i: (mix_tile(i) // nt, 0, 0)),
        pl.BlockSpec((None, POOL_BUF, G), lambda i: (mix_tile(i) // nt, 0, 0)),
        pl.BlockSpec((None, HEADS, DK, DK), lambda i: (mix_tile(i) // nt, 0, 0, 0)),
        pl.BlockSpec((None, CONF_WIDTH - 1, G), lambda i: (mix_tile(i) // nt, 0, 0)),
    )
    scratch = [
        pltpu.VMEM((TT, D_IN), F32),
        pltpu.VMEM((8 + TT, G), F32),
        pltpu.VMEM((16 + TT, G), F32),
        pltpu.VMEM((32 + TT, G), F32),
        pltpu.VMEM((7, 32 + TT, G), F32),
        pltpu.VMEM((G, G), F32),
        pltpu.VMEM((TT, G), F32),
        pltpu.VMEM((TT, G), F32),
        pltpu.VMEM((TT, G), F32),
        pltpu.VMEM((TT, G), F32),
        pltpu.VMEM((TT, G), F32),
        pltpu.VMEM((TT, D_MODEL), BF16),
        pltpu.VMEM((3, TT, D_MODEL), F32),
        pltpu.VMEM((TT, D_MODEL), BF16),
        pltpu.VMEM((TT, D_MODEL), BF16),
        pltpu.VMEM((2, TT, D_FF), BF16),
        pltpu.VMEM((TT, D_MODEL), F32),
        pltpu.VMEM((TT, D_MODEL), F32),
    ]
    return pl.pallas_call(
        functools.partial(_layer_prompt_kernel, layer, TT, nt, n * nt),
        grid=(n * nt + 2,),
        in_specs=[pl.BlockSpec((None, TT, D_MODEL),
                               lambda i: (mix_tile(i) // nt, mix_tile(i) % nt, 0))]
        + _mixer_weight_specs(layer) + _ffn_weight_specs(layer),
        out_specs=out_specs,
        out_shape=out_shape,
        scratch_shapes=scratch,
        compiler_params=pltpu.CompilerParams(
            dimension_semantics=("arbitrary",), vmem_limit_bytes=VMEM_LIMIT),
        name=f"layer_prompt_l{layer}",
    )(x, *mixer_wts, *ffn_wts)


def _mix_sample(layer, x2d, s_conv, s_pool, s_hgrn, s_conf, wts, n_seq):
    m = x2d.shape[0]
    NS = n_seq
    TS = SAMPLE_STEPS
    M = TS * NS

    def state_spec(shape):
        nd = len(shape)
        return pl.BlockSpec((None,) + tuple(shape), lambda i: (layer,) + (0,) * nd,
                            pipeline_mode=pl.Buffered(1))

    def out_spec(shape):
        nd = len(shape)
        return pl.BlockSpec(tuple(shape), lambda i: (0,) * nd, pipeline_mode=pl.Buffered(1))

    state_shapes = [(SC_WIDTH - 1, NS, G), (POOL_BUF, NS, G), (G, DK, NS), (CONF_WIDTH - 1, NS, G)]
    in_specs = ([pl.BlockSpec((M, D_MODEL), lambda i: (i, 0))]
                + [state_spec(s) for s in state_shapes] + _mixer_weight_specs(layer))
    out_specs = tuple([pl.BlockSpec((M, D_MODEL), lambda i: (i, 0))]
                      + [out_spec(s) for s in state_shapes])
    out_shape = tuple([jax.ShapeDtypeStruct((m, D_MODEL), F32)]
                      + [jax.ShapeDtypeStruct(s, F32) for s in state_shapes])
    scratch = [
        pltpu.VMEM((M, D_IN), F32),
        pltpu.VMEM((M, G), F32),
        pltpu.VMEM((M, G), F32),
        pltpu.VMEM((TS, G, NS), F32),
        pltpu.VMEM((TS, G, NS), F32),
        pltpu.VMEM((TS, G, NS), F32),
        pltpu.VMEM((TS, G, NS), F32),
        pltpu.VMEM((TS, G, NS), F32),
        pltpu.VMEM((M, D_MODEL), BF16),
    ]
    return pl.pallas_call(
        functools.partial(_mix_sample_kernel, layer, NS, TS, PAST_LEN),
        grid=(m // M,),
        in_specs=in_specs,
        out_specs=out_specs,
        out_shape=out_shape,
        scratch_shapes=scratch,
        compiler_params=pltpu.CompilerParams(
            dimension_semantics=("arbitrary",), vmem_limit_bytes=VMEM_LIMIT),
        name=f"mix_sample_l{layer}",
    )(x2d, s_conv, s_pool, s_hgrn, s_conf, *wts)


def _ffn(layer, x2d, ffn_wts, tag):
    m = x2d.shape[0]
    TM = min(FFN_TILE, m)
    assert m % TM == 0
    return pl.pallas_call(
        _ffn_kernel,
        grid=(m // TM,),
        in_specs=[pl.BlockSpec((TM, D_MODEL), lambda i: (i, 0))] + _ffn_weight_specs(layer),
        out_specs=pl.BlockSpec((TM, D_MODEL), lambda i: (i, 0)),
        out_shape=jax.ShapeDtypeStruct((m, D_MODEL), F32),
        compiler_params=pltpu.CompilerParams(
            dimension_semantics=("arbitrary",), vmem_limit_bytes=VMEM_LIMIT),
        name=f"ffn_{tag}_l{layer}",
    )(x2d, *ffn_wts)


def kernel(x_prompt, x_sample, state_conv, state_pool, state_hgrn, state_conf, norm_mix_pre, norm_mix_post, w_in, conv_w, pool_w, pool_scale, hgrn_lb, hgrn_norm, conf_dw, conf_b, conf_ln_g, conf_ln_b, w_out, norm_ffn_pre, norm_ffn_post, w_gate, w_up, w_down):
    def row(a):
        return a.reshape(DEPTH, 1, a.shape[-1])

    eye = jnp.eye(G // POOL_CH, dtype=pool_w.dtype)
    pool_bd = (pool_w[:, :, :, None, :] * eye[None, :, None, :, None]).reshape(DEPTH, G, G)
    mixer_wts = (row(norm_mix_pre), w_in.astype(BF16), conv_w, pool_bd.astype(BF16),
                 row(pool_scale), hgrn_lb, row(hgrn_norm), conf_dw, row(conf_b),
                 row(conf_ln_g), row(conf_ln_b), w_out.astype(BF16), row(norm_mix_post))
    ffn_wts = (row(norm_ffn_pre), w_gate.astype(BF16), w_up.astype(BF16), w_down.astype(BF16),
               row(norm_ffn_post))

    ns, ts, _ = x_sample.shape
    xs = x_sample.transpose(1, 0, 2).reshape(ts * ns, D_MODEL)
    sc_t = state_conv.transpose(1, 2, 0, 3)
    sp_t = state_pool.transpose(1, 2, 0, 3)
    sf_t = state_conf.transpose(1, 2, 0, 3)
    sh_t = state_hgrn.transpose(1, 2, 3, 4, 0).reshape(DEPTH, G, DK, ns)
    xp = x_prompt
    p_states, s_states = [], []
    for layer in range(DEPTH):
        xp, *stp = _layer_prompt(layer, xp, mixer_wts, ffn_wts)
        p_states.append(stp)
        xs, *sts = _mix_sample(layer, xs, sc_t, sp_t, sh_t, sf_t, mixer_wts, ns)
        s_states.append(sts)
        xs = _ffn(layer, xs, ffn_wts, "sample")
    xs = xs.reshape(ts, ns, D_MODEL).transpose(1, 0, 2)

    def stack(states, i):
        return jnp.stack([states[layer][i] for layer in range(DEPTH)], axis=1)

    def stack_t(i):
        return jnp.stack([s_states[layer][i] for layer in range(DEPTH)], axis=0).transpose(2, 0, 1, 3)

    hgrn_s = jnp.stack([s_states[layer][2] for layer in range(DEPTH)], axis=0)
    hgrn_s = hgrn_s.reshape(DEPTH, HEADS, DK, DK, ns).transpose(4, 0, 1, 2, 3)

    return (xp, xs,
            stack(p_states, 0), stack(p_states, 1), stack(p_states, 2), stack(p_states, 3),
            stack_t(0), stack_t(1), hgrn_s, stack_t(3))
```

```python
import functools

import jax
import jax.numpy as jnp
from jax import lax
from jax.experimental import pallas as pl
from jax.experimental.pallas import tpu as pltpu

F32 = jnp.float32
BF16 = jnp.bfloat16

D_MODEL = 1024
DEPTH = 2
PAST_LEN = 16384
G = 256
N_BLOCKS = 10
D_IN = N_BLOCKS * G
SC_WIDTH = 3
POOL_BUF = 15
POOL_CH = 64
HEADS = 4
DK = 64
CONF_WIDTH = 31
D_FF = 2816
EPS = 1e-6
F_MIN = 1e-20

HGRN_CHUNK = 64
FAST_DECAY_LIMIT = 60.0
PROMPT_TILE = 256
SAMPLE_STEPS = 2
FFN_TILE = 512
FFN_COLS = 256
CAST_ROWS = 16
CONV_ROWS = 64
VMEM_LIMIT = 56 * 1024 * 1024


def _sigmoid(x):
    return jax.nn.sigmoid(x)


def _silu(x):
    return x * jax.nn.sigmoid(x)


def _rmsnorm(x, g):
    ms = jnp.mean(x * x, axis=-1, keepdims=True)
    return x * lax.rsqrt(ms + EPS) * g


def _head_block_mask(rows, cols, row_block, col_block):
    r = lax.broadcasted_iota(jnp.int32, (rows, cols), 0) // row_block
    c = lax.broadcasted_iota(jnp.int32, (rows, cols), 1) // col_block
    return r == c


def _cumsum_rows_mxu(x):
    n = x.shape[0]
    tri = (lax.broadcasted_iota(jnp.int32, (n, n), 0)
           >= lax.broadcasted_iota(jnp.int32, (n, n), 1))
    tri = jnp.where(tri, 1.0, 0.0).astype(BF16)
    hi = x.astype(BF16)
    r1 = x - hi.astype(F32)
    mid = r1.astype(BF16)
    lo = (r1 - mid.astype(F32)).astype(BF16)
    return (jnp.dot(tri, hi, preferred_element_type=F32)
            + jnp.dot(tri, mid, preferred_element_type=F32)
            + jnp.dot(tri, lo, preferred_element_type=F32))


def _lower_bound(lb_all, layer):
    m = jnp.max(lb_all, axis=0, keepdims=True)
    e = jnp.exp(lb_all - m)
    sm = e / jnp.sum(e, axis=0, keepdims=True)
    cs = sm[0:1]
    for i in range(1, layer + 1):
        cs = cs + sm[i:i + 1]
    return cs - sm[0:1]


def _hgrn_gates(zq, zf, zi, lower):
    q = _silu(zq)
    f = lower + (1.0 - lower) * _sigmoid(zf)
    logf = jnp.log(jnp.maximum(f, F_MIN))
    return q, 1.0 - f, zi, logf


def _hgrn_state_terms(q, kk, v, b, st_ref):
    TT = q.shape[0]
    b_end = b[TT - 1:TT, :]
    st = st_ref[...]
    qs = (q * jnp.exp(b)).astype(BF16)
    o_inter = lax.dot_general(qs, st.astype(BF16), (((1,), (1,)), ((), ())),
                              preferred_element_type=F32)
    kh = (kk * jnp.exp(b_end - b)).astype(BF16)
    upd = lax.dot_general(v.astype(BF16), kh, (((0,), (0,)), ((), ())),
                          preferred_element_type=F32)
    bd = _head_block_mask(G, G, DK, DK)
    st_ref[...] = st * jnp.exp(b_end) + jnp.where(bd, upd, 0.0)
    return o_inter


def _hgrn_refs(b, C):
    refs = []
    span = None
    for j in range(b.shape[0] // C):
        first = b[j * C:j * C + 1, :]
        last = b[(j + 1) * C - 1:(j + 1) * C, :]
        refs.append(0.5 * (first + last))
        half = jnp.max(0.5 * (first - last))
        span = half if span is None else jnp.maximum(span, half)
    return refs, span


def _hgrn_fast_chunk(q_tgt, b_tgt, kk_src, v_src, r, C):
    nt = q_tgt.shape[0]
    rows_mask = _head_block_mask(HEADS * C, G, C, DK)
    qz = (q_tgt * jnp.exp(b_tgt - r)).astype(BF16)
    ke = kk_src * jnp.exp(r - b_tgt[0:C])
    kebd = jnp.where(rows_mask, jnp.concatenate([ke] * HEADS, axis=0), 0.0).astype(BF16)
    attn = lax.dot_general(qz, kebd, (((1,), (1,)), ((), ())),
                           preferred_element_type=F32)
    t_idx = lax.broadcasted_iota(jnp.int32, (nt, HEADS * C), 0)
    s_idx = lax.broadcasted_iota(jnp.int32, (nt, HEADS * C), 1) % C
    attn = jnp.where(t_idx >= s_idx, attn, 0.0).astype(BF16)
    vbd = jnp.where(rows_mask, jnp.concatenate([v_src] * HEADS, axis=0), 0.0).astype(BF16)
    return jnp.dot(attn, vbd, preferred_element_type=F32)


def _hgrn_exact_attn(hb_ref, hq_ref, kk, v, o_ref, row0, TT):
    b = hb_ref[0:TT, :]
    ones_bd = jnp.where(_head_block_mask(G, G, DK, DK), 1.0, 0.0).astype(BF16)
    s_row = lax.broadcasted_iota(jnp.int32, (TT, G), 0)

    def body(t, carry):
        bt = hb_ref[pl.ds(t, 1), :]
        qt = hq_ref[pl.ds(t, 1), :]
        e = jnp.where(s_row <= t, qt * kk * jnp.exp(jnp.minimum(bt - b, 0.0)), 0.0)
        a = jnp.dot(e.astype(BF16), ones_bd, preferred_element_type=F32)
        o_ref[pl.ds(row0 + t, 1), :] = jnp.sum(a * v, axis=0, keepdims=True)
        return carry

    lax.fori_loop(0, TT, body, 0)


def _head_norm_gate(o, zg, hnorm):
    ones_bd = jnp.where(_head_block_mask(G, G, DK, DK), 1.0, 0.0).astype(BF16)
    o2 = o * o
    hi = o2.astype(BF16)
    lo = (o2 - hi.astype(F32)).astype(BF16)
    ssq = (jnp.dot(hi, ones_bd, preferred_element_type=F32)
           + jnp.dot(lo, ones_bd, preferred_element_type=F32))
    return o * lax.rsqrt(ssq * (1.0 / DK) + EPS) * hnorm * _silu(zg)


def _pool_select(s2, s4, s8, s16, pos):
    shape = s2.shape
    grp = lax.broadcasted_iota(jnp.int32, shape, len(shape) - 1) // POOL_CH
    ssum = jnp.where(grp == 0, s2, jnp.where(grp == 1, s4, jnp.where(grp == 2, s8, s16)))
    win = jnp.where(grp == 0, 2, jnp.where(grp == 1, 4, jnp.where(grp == 2, 8, 16)))
    cnt = jnp.minimum(pos + 1, win).astype(F32)
    return ssum / cnt


def _conf_tail(z, cb, lng, lnb):
    z = z + cb
    mu = jnp.mean(z, axis=-1, keepdims=True)
    zc = z - mu
    var = jnp.mean(zc * zc, axis=-1, keepdims=True)
    return _silu(zc * lax.rsqrt(var + EPS) * lng + lnb)


def _ffn_block(x, npre, wg, wu, wd, npost):
    h = _rmsnorm(x, npre).astype(BF16)
    g = jnp.dot(h, wg, preferred_element_type=F32)
    u = jnp.dot(h, wu, preferred_element_type=F32)
    a = (_silu(g) * u).astype(BF16)
    ff = jnp.dot(a, wd, preferred_element_type=F32)
    return x + _rmsnorm(ff, npost)


def _layer_prompt_kernel(layer, TT, nt, n_tiles, n_cast, *refs):
    (x_ref, npre_ref, win_ref, convw_ref, poolbd_ref, pscale_ref, lb_ref,
     hnorm_ref, cdw_ref, cb_ref, lng_ref, lnb_ref, wout_ref, npost_ref,
     fpre_ref, wg_ref, wu_ref, wd_ref, fpost_ref) = refs[:19]
    cast_in = refs[19:19 + n_cast]
    y_ref, oconv_ref, opool_ref, ohgrn_ref, oconf_ref = refs[19 + n_cast:24 + n_cast]
    cast_out = refs[24 + n_cast:24 + 2 * n_cast]
    (p_ref, ea_ref, eb_ref, ed_ref, sh_ref, st_ref, hb_ref, hq_ref, hk_ref, oi_ref,
     o_ref, cat_ref, x1_ref, hm_ref, hf_ref, a_ref, ff_ref, mix_ref) = refs[24 + 2 * n_cast:]
    i = pl.program_id(0)
    t = i % nt
    slot = i % 3

    @pl.when(i == 0)
    def _first():
        x1_ref[1] = jnp.zeros((TT, D_MODEL), F32)
        x1_ref[2] = jnp.zeros((TT, D_MODEL), F32)
        a_ref[0] = jnp.zeros((TT, D_FF), BF16)

    @pl.when((t == 0) & (i < n_tiles))
    def _new_sequence():
        ea_ref[0:8, :] = jnp.zeros((8, G), F32)
        eb_ref[0:16, :] = jnp.zeros((16, G), F32)
        ed_ref[0:32, :] = jnp.zeros((32, G), F32)
        st_ref[...] = jnp.zeros((G, G), F32)

    lower = _lower_bound(lb_ref[...], layer)
    n_chunks = TT // HGRN_CHUNK
    slot_up = (i + 2) % 3
    slot_down = (i + 1) % 3
    a_new = (i + 1) % 2
    a_old = i % 2

    def f_norm():
        hf_ref[...] = _rmsnorm(x1_ref[slot_up], fpre_ref[...]).astype(BF16)

    def f_gate_up(j):
        cols = slice(j * FFN_COLS, (j + 1) * FFN_COLS)
        hf = hf_ref[...]
        g = jnp.dot(hf, wg_ref[:, cols], preferred_element_type=F32)
        u = jnp.dot(hf, wu_ref[:, cols], preferred_element_type=F32)
        a_ref[a_new, :, cols] = (_silu(g) * u).astype(BF16)

    def f_down(k):
        cols = slice(k * G, (k + 1) * G)
        ff_ref[:, cols] = jnp.dot(a_ref[a_old], wd_ref[:, cols], preferred_element_type=F32)

    def f_out():
        y_ref[...] = x1_ref[slot_down] + _rmsnorm(ff_ref[...], fpost_ref[...])

    def cast_next():
        for src, dst in zip(cast_in, cast_out):
            dst[...] = src[...].astype(BF16)

    def m_norm():
        hm_ref[...] = _rmsnorm(x_ref[...], npre_ref[...]).astype(BF16)

    def m_proj(blk):
        cols = slice(blk * G, (blk + 1) * G)
        p_ref[:, cols] = jnp.dot(hm_ref[...], win_ref[:, cols], preferred_element_type=F32)

    def m_conv():
        cu = p_ref[:, G:2 * G] * p_ref[:, 2 * G:3 * G]
        ea_ref[8:8 + TT, :] = cu
        ya = (convw_ref[2:3, :] * cu + convw_ref[1:2, :] * ea_ref[7:7 + TT, :]
              + convw_ref[0:1, :] * ea_ref[6:6 + TT, :])
        cat_ref[:, 0:G] = (p_ref[:, 0:G] * ya).astype(BF16)
        oconv_ref[...] = ea_ref[TT + 6:TT + 8, :]
        ea_ref[0:8, :] = ea_ref[TT:TT + 8, :]

    def m_pool():
        pp = p_ref[:, 3 * G:4 * G]
        eb_ref[16:16 + TT, :] = pp
        e = eb_ref[...]
        s2 = e + pltpu.roll(e, 1, 0)
        s4 = s2 + pltpu.roll(s2, 2, 0)
        s8 = s4 + pltpu.roll(s4, 4, 0)
        s16 = s8 + pltpu.roll(s8, 8, 0)
        pos = t * TT + lax.broadcasted_iota(jnp.int32, (TT, G), 0)
        mean = _pool_select(s2[16:], s4[16:], s8[16:], s16[16:], pos)
        yb = jnp.dot((mean - pp).astype(BF16), poolbd_ref[...], preferred_element_type=F32)
        cat_ref[:, G:2 * G] = (yb * pscale_ref[...]).astype(BF16)
        opool_ref[...] = eb_ref[TT + 1:TT + 16, :]
        eb_ref[0:16, :] = eb_ref[TT:TT + 16, :]

    hg = {}

    def m_hgrn_gates():
        q, kk, _, logf = _hgrn_gates(p_ref[:, 4 * G:5 * G], p_ref[:, 5 * G:6 * G],
                                     p_ref[:, 6 * G:7 * G], lower)
        hq_ref[...] = q
        hk_ref[...] = kk
        hb_ref[...] = _cumsum_rows_mxu(logf)

    def m_hgrn_state():
        b = hb_ref[...]
        o_inter = _hgrn_state_terms(hq_ref[...], hk_ref[...], p_ref[:, 6 * G:7 * G], b, st_ref)
        oi_ref[...] = o_inter
        o_ref[...] = o_inter
        hg["refs"], hg["span"] = _hgrn_refs(b, HGRN_CHUNK)

    def m_hgrn_chunk(j):
        lo, hi = j * HGRN_CHUNK, (j + 1) * HGRN_CHUNK
        contrib = _hgrn_fast_chunk(hq_ref[lo:TT, :], hb_ref[lo:TT, :], hk_ref[lo:hi, :],
                                   p_ref[lo:hi, 6 * G:7 * G], hg["refs"][j], HGRN_CHUNK)
        o_ref[lo:TT, :] = o_ref[lo:TT, :] + contrib

    def m_hgrn_out():
        yc = _head_norm_gate(o_ref[...], p_ref[:, 7 * G:8 * G], hnorm_ref[...])
        cat_ref[:, 2 * G:3 * G] = yc.astype(BF16)

    def m_glu():
        ed_ref[32:32 + TT, :] = p_ref[:, 8 * G:9 * G] * _sigmoid(p_ref[:, 9 * G:10 * G])
        ed = ed_ref[...]
        for r in range(1, 8):
            sh_ref[r - 1] = pltpu.roll(ed, TT + 32 - r, 0)

    def m_conf(rb):
        base = rb * CONV_ROWS
        acc = None
        for j in range(CONF_WIDTH):
            a8, r = divmod(2 + j, 8)
            lo = base + 8 * a8
            src = ed_ref[lo:lo + CONV_ROWS, :] if r == 0 else sh_ref[r - 1, lo:lo + CONV_ROWS, :]
            term = cdw_ref[j:j + 1, :] * src
            acc = term if acc is None else acc + term
        yd = _conf_tail(acc, cb_ref[...], lng_ref[...], lnb_ref[...])
        cat_ref[base:base + CONV_ROWS, 3 * G:4 * G] = yd.astype(BF16)

    def m_conf_tail():
        oconf_ref[...] = ed_ref[TT + 2:TT + 32, :]
        ed_ref[0:32, :] = ed_ref[TT:TT + 32, :]

    def m_out():
        mix_ref[...] = jnp.dot(cat_ref[...], wout_ref[...], preferred_element_type=F32)

    def m_out_norm():
        x1_ref[slot] = x_ref[...] + _rmsnorm(mix_ref[...], npost_ref[...])

    n_gu = D_FF // FFN_COLS
    gate_up = [functools.partial(f_gate_up, j) for j in range(n_gu)]
    proj = [functools.partial(m_proj, blk) for blk in range(N_BLOCKS)]
    down = [functools.partial(f_down, k) for k in range(D_MODEL // G)]
    conf = [functools.partial(m_conf, rb) for rb in range(TT // CONV_ROWS)]
    chunk = [functools.partial(m_hgrn_chunk, j) for j in range(n_chunks)]
    g = gate_up
    schedule = [
        down[0], f_norm, down[1], m_norm, down[2], cast_next, down[3],
        proj[8], proj[9], f_out,
        g[0], m_glu, proj[3],
        g[1], conf[0], proj[0], proj[1], proj[2],
        g[2], conf[1], proj[4], proj[5], proj[6],
        g[3], conf[2], proj[7],
        g[4], conf[3], m_conf_tail,
        g[5], m_conv, m_pool,
        g[6], m_hgrn_gates,
        g[7], m_hgrn_state,
        g[8], chunk[0], chunk[1],
        g[9], chunk[2], chunk[3],
        m_hgrn_out, m_out, g[10], m_out_norm,
    ]
    assert n_gu == 11 and n_chunks == 4 and len(conf) == 4 and len(down) == 4
    for piece in schedule:
        piece()

    @pl.when(hg["span"] >= FAST_DECAY_LIMIT)
    def _redo_exact():
        f = lower + (1.0 - lower) * _sigmoid(p_ref[:, 5 * G:6 * G])
        _hgrn_exact_attn(hb_ref, hq_ref, 1.0 - f, p_ref[:, 6 * G:7 * G], o_ref, 0, TT)
        yce = _head_norm_gate(o_ref[...] + oi_ref[...], p_ref[:, 7 * G:8 * G], hnorm_ref[...])
        cat_ref[:, 2 * G:3 * G] = yce.astype(BF16)
        mixe = jnp.dot(cat_ref[...], wout_ref[...], preferred_element_type=F32)
        x1_ref[slot] = x_ref[...] + _rmsnorm(mixe, npost_ref[...])

    @pl.when((t == nt - 1) & (i < n_tiles))
    def _state_out():
        s = st_ref[...].T
        for hh in range(HEADS):
            ohgrn_ref[hh] = s[hh * DK:(hh + 1) * DK, hh * DK:(hh + 1) * DK]


def _mix_sample_kernel(layer, NS, TS, start_pos,
                       x_ref, sconv_ref, spool_ref, shgrn_ref, sconf_ref,
                       npre_ref, win_ref, convw_ref, poolbd_ref, pscale_ref, lb_ref,
                       hnorm_ref, cdw_ref, cb_ref, lng_ref, lnb_ref, wout_ref, npost_ref,
                       y_ref, oconv_ref, opool_ref, ohgrn_ref, oconf_ref,
                       p_ref, u_ref, pool_ref, qT_ref, fT_ref, kT_ref, vT_ref, oT_ref, cat_ref):
    i = pl.program_id(0)

    @pl.when(i == 0)
    def _load_states():
        oconv_ref[...] = sconv_ref[...]
        opool_ref[...] = spool_ref[...]
        oconf_ref[...] = sconf_ref[...]
        ohgrn_ref[...] = shgrn_ref[...]

    def slab(t):
        return slice(t * NS, (t + 1) * NS)

    x = x_ref[...]
    h = _rmsnorm(x, npre_ref[...]).astype(BF16)
    p_ref[...] = jnp.dot(h, win_ref[...], preferred_element_type=F32)

    def conv_in(j):
        if j < SC_WIDTH - 1:
            return oconv_ref[j]
        rows = slab(j - (SC_WIDTH - 1))
        return p_ref[rows, G:2 * G] * p_ref[rows, 2 * G:3 * G]

    for t in range(TS):
        ya = (convw_ref[0:1, :] * conv_in(t) + convw_ref[1:2, :] * conv_in(t + 1)
              + convw_ref[2:3, :] * conv_in(t + 2))
        cat_ref[slab(t), 0:G] = (p_ref[slab(t), 0:G] * ya).astype(BF16)
    for j in range(SC_WIDTH - 1):
        oconv_ref[j] = conv_in(j + TS)

    def pool_in(j):
        if j < POOL_BUF:
            return opool_ref[j]
        return p_ref[slab(j - POOL_BUF), 3 * G:4 * G]

    for t in range(TS):
        idx = POOL_BUF + t
        run = pool_in(idx)
        sums = {}
        for j in range(1, 16):
            run = run + pool_in(idx - j)
            if j + 1 in (2, 4, 8, 16):
                sums[j + 1] = run
        pos = jnp.full((NS, G), start_pos + i * TS + t, jnp.int32)
        mean = _pool_select(sums[2], sums[4], sums[8], sums[16], pos)
        pool_ref[slab(t), :] = mean - pool_in(idx)
    yb = jnp.dot(pool_ref[...].astype(BF16), poolbd_ref[...], preferred_element_type=F32)
    cat_ref[:, G:2 * G] = (yb * pscale_ref[...]).astype(BF16)
    for j in range(POOL_BUF):
        opool_ref[j] = pool_in(j + TS)

    lower = _lower_bound(lb_ref[...], layer)
    q = _silu(p_ref[:, 4 * G:5 * G])
    f = lower + (1.0 - lower) * _sigmoid(p_ref[:, 5 * G:6 * G])
    for t in range(TS):
        qT_ref[t] = q[slab(t)].T
        fT_ref[t] = jnp.maximum(f[slab(t)], F_MIN).T
        kT_ref[t] = (1.0 - f[slab(t)]).T
        vT_ref[t] = p_ref[slab(t), 6 * G:7 * G].T
    for hh in range(HEADS):
        head = slice(hh * DK, (hh + 1) * DK)
        vts = [vT_ref[t, head, :] for t in range(TS)]

        def body(k, accs, hh=hh, vts=vts):
            c = hh * DK + k
            s = ohgrn_ref[c]
            out = []
            for t in range(TS):
                s = fT_ref[t, pl.ds(c, 1), :] * s + kT_ref[t, pl.ds(c, 1), :] * vts[t]
                out.append(accs[t] + qT_ref[t, pl.ds(c, 1), :] * s)
            ohgrn_ref[c] = s
            return tuple(out)

        accs = lax.fori_loop(0, DK, body, tuple(jnp.zeros((DK, NS), F32) for _ in range(TS)),
                             unroll=2)
        for t in range(TS):
            oT_ref[t, head, :] = accs[t]
    o = jnp.concatenate([oT_ref[t].T for t in range(TS)], axis=0)
    yc = _head_norm_gate(o, p_ref[:, 7 * G:8 * G], hnorm_ref[...])
    cat_ref[:, 2 * G:3 * G] = yc.astype(BF16)

    u_ref[...] = p_ref[:, 8 * G:9 * G] * _sigmoid(p_ref[:, 9 * G:10 * G])
    HALF = NS // 2

    def conf_in(j, rows):
        if j < CONF_WIDTH - 1:
            return oconf_ref[j, rows, :]
        base = (j - (CONF_WIDTH - 1)) * NS
        return u_ref[base + rows.start:base + rows.stop, :]

    for t in range(TS):
        for hf in range(2):
            rows = slice(hf * HALF, (hf + 1) * HALF)
            acc = None
            for j in range(CONF_WIDTH):
                term = cdw_ref[j:j + 1, :] * conf_in(t + j, rows)
                acc = term if acc is None else acc + term
            yd = _conf_tail(acc, cb_ref[...], lng_ref[...], lnb_ref[...])
            cat_ref[t * NS + hf * HALF:t * NS + (hf + 1) * HALF, 3 * G:4 * G] = yd.astype(BF16)
    for j in range(CONF_WIDTH - 1):
        oconf_ref[j] = conf_in(j + TS, slice(0, NS))

    mix = jnp.dot(cat_ref[...], wout_ref[...], preferred_element_type=F32)
    y_ref[...] = x + _rmsnorm(mix, npost_ref[...])


def _ffn_kernel(x_ref, npre_ref, wg_ref, wu_ref, wd_ref, npost_ref, y_ref):
    y_ref[...] = _ffn_block(x_ref[...], npre_ref[...], wg_ref[...], wu_ref[...], wd_ref[...],
                            npost_ref[...])


def _layer_spec(shape, layer, single_buffer=False):
    nd = len(shape)

    def imap(*_):
        return (layer,) + (0,) * nd

    if single_buffer:
        return pl.BlockSpec((None,) + tuple(shape), imap, pipeline_mode=pl.Buffered(1))
    return pl.BlockSpec((None,) + tuple(shape), imap)


def _mixer_weight_specs(layer):
    return [
        _layer_spec((1, D_MODEL), layer),
        _layer_spec((D_MODEL, D_IN), 0, True),
        _layer_spec((SC_WIDTH, G), layer),
        _layer_spec((G, G), layer),
        _layer_spec((1, G), layer),
        pl.BlockSpec((DEPTH, G), lambda *_: (0, 0)),
        _layer_spec((1, G), layer),
        _layer_spec((CONF_WIDTH, G), layer),
        _layer_spec((1, G), layer),
        _layer_spec((1, G), layer),
        _layer_spec((1, G), layer),
        _layer_spec((D_MODEL, D_MODEL), 0, True),
        _layer_spec((1, D_MODEL), layer),
    ]


def _ffn_weight_specs(layer):
    return [
        _layer_spec((1, D_MODEL), layer),
        _layer_spec((D_MODEL, D_FF), 0, True),
        _layer_spec((D_MODEL, D_FF), 0, True),
        _layer_spec((D_FF, D_MODEL), 0, True),
        _layer_spec((1, D_MODEL), layer),
    ]


def _layer_prompt(layer, x, mixer_wts, ffn_wts, next_f32):
    n, seq, _ = x.shape
    TT = PROMPT_TILE
    nt = seq // TT
    last = n * nt - 1
    n_steps = n * nt + 2

    def mix_tile(i):
        return jnp.minimum(i, last)

    def ffn_tile(i):
        return jnp.maximum(i - 2, 0)

    out_shape = [
        jax.ShapeDtypeStruct((n, seq, D_MODEL), F32),
        jax.ShapeDtypeStruct((n, SC_WIDTH - 1, G), F32),
        jax.ShapeDtypeStruct((n, POOL_BUF, G), F32),
        jax.ShapeDtypeStruct((n, HEADS, DK, DK), F32),
        jax.ShapeDtypeStruct((n, CONF_WIDTH - 1, G), F32),
    ]
    out_specs = [
        pl.BlockSpec((None, TT, D_MODEL), lambda i: (ffn_tile(i) // nt, ffn_tile(i) % nt, 0)),
        pl.BlockSpec((None, SC_WIDTH - 1, G), lambda i: (mix_tile(i) // nt, 0, 0)),
        pl.BlockSpec((None, POOL_BUF, G), lambda i: (mix_tile(i) // nt, 0, 0)),
        pl.BlockSpec((None, HEADS, DK, DK), lambda i: (mix_tile(i) // nt, 0, 0, 0)),
        pl.BlockSpec((None, CONF_WIDTH - 1, G), lambda i: (mix_tile(i) // nt, 0, 0)),
    ]
    cast_in_specs = []
    for w in next_f32:
        _, rows, cols = w.shape
        blk = next(b for b in range(CAST_ROWS, rows + 1, CAST_ROWS)
                   if rows % b == 0 and rows // b <= n_steps)
        n_blk = rows // blk

        def in_map(i, n_blk=n_blk):
            return (layer + 1, jnp.minimum(i, n_blk - 1), 0)

        def out_map(i, n_blk=n_blk):
            return (0, jnp.minimum(i, n_blk - 1), 0)

        cast_in_specs.append(pl.BlockSpec((None, blk, cols), in_map))
        out_specs.append(pl.BlockSpec((None, blk, cols), out_map))
        out_shape.append(jax.ShapeDtypeStruct((1, rows, cols), BF16))
    scratch = [
        pltpu.VMEM((TT, D_IN), F32),
        pltpu.VMEM((8 + TT, G), F32),
        pltpu.VMEM((16 + TT, G), F32),
        pltpu.VMEM((32 + TT, G), F32),
        pltpu.VMEM((7, 32 + TT, G), F32),
        pltpu.VMEM((G, G), F32),
        pltpu.VMEM((TT, G), F32),
        pltpu.VMEM((TT, G), F32),
        pltpu.VMEM((TT, G), F32),
        pltpu.VMEM((TT, G), F32),
        pltpu.VMEM((TT, G), F32),
        pltpu.VMEM((TT, D_MODEL), BF16),
        pltpu.VMEM((3, TT, D_MODEL), F32),
        pltpu.VMEM((TT, D_MODEL), BF16),
        pltpu.VMEM((TT, D_MODEL), BF16),
        pltpu.VMEM((2, TT, D_FF), BF16),
        pltpu.VMEM((TT, D_MODEL), F32),
        pltpu.VMEM((TT, D_MODEL), F32),
    ]
    return pl.pallas_call(
        functools.partial(_layer_prompt_kernel, layer, TT, nt, n * nt, len(next_f32)),
        grid=(n_steps,),
        in_specs=[pl.BlockSpec((None, TT, D_MODEL),
                               lambda i: (mix_tile(i) // nt, mix_tile(i) % nt, 0))]
        + _mixer_weight_specs(layer) + _ffn_weight_specs(layer) + cast_in_specs,
        out_specs=tuple(out_specs),
        out_shape=tuple(out_shape),
        scratch_shapes=scratch,
        compiler_params=pltpu.CompilerParams(
            dimension_semantics=("arbitrary",), vmem_limit_bytes=VMEM_LIMIT),
        name=f"layer_prompt_l{layer}",
    )(x, *mixer_wts, *ffn_wts, *next_f32)


def _mix_sample(layer, x2d, s_conv, s_pool, s_hgrn, s_conf, wts, n_seq):
    m = x2d.shape[0]
    NS = n_seq
    TS = SAMPLE_STEPS
    M = TS * NS

    def state_spec(shape):
        nd = len(shape)
        return pl.BlockSpec((None,) + tuple(shape), lambda i: (layer,) + (0,) * nd,
                            pipeline_mode=pl.Buffered(1))

    def out_spec(shape):
        nd = len(shape)
        return pl.BlockSpec(tuple(shape), lambda i: (0,) * nd, pipeline_mode=pl.Buffered(1))

    state_shapes = [(SC_WIDTH - 1, NS, G), (POOL_BUF, NS, G), (G, DK, NS), (CONF_WIDTH - 1, NS, G)]
    in_specs = ([pl.BlockSpec((M, D_MODEL), lambda i: (i, 0))]
                + [state_spec(s) for s in state_shapes] + _mixer_weight_specs(layer))
    out_specs = tuple([pl.BlockSpec((M, D_MODEL), lambda i: (i, 0))]
                      + [out_spec(s) for s in state_shapes])
    out_shape = tuple([jax.ShapeDtypeStruct((m, D_MODEL), F32)]
                      + [jax.ShapeDtypeStruct(s, F32) for s in state_shapes])
    scratch = [
        pltpu.VMEM((M, D_IN), F32),
        pltpu.VMEM((M, G), F32),
        pltpu.VMEM((M, G), F32),
        pltpu.VMEM((TS, G, NS), F32),
        pltpu.VMEM((TS, G, NS), F32),
        pltpu.VMEM((TS, G, NS), F32),
        pltpu.VMEM((TS, G, NS), F32),
        pltpu.VMEM((TS, G, NS), F32),
        pltpu.VMEM((M, D_MODEL), BF16),
    ]
    return pl.pallas_call(
        functools.partial(_mix_sample_kernel, layer, NS, TS, PAST_LEN),
        grid=(m // M,),
        in_specs=in_specs,
        out_specs=out_specs,
        out_shape=out_shape,
        scratch_shapes=scratch,
        compiler_params=pltpu.CompilerParams(
            dimension_semantics=("arbitrary",), vmem_limit_bytes=VMEM_LIMIT),
        name=f"mix_sample_l{layer}",
    )(x2d, s_conv, s_pool, s_hgrn, s_conf, *wts)


def _ffn(layer, x2d, ffn_wts, tag):
    m = x2d.shape[0]
    TM = min(FFN_TILE, m)
    assert m % TM == 0
    return pl.pallas_call(
        _ffn_kernel,
        grid=(m // TM,),
        in_specs=[pl.BlockSpec((TM, D_MODEL), lambda i: (i, 0))] + _ffn_weight_specs(layer),
        out_specs=pl.BlockSpec((TM, D_MODEL), lambda i: (i, 0)),
        out_shape=jax.ShapeDtypeStruct((m, D_MODEL), F32),
        compiler_params=pltpu.CompilerParams(
            dimension_semantics=("arbitrary",), vmem_limit_bytes=VMEM_LIMIT),
        name=f"ffn_{tag}_l{layer}",
    )(x2d, *ffn_wts)


def kernel(x_prompt, x_sample, state_conv, state_pool, state_hgrn, state_conf, norm_mix_pre, norm_mix_post, w_in, conv_w, pool_w, pool_scale, hgrn_lb, hgrn_norm, conf_dw, conf_b, conf_ln_g, conf_ln_b, w_out, norm_ffn_pre, norm_ffn_post, w_gate, w_up, w_down):
    def row(a):
        return a.reshape(DEPTH, 1, a.shape[-1])

    eye = jnp.eye(G // POOL_CH, dtype=pool_w.dtype)
    pool_bd = (pool_w[:, :, :, None, :] * eye[None, :, None, :, None]).reshape(DEPTH, G, G)
    pool_bd = pool_bd.astype(BF16)
    big_f32 = (w_in, w_out, w_gate, w_up, w_down)
    big = tuple(w[0:1].astype(BF16) for w in big_f32)

    def mixer_weights(big):
        return (row(norm_mix_pre), big[0], conv_w, pool_bd, row(pool_scale), hgrn_lb,
                row(hgrn_norm), conf_dw, row(conf_b), row(conf_ln_g), row(conf_ln_b), big[1],
                row(norm_mix_post))

    def ffn_weights(big):
        return (row(norm_ffn_pre), big[2], big[3], big[4], row(norm_ffn_post))

    ns, ts, _ = x_sample.shape
    xs = x_sample.transpose(1, 0, 2).reshape(ts * ns, D_MODEL)
    sc_t = state_conv.transpose(1, 2, 0, 3)
    sp_t = state_pool.transpose(1, 2, 0, 3)
    sf_t = state_conf.transpose(1, 2, 0, 3)
    sh_t = state_hgrn.transpose(1, 2, 3, 4, 0).reshape(DEPTH, G, DK, ns)
    xp = x_prompt
    p_states, s_states = [], []
    for layer in range(DEPTH):
        mixer_wts, ffn_wts = mixer_weights(big), ffn_weights(big)
        next_f32 = big_f32 if layer + 1 < DEPTH else ()
        xp, *rest = _layer_prompt(layer, xp, mixer_wts, ffn_wts, next_f32)
        p_states.append(rest[:4])
        big = tuple(rest[4:])
        xs, *sts = _mix_sample(layer, xs, sc_t, sp_t, sh_t, sf_t, mixer_wts, ns)
        s_states.append(sts)
        xs = _ffn(layer, xs, ffn_wts, "sample")
    xs = xs.reshape(ts, ns, D_MODEL).transpose(1, 0, 2)

    def stack(states, i):
        return jnp.stack([states[layer][i] for layer in range(DEPTH)], axis=1)

    def stack_t(i):
        return jnp.stack([s_states[layer][i] for layer in range(DEPTH)], axis=0).transpose(2, 0, 1, 3)

    hgrn_s = jnp.stack([s_states[layer][2] for layer in range(DEPTH)], axis=0)
    hgrn_s = hgrn_s.reshape(DEPTH, HEADS, DK, DK, ns).transpose(4, 0, 1, 2, 3)

    return (xp, xs,
            stack(p_states, 0), stack(p_states, 1), stack(p_states, 2), stack(p_states, 3),
            stack_t(0), stack_t(1), hgrn_s, stack_t(3))
```

```python
import functools

import jax
import jax.numpy as jnp
from jax import lax
from jax.experimental import pallas as pl
from jax.experimental.pallas import tpu as pltpu

F32 = jnp.float32
BF16 = jnp.bfloat16

D_MODEL = 1024
DEPTH = 2
PAST_LEN = 16384
G = 256
N_BLOCKS = 10
D_IN = N_BLOCKS * G
SC_WIDTH = 3
POOL_WINDOWS = (2, 4, 8, 16)
POOL_BUF = max(POOL_WINDOWS) - 1
POOL_CH = 64
HEADS = 4
DK = 64
CONF_WIDTH = 31
D_FF = 2816
EPS = 1e-6
F_MIN = 1e-20

SUBLANES = 8


def _round_up(n, m):
    return -(-n // m) * m


CONV_PAD = _round_up(SC_WIDTH - 1, SUBLANES)
POOL_PAD = _round_up(POOL_BUF, SUBLANES)
CONF_PAD = _round_up(CONF_WIDTH - 1, SUBLANES)

HGRN_CHUNK = 64
FAST_DECAY_LIMIT = 60.0
PROMPT_TILE = 256
SAMPLE_STEPS = 2
FFN_TILE = 256
FFN_COLS = 256
CAST_ROWS = 16
CONV_ROWS = 64
VMEM_LIMIT = 56 * 1024 * 1024


def _sigmoid(x):
    return jax.nn.sigmoid(x)


def _silu(x):
    return x * jax.nn.sigmoid(x)


def _rmsnorm(x, g):
    ms = jnp.mean(x * x, axis=-1, keepdims=True)
    return x * lax.rsqrt(ms + EPS) * g


def _head_block_mask(rows, cols, row_block, col_block):
    r = lax.broadcasted_iota(jnp.int32, (rows, cols), 0) // row_block
    c = lax.broadcasted_iota(jnp.int32, (rows, cols), 1) // col_block
    return r == c


def _cumsum_rows_mxu(x):
    n = x.shape[0]
    tri = (lax.broadcasted_iota(jnp.int32, (n, n), 0)
           >= lax.broadcasted_iota(jnp.int32, (n, n), 1))
    tri = jnp.where(tri, 1.0, 0.0).astype(BF16)
    hi = x.astype(BF16)
    r1 = x - hi.astype(F32)
    mid = r1.astype(BF16)
    lo = (r1 - mid.astype(F32)).astype(BF16)
    return (jnp.dot(tri, hi, preferred_element_type=F32)
            + jnp.dot(tri, mid, preferred_element_type=F32)
            + jnp.dot(tri, lo, preferred_element_type=F32))


def _lower_bound(lb_all, layer):
    m = jnp.max(lb_all, axis=0, keepdims=True)
    e = jnp.exp(lb_all - m)
    sm = e / jnp.sum(e, axis=0, keepdims=True)
    cs = sm[0:1]
    for i in range(1, layer + 1):
        cs = cs + sm[i:i + 1]
    return cs - sm[0:1]


def _hgrn_gates(zq, zf, zi, lower):
    q = _silu(zq)
    f = lower + (1.0 - lower) * _sigmoid(zf)
    logf = jnp.log(jnp.maximum(f, F_MIN))
    return q, 1.0 - f, zi, logf


def _hgrn_state_terms(q, kk, v, b, st_ref):
    TT = q.shape[0]
    b_end = b[TT - 1:TT, :]
    st = st_ref[...]
    qs = (q * jnp.exp(b)).astype(BF16)
    o_inter = lax.dot_general(qs, st.astype(BF16), (((1,), (1,)), ((), ())),
                              preferred_element_type=F32)
    kh = (kk * jnp.exp(b_end - b)).astype(BF16)
    upd = lax.dot_general(v.astype(BF16), kh, (((0,), (0,)), ((), ())),
                          preferred_element_type=F32)
    bd = _head_block_mask(G, G, DK, DK)
    st_ref[...] = st * jnp.exp(b_end) + jnp.where(bd, upd, 0.0)
    return o_inter


def _hgrn_refs(b, C):
    refs = []
    span = None
    for j in range(b.shape[0] // C):
        first = b[j * C:j * C + 1, :]
        last = b[(j + 1) * C - 1:(j + 1) * C, :]
        refs.append(0.5 * (first + last))
        half = jnp.max(0.5 * (first - last))
        span = half if span is None else jnp.maximum(span, half)
    return refs, span


def _hgrn_fast_chunk(q_tgt, b_tgt, kk_src, v_src, r, C):
    nt = q_tgt.shape[0]
    rows_mask = _head_block_mask(HEADS * C, G, C, DK)
    qz = (q_tgt * jnp.exp(b_tgt - r)).astype(BF16)
    ke = kk_src * jnp.exp(r - b_tgt[0:C])
    kebd = jnp.where(rows_mask, jnp.concatenate([ke] * HEADS, axis=0), 0.0).astype(BF16)
    attn = lax.dot_general(qz, kebd, (((1,), (1,)), ((), ())),
                           preferred_element_type=F32)
    t_idx = lax.broadcasted_iota(jnp.int32, (nt, HEADS * C), 0)
    s_idx = lax.broadcasted_iota(jnp.int32, (nt, HEADS * C), 1) % C
    attn = jnp.where(t_idx >= s_idx, attn, 0.0).astype(BF16)
    vbd = jnp.where(rows_mask, jnp.concatenate([v_src] * HEADS, axis=0), 0.0).astype(BF16)
    return jnp.dot(attn, vbd, preferred_element_type=F32)


def _hgrn_exact_attn(hb_ref, hq_ref, kk, v, o_ref, row0, TT):
    b = hb_ref[0:TT, :]
    ones_bd = jnp.where(_head_block_mask(G, G, DK, DK), 1.0, 0.0).astype(BF16)
    s_row = lax.broadcasted_iota(jnp.int32, (TT, G), 0)

    def body(t, carry):
        bt = hb_ref[pl.ds(t, 1), :]
        qt = hq_ref[pl.ds(t, 1), :]
        e = jnp.where(s_row <= t, qt * kk * jnp.exp(jnp.minimum(bt - b, 0.0)), 0.0)
        a = jnp.dot(e.astype(BF16), ones_bd, preferred_element_type=F32)
        o_ref[pl.ds(row0 + t, 1), :] = jnp.sum(a * v, axis=0, keepdims=True)
        return carry

    lax.fori_loop(0, TT, body, 0)


def _head_norm_gate(o, zg, hnorm):
    ones_bd = jnp.where(_head_block_mask(G, G, DK, DK), 1.0, 0.0).astype(BF16)
    o2 = o * o
    hi = o2.astype(BF16)
    lo = (o2 - hi.astype(F32)).astype(BF16)
    ssq = (jnp.dot(hi, ones_bd, preferred_element_type=F32)
           + jnp.dot(lo, ones_bd, preferred_element_type=F32))
    return o * lax.rsqrt(ssq * (1.0 / DK) + EPS) * hnorm * _silu(zg)


def _pool_select(sums, pos):
    shape = sums[POOL_WINDOWS[0]].shape
    grp = lax.broadcasted_iota(jnp.int32, shape, len(shape) - 1) // POOL_CH
    ssum = sums[POOL_WINDOWS[-1]]
    win = jnp.full(shape, POOL_WINDOWS[-1], jnp.int32)
    for gi in range(len(POOL_WINDOWS) - 2, -1, -1):
        ssum = jnp.where(grp == gi, sums[POOL_WINDOWS[gi]], ssum)
        win = jnp.where(grp == gi, POOL_WINDOWS[gi], win)
    cnt = jnp.minimum(pos + 1, win).astype(F32)
    return ssum / cnt


def _conf_tail(z, cb, lng, lnb):
    z = z + cb
    mu = jnp.mean(z, axis=-1, keepdims=True)
    zc = z - mu
    var = jnp.mean(zc * zc, axis=-1, keepdims=True)
    return _silu(zc * lax.rsqrt(var + EPS) * lng + lnb)


def _ffn_block(x, npre, wg, wu, wd, npost):
    h = _rmsnorm(x, npre).astype(BF16)
    g = jnp.dot(h, wg, preferred_element_type=F32)
    u = jnp.dot(h, wu, preferred_element_type=F32)
    a = (_silu(g) * u).astype(BF16)
    ff = jnp.dot(a, wd, preferred_element_type=F32)
    return x + _rmsnorm(ff, npost)


def _layer_prompt_kernel(layer, TT, nt, n_tiles, n_cast, *refs):
    (x_ref, npre_ref, win_ref, convw_ref, poolbd_ref, pscale_ref, lb_ref,
     hnorm_ref, cdw_ref, cb_ref, lng_ref, lnb_ref, wout_ref, npost_ref,
     fpre_ref, wg_ref, wu_ref, wd_ref, fpost_ref) = refs[:19]
    cast_in = refs[19:19 + n_cast]
    y_ref, oconv_ref, opool_ref, ohgrn_ref, oconf_ref = refs[19 + n_cast:24 + n_cast]
    cast_out = refs[24 + n_cast:24 + 2 * n_cast]
    (p_ref, ea_ref, eb_ref, ed_ref, sh_ref, st_ref, hb_ref, hq_ref, hk_ref, oi_ref,
     o_ref, cat_ref, x1_ref, hm_ref, hf_ref, a_ref, ff_ref, mix_ref) = refs[24 + 2 * n_cast:]
    i = pl.program_id(0)
    t = i % nt
    slot = i % 3

    @pl.when(i == 0)
    def _first():
        x1_ref[1] = jnp.zeros((TT, D_MODEL), F32)
        x1_ref[2] = jnp.zeros((TT, D_MODEL), F32)
        a_ref[0] = jnp.zeros((TT, D_FF), BF16)

    @pl.when((t == 0) & (i < n_tiles))
    def _new_sequence():
        ea_ref[0:CONV_PAD, :] = jnp.zeros((CONV_PAD, G), F32)
        eb_ref[0:POOL_PAD, :] = jnp.zeros((POOL_PAD, G), F32)
        ed_ref[0:CONF_PAD, :] = jnp.zeros((CONF_PAD, G), F32)
        st_ref[...] = jnp.zeros((G, G), F32)

    lower = _lower_bound(lb_ref[...], layer)
    n_chunks = TT // HGRN_CHUNK
    slot_up = (i + 2) % 3
    slot_down = (i + 1) % 3
    a_new = (i + 1) % 2
    a_old = i % 2

    def f_norm():
        hf_ref[...] = _rmsnorm(x1_ref[slot_up], fpre_ref[...]).astype(BF16)

    def f_gate_up(j):
        cols = slice(j * FFN_COLS, (j + 1) * FFN_COLS)
        hf = hf_ref[...]
        g = jnp.dot(hf, wg_ref[:, cols], preferred_element_type=F32)
        u = jnp.dot(hf, wu_ref[:, cols], preferred_element_type=F32)
        a_ref[a_new, :, cols] = (_silu(g) * u).astype(BF16)

    def f_down(k):
        cols = slice(k * G, (k + 1) * G)
        ff_ref[:, cols] = jnp.dot(a_ref[a_old], wd_ref[:, cols], preferred_element_type=F32)

    def f_out():
        y_ref[...] = x1_ref[slot_down] + _rmsnorm(ff_ref[...], fpost_ref[...])

    def cast_next():
        for src, dst in zip(cast_in, cast_out):
            dst[...] = src[...].astype(BF16)

    def m_norm():
        hm_ref[...] = _rmsnorm(x_ref[...], npre_ref[...]).astype(BF16)

    def m_proj(blk):
        cols = slice(blk * G, (blk + 1) * G)
        p_ref[:, cols] = jnp.dot(hm_ref[...], win_ref[:, cols], preferred_element_type=F32)

    def m_conv():
        cu = p_ref[:, G:2 * G] * p_ref[:, 2 * G:3 * G]
        ea_ref[CONV_PAD:CONV_PAD + TT, :] = cu
        ya = convw_ref[SC_WIDTH - 1:SC_WIDTH, :] * cu
        for back in range(1, SC_WIDTH):
            w = convw_ref[SC_WIDTH - 1 - back:SC_WIDTH - back, :]
            ya = ya + w * ea_ref[CONV_PAD - back:CONV_PAD - back + TT, :]
        cat_ref[:, 0:G] = (p_ref[:, 0:G] * ya).astype(BF16)
        oconv_ref[...] = ea_ref[TT + CONV_PAD - (SC_WIDTH - 1):TT + CONV_PAD, :]
        ea_ref[0:CONV_PAD, :] = ea_ref[TT:TT + CONV_PAD, :]

    def m_pool():
        pp = p_ref[:, 3 * G:4 * G]
        eb_ref[POOL_PAD:POOL_PAD + TT, :] = pp
        run = eb_ref[...]
        sums = {}
        w = 1
        while w < POOL_WINDOWS[-1]:
            run = run + pltpu.roll(run, w, 0)
            w *= 2
            sums[w] = run[POOL_PAD:]
        pos = t * TT + lax.broadcasted_iota(jnp.int32, (TT, G), 0)
        mean = _pool_select(sums, pos)
        yb = jnp.dot((mean - pp).astype(BF16), poolbd_ref[...], preferred_element_type=F32)
        cat_ref[:, G:2 * G] = (yb * pscale_ref[...]).astype(BF16)
        opool_ref[...] = eb_ref[TT + POOL_PAD - POOL_BUF:TT + POOL_PAD, :]
        eb_ref[0:POOL_PAD, :] = eb_ref[TT:TT + POOL_PAD, :]

    hg = {}

    def m_hgrn_gates():
        q, kk, _, logf = _hgrn_gates(p_ref[:, 4 * G:5 * G], p_ref[:, 5 * G:6 * G],
                                     p_ref[:, 6 * G:7 * G], lower)
        hq_ref[...] = q
        hk_ref[...] = kk
        hb_ref[...] = _cumsum_rows_mxu(logf)

    def m_hgrn_state():
        b = hb_ref[...]
        o_inter = _hgrn_state_terms(hq_ref[...], hk_ref[...], p_ref[:, 6 * G:7 * G], b, st_ref)
        oi_ref[...] = o_inter
        o_ref[...] = o_inter
        hg["refs"], hg["span"] = _hgrn_refs(b, HGRN_CHUNK)

    def m_hgrn_chunk(j):
        lo, hi = j * HGRN_CHUNK, (j + 1) * HGRN_CHUNK
        contrib = _hgrn_fast_chunk(hq_ref[lo:TT, :], hb_ref[lo:TT, :], hk_ref[lo:hi, :],
                                   p_ref[lo:hi, 6 * G:7 * G], hg["refs"][j], HGRN_CHUNK)
        o_ref[lo:TT, :] = o_ref[lo:TT, :] + contrib

    def m_hgrn_out():
        yc = _head_norm_gate(o_ref[...], p_ref[:, 7 * G:8 * G], hnorm_ref[...])
        cat_ref[:, 2 * G:3 * G] = yc.astype(BF16)

    def m_glu():
        ed_ref[CONF_PAD:CONF_PAD + TT, :] = (p_ref[:, 8 * G:9 * G]
                                             * _sigmoid(p_ref[:, 9 * G:10 * G]))
        ed = ed_ref[...]
        for r in range(1, SUBLANES):
            sh_ref[r - 1] = pltpu.roll(ed, TT + CONF_PAD - r, 0)

    def m_conf(rb):
        base = rb * CONV_ROWS
        first = CONF_PAD - (CONF_WIDTH - 1)
        acc = None
        for j in range(CONF_WIDTH):
            tiles, r = divmod(first + j, SUBLANES)
            lo = base + SUBLANES * tiles
            src = ed_ref[lo:lo + CONV_ROWS, :] if r == 0 else sh_ref[r - 1, lo:lo + CONV_ROWS, :]
            term = cdw_ref[j:j + 1, :] * src
            acc = term if acc is None else acc + term
        yd = _conf_tail(acc, cb_ref[...], lng_ref[...], lnb_ref[...])
        cat_ref[base:base + CONV_ROWS, 3 * G:4 * G] = yd.astype(BF16)

    def m_conf_tail():
        oconf_ref[...] = ed_ref[TT + CONF_PAD - (CONF_WIDTH - 1):TT + CONF_PAD, :]
        ed_ref[0:CONF_PAD, :] = ed_ref[TT:TT + CONF_PAD, :]

    def m_out():
        mix_ref[...] = jnp.dot(cat_ref[...], wout_ref[...], preferred_element_type=F32)

    def m_out_norm():
        x1_ref[slot] = x_ref[...] + _rmsnorm(mix_ref[...], npost_ref[...])

    n_gu = D_FF // FFN_COLS
    gate_up = [functools.partial(f_gate_up, j) for j in range(n_gu)]
    proj = [functools.partial(m_proj, blk) for blk in range(N_BLOCKS)]
    down = [functools.partial(f_down, k) for k in range(D_MODEL // G)]
    conf_all = [functools.partial(m_conf, rb) for rb in range(TT // CONV_ROWS)]
    per = len(conf_all) // 4
    conf = [conf_all[k * per:(k + 1) * per] for k in range(4)]
    chunk = [functools.partial(m_hgrn_chunk, j) for j in range(n_chunks)]
    g = gate_up
    schedule = [
        down[0], f_norm, down[1], m_norm, down[2], cast_next, down[3],
        proj[8], proj[9], f_out,
        g[0], m_glu, proj[3],
        g[1], *conf[0], proj[0], proj[1], proj[2],
        g[2], *conf[1], proj[4], proj[5], proj[6],
        g[3], *conf[2], proj[7],
        g[4], *conf[3], m_conf_tail,
        g[5], m_conv, m_pool,
        g[6], m_hgrn_gates,
        g[7], m_hgrn_state,
        g[8], chunk[0], chunk[1],
        g[9], chunk[2], chunk[3],
        m_hgrn_out, m_out, g[10], m_out_norm,
    ]
    assert n_gu == 11 and n_chunks == 4 and len(conf_all) == 4 * per and len(down) == 4
    for piece in schedule:
        piece()

    @pl.when(hg["span"] >= FAST_DECAY_LIMIT)
    def _redo_exact():
        f = lower + (1.0 - lower) * _sigmoid(p_ref[:, 5 * G:6 * G])
        _hgrn_exact_attn(hb_ref, hq_ref, 1.0 - f, p_ref[:, 6 * G:7 * G], o_ref, 0, TT)
        yce = _head_norm_gate(o_ref[...] + oi_ref[...], p_ref[:, 7 * G:8 * G], hnorm_ref[...])
        cat_ref[:, 2 * G:3 * G] = yce.astype(BF16)
        mixe = jnp.dot(cat_ref[...], wout_ref[...], preferred_element_type=F32)
        x1_ref[slot] = x_ref[...] + _rmsnorm(mixe, npost_ref[...])

    @pl.when((t == nt - 1) & (i < n_tiles))
    def _state_out():
        s = st_ref[...].T
        for hh in range(HEADS):
            ohgrn_ref[hh] = s[hh * DK:(hh + 1) * DK, hh * DK:(hh + 1) * DK]


def _mix_sample_kernel(layer, NS, TS, start_pos,
                       x_ref, sconv_ref, spool_ref, shgrn_ref, sconf_ref,
                       npre_ref, win_ref, convw_ref, poolbd_ref, pscale_ref, lb_ref,
                       hnorm_ref, cdw_ref, cb_ref, lng_ref, lnb_ref, wout_ref, npost_ref,
                       y_ref, oconv_ref, opool_ref, ohgrn_ref, oconf_ref,
                       p_ref, u_ref, pool_ref, qT_ref, fT_ref, kT_ref, vT_ref, oT_ref, cat_ref,
                       hm_ref):
    i = pl.program_id(0)

    @pl.when(i == 0)
    def _load_states():
        oconv_ref[...] = sconv_ref[...]
        opool_ref[...] = spool_ref[...]
        oconf_ref[...] = sconf_ref[...]
        ohgrn_ref[...] = shgrn_ref[...]

    def slab(t):
        return slice(t * NS, (t + 1) * NS)

    def proj(blk):
        cols = slice(blk * G, (blk + 1) * G)
        p_ref[:, cols] = jnp.dot(hm_ref[...], win_ref[:, cols], preferred_element_type=F32)

    def conv_in(j):
        if j < SC_WIDTH - 1:
            return oconv_ref[j]
        rows = slab(j - (SC_WIDTH - 1))
        return p_ref[rows, G:2 * G] * p_ref[rows, 2 * G:3 * G]

    def m_conv():
        for t in range(TS):
            ya = (convw_ref[0:1, :] * conv_in(t) + convw_ref[1:2, :] * conv_in(t + 1)
                  + convw_ref[2:3, :] * conv_in(t + 2))
            cat_ref[slab(t), 0:G] = (p_ref[slab(t), 0:G] * ya).astype(BF16)
        for j in range(SC_WIDTH - 1):
            oconv_ref[j] = conv_in(j + TS)

    def pool_in(j):
        if j < POOL_BUF:
            return opool_ref[j]
        return p_ref[slab(j - POOL_BUF), 3 * G:4 * G]

    def m_pool():
        for t in range(TS):
            idx = POOL_BUF + t
            run = pool_in(idx)
            sums = {}
            for j in range(1, POOL_BUF + 1):
                run = run + pool_in(idx - j)
                if j + 1 in POOL_WINDOWS:
                    sums[j + 1] = run
            pos = jnp.full((NS, G), start_pos + i * TS + t, jnp.int32)
            mean = _pool_select(sums, pos)
            pool_ref[slab(t), :] = mean - pool_in(idx)
        yb = jnp.dot(pool_ref[...].astype(BF16), poolbd_ref[...], preferred_element_type=F32)
        cat_ref[:, G:2 * G] = (yb * pscale_ref[...]).astype(BF16)
        for j in range(POOL_BUF):
            opool_ref[j] = pool_in(j + TS)

    def m_hgrn_gates():
        lower = _lower_bound(lb_ref[...], layer)
        q = _silu(p_ref[:, 4 * G:5 * G])
        f = lower + (1.0 - lower) * _sigmoid(p_ref[:, 5 * G:6 * G])
        for t in range(TS):
            qT_ref[t] = q[slab(t)].T
            fT_ref[t] = jnp.maximum(f[slab(t)], F_MIN).T
            kT_ref[t] = (1.0 - f[slab(t)]).T
            vT_ref[t] = p_ref[slab(t), 6 * G:7 * G].T

    def m_hgrn_scan():
        for hh in range(HEADS):
            head = slice(hh * DK, (hh + 1) * DK)
            vts = [vT_ref[t, head, :] for t in range(TS)]

            def body(k, accs, hh=hh, vts=vts):
                c = hh * DK + k
                s = ohgrn_ref[c]
                out = []
                for t in range(TS):
                    s = fT_ref[t, pl.ds(c, 1), :] * s + kT_ref[t, pl.ds(c, 1), :] * vts[t]
                    out.append(accs[t] + qT_ref[t, pl.ds(c, 1), :] * s)
                ohgrn_ref[c] = s
                return tuple(out)

            accs = lax.fori_loop(0, DK, body,
                                 tuple(jnp.zeros((DK, NS), F32) for _ in range(TS)), unroll=2)
            for t in range(TS):
                oT_ref[t, head, :] = accs[t]

    def m_hgrn_out():
        o = jnp.concatenate([oT_ref[t].T for t in range(TS)], axis=0)
        yc = _head_norm_gate(o, p_ref[:, 7 * G:8 * G], hnorm_ref[...])
        cat_ref[:, 2 * G:3 * G] = yc.astype(BF16)

    def m_glu():
        u_ref[...] = p_ref[:, 8 * G:9 * G] * _sigmoid(p_ref[:, 9 * G:10 * G])

    HALF = NS // 2

    def conf_in(j, rows):
        if j < CONF_WIDTH - 1:
            return oconf_ref[j, rows, :]
        base = (j - (CONF_WIDTH - 1)) * NS
        return u_ref[base + rows.start:base + rows.stop, :]

    def m_conf():
        for t in range(TS):
            for hf in range(2):
                rows = slice(hf * HALF, (hf + 1) * HALF)
                acc = None
                for j in range(CONF_WIDTH):
                    term = cdw_ref[j:j + 1, :] * conf_in(t + j, rows)
                    acc = term if acc is None else acc + term
                yd = _conf_tail(acc, cb_ref[...], lng_ref[...], lnb_ref[...])
                cat_ref[t * NS + hf * HALF:t * NS + (hf + 1) * HALF, 3 * G:4 * G] = (
                    yd.astype(BF16))
        for j in range(CONF_WIDTH - 1):
            oconf_ref[j] = conf_in(j + TS, slice(0, NS))

    hm_ref[...] = _rmsnorm(x_ref[...], npre_ref[...]).astype(BF16)
    for blk in range(N_BLOCKS):
        proj(blk)
    m_conv()
    m_pool()
    m_hgrn_gates()
    m_hgrn_scan()
    m_hgrn_out()
    m_glu()
    m_conf()
    mix =jnp.dot(cat_ref[...], wout_ref[...], preferred_element_type=F32)
    y_ref[...] = x_ref[...] + _rmsnorm(mix, npost_ref[...])


def _ffn_kernel(x_ref, npre_ref, wg_ref, wu_ref, wd_ref, npost_ref, y_ref):
    y_ref[...] = _ffn_block(x_ref[...], npre_ref[...], wg_ref[...], wu_ref[...], wd_ref[...],
                            npost_ref[...])


def _layer_spec(shape, layer, single_buffer=False):
    nd = len(shape)

    def imap(*_):
        return (layer,) + (0,) * nd

    if single_buffer:
        return pl.BlockSpec((None,) + tuple(shape), imap, pipeline_mode=pl.Buffered(1))
    return pl.BlockSpec((None,) + tuple(shape), imap)


def _mixer_weight_specs(layer):
    return [
        _layer_spec((1, D_MODEL), layer),
        _layer_spec((D_MODEL, D_IN), 0, True),
        _layer_spec((SC_WIDTH, G), layer),
        _layer_spec((G, G), layer),
        _layer_spec((1, G), layer),
        pl.BlockSpec((DEPTH, G), lambda *_: (0, 0)),
        _layer_spec((1, G), layer),
        _layer_spec((CONF_WIDTH, G), layer),
        _layer_spec((1, G), layer),
        _layer_spec((1, G), layer),
        _layer_spec((1, G), layer),
        _layer_spec((D_MODEL, D_MODEL), 0, True),
        _layer_spec((1, D_MODEL), layer),
    ]


def _ffn_weight_specs(layer):
    return [
        _layer_spec((1, D_MODEL), layer),
        _layer_spec((D_MODEL, D_FF), 0, True),
        _layer_spec((D_MODEL, D_FF), 0, True),
        _layer_spec((D_FF, D_MODEL), 0, True),
        _layer_spec((1, D_MODEL), layer),
    ]


def _layer_prompt(layer, x, mixer_wts, ffn_wts, next_f32):
    n, seq, _ = x.shape
    TT = PROMPT_TILE
    nt = seq // TT
    last = n * nt - 1
    n_steps = n * nt + 2

    def mix_tile(i):
        return jnp.minimum(i, last)

    def ffn_tile(i):
        return jnp.maximum(i - 2, 0)

    out_shape = [
        jax.ShapeDtypeStruct((n, seq, D_MODEL), F32),
        jax.ShapeDtypeStruct((n, SC_WIDTH - 1, G), F32),
        jax.ShapeDtypeStruct((n, POOL_BUF, G), F32),
        jax.ShapeDtypeStruct((n, HEADS, DK, DK), F32),
        jax.ShapeDtypeStruct((n, CONF_WIDTH - 1, G), F32),
    ]
    out_specs = [
        pl.BlockSpec((None, TT, D_MODEL), lambda i: (ffn_tile(i) // nt, ffn_tile(i) % nt, 0)),
        pl.BlockSpec((None, SC_WIDTH - 1, G), lambda i: (mix_tile(i) // nt, 0, 0)),
        pl.BlockSpec((None, POOL_BUF, G), lambda i: (mix_tile(i) // nt, 0, 0)),
        pl.BlockSpec((None, HEADS, DK, DK), lambda i: (mix_tile(i) // nt, 0, 0, 0)),
        pl.BlockSpec((None, CONF_WIDTH - 1, G), lambda i: (mix_tile(i) // nt, 0, 0)),
    ]
    cast_in_specs = []
    for w in next_f32:
        _, rows, cols = w.shape
        blk = next(b for b in range(CAST_ROWS, rows + 1, CAST_ROWS)
                   if rows % b == 0 and rows // b <= n_steps)
        n_blk = rows // blk

        def in_map(i, n_blk=n_blk):
            return (layer + 1, jnp.minimum(i, n_blk - 1), 0)

        def out_map(i, n_blk=n_blk):
            return (0, jnp.minimum(i, n_blk - 1), 0)

        cast_in_specs.append(pl.BlockSpec((None, blk, cols), in_map))
        out_specs.append(pl.BlockSpec((None, blk, cols), out_map))
        out_shape.append(jax.ShapeDtypeStruct((1, rows, cols), BF16))
    scratch = [
        pltpu.VMEM((TT, D_IN), F32),
        pltpu.VMEM((CONV_PAD + TT, G), F32),
        pltpu.VMEM((POOL_PAD + TT, G), F32),
        pltpu.VMEM((CONF_PAD + TT, G), F32),
        pltpu.VMEM((SUBLANES - 1, CONF_PAD + TT, G), F32),
        pltpu.VMEM((G, G), F32),
        pltpu.VMEM((TT, G), F32),
        pltpu.VMEM((TT, G), F32),
        pltpu.VMEM((TT, G), F32),
        pltpu.VMEM((TT, G), F32),
        pltpu.VMEM((TT, G), F32),
        pltpu.VMEM((TT, D_MODEL), BF16),
        pltpu.VMEM((3, TT, D_MODEL), F32),
        pltpu.VMEM((TT, D_MODEL), BF16),
        pltpu.VMEM((TT, D_MODEL), BF16),
        pltpu.VMEM((2, TT, D_FF), BF16),
        pltpu.VMEM((TT, D_MODEL), F32),
        pltpu.VMEM((TT, D_MODEL), F32),
    ]
    return pl.pallas_call(
        functools.partial(_layer_prompt_kernel, layer, TT, nt, n * nt, len(next_f32)),
        grid=(n_steps,),
        in_specs=[pl.BlockSpec((None, TT, D_MODEL),
                               lambda i: (mix_tile(i) // nt, mix_tile(i) % nt, 0))]
        + _mixer_weight_specs(layer) + _ffn_weight_specs(layer) + cast_in_specs,
        out_specs=tuple(out_specs),
        out_shape=tuple(out_shape),
        scratch_shapes=scratch,
        compiler_params=pltpu.CompilerParams(
            dimension_semantics=("arbitrary",), vmem_limit_bytes=VMEM_LIMIT),
        name=f"layer_prompt_l{layer}",
    )(x, *mixer_wts, *ffn_wts, *next_f32)


def _mix_sample(layer, x2d, s_conv, s_pool, s_hgrn, s_conf, wts, n_seq):
    m = x2d.shape[0]
    NS = n_seq
    TS = SAMPLE_STEPS
    M = TS * NS

    def state_spec(shape):
        nd = len(shape)
        return pl.BlockSpec((None,) + tuple(shape), lambda i: (layer,) + (0,) * nd,
                            pipeline_mode=pl.Buffered(1))

    def out_spec(shape):
        nd = len(shape)
        return pl.BlockSpec(tuple(shape), lambda i: (0,) * nd, pipeline_mode=pl.Buffered(1))

    state_shapes = [(SC_WIDTH - 1, NS, G), (POOL_BUF, NS, G), (G, DK, NS), (CONF_WIDTH - 1, NS, G)]
    in_specs = ([pl.BlockSpec((M, D_MODEL), lambda i: (i, 0))]
                + [state_spec(s) for s in state_shapes] + _mixer_weight_specs(layer))
    out_specs = tuple([pl.BlockSpec((M, D_MODEL), lambda i: (i, 0))]
                      + [out_spec(s) for s in state_shapes])
    out_shape = tuple([jax.ShapeDtypeStruct((m, D_MODEL), F32)]
                      + [jax.ShapeDtypeStruct(s, F32) for s in state_shapes])
    scratch = [
        pltpu.VMEM((M, D_IN), F32),
        pltpu.VMEM((M, G), F32),
        pltpu.VMEM((M, G), F32),
        pltpu.VMEM((TS, G, NS), F32),
        pltpu.VMEM((TS, G, NS), F32),
        pltpu.VMEM((TS, G, NS), F32),
        pltpu.VMEM((TS, G, NS), F32),
        pltpu.VMEM((TS, G, NS), F32),
        pltpu.VMEM((M, D_MODEL), BF16),
        pltpu.VMEM((M, D_MODEL), BF16),
    ]
    return pl.pallas_call(
        functools.partial(_mix_sample_kernel, layer, NS, TS, PAST_LEN),
        grid=(m // M,),
        in_specs=in_specs,
        out_specs=out_specs,
        out_shape=out_shape,
        scratch_shapes=scratch,
        compiler_params=pltpu.CompilerParams(
            dimension_semantics=("arbitrary",), vmem_limit_bytes=VMEM_LIMIT),
        name=f"mix_sample_l{layer}",
    )(x2d, s_conv, s_pool, s_hgrn, s_conf, *wts)


def _ffn(layer, x2d, ffn_wts, tag):
    m = x2d.shape[0]
    TM = min(FFN_TILE, m)
    assert m % TM == 0
    return pl.pallas_call(
        _ffn_kernel,
        grid=(m // TM,),
        in_specs=[pl.BlockSpec((TM, D_MODEL), lambda i: (i, 0))] + _ffn_weight_specs(layer),
        out_specs=pl.BlockSpec((TM, D_MODEL), lambda i: (i, 0)),
        out_shape=jax.ShapeDtypeStruct((m, D_MODEL), F32),
        compiler_params=pltpu.CompilerParams(
            dimension_semantics=("arbitrary",), vmem_limit_bytes=VMEM_LIMIT),
        name=f"ffn_{tag}_l{layer}",
    )(x2d, *ffn_wts)


def kernel(x_prompt, x_sample, state_conv, state_pool, state_hgrn, state_conf, norm_mix_pre, norm_mix_post, w_in, conv_w, pool_w, pool_scale, hgrn_lb, hgrn_norm, conf_dw, conf_b, conf_ln_g, conf_ln_b, w_out, norm_ffn_pre, norm_ffn_post, w_gate, w_up, w_down):
    def row(a):
        return a.reshape(DEPTH, 1, a.shape[-1])

    eye = jnp.eye(G // POOL_CH, dtype=pool_w.dtype)
    pool_bd = (pool_w[:, :, :, None, :] * eye[None, :, None, :, None]).reshape(DEPTH, G, G)
    pool_bd = pool_bd.astype(BF16)
    big_f32 = (w_in, w_out, w_gate, w_up, w_down)
    big = tuple(w[0:1].astype(BF16) for w in big_f32)

    def mixer_weights(big):
        return (row(norm_mix_pre), big[0], conv_w, pool_bd, row(pool_scale), hgrn_lb,
                row(hgrn_norm), conf_dw, row(conf_b), row(conf_ln_g), row(conf_ln_b), big[1],
                row(norm_mix_post))

    def ffn_weights(big):
        return (row(norm_ffn_pre), big[2], big[3], big[4], row(norm_ffn_post))

    ns, ts, _ = x_sample.shape
    xs = x_sample.transpose(1, 0, 2).reshape(ts * ns, D_MODEL)
    sc_t = state_conv.transpose(1, 2, 0, 3)
    sp_t = state_pool.transpose(1, 2, 0, 3)
    sf_t = state_conf.transpose(1, 2, 0, 3)
    sh_t = state_hgrn.transpose(1, 2, 3, 4, 0).reshape(DEPTH, G, DK, ns)
    xp = x_prompt
    p_states, s_states = [], []
    for layer in range(DEPTH):
        mixer_wts, ffn_wts = mixer_weights(big), ffn_weights(big)
        next_f32 = big_f32 if layer + 1 < DEPTH else ()
        xp, *rest = _layer_prompt(layer, xp, mixer_wts, ffn_wts, next_f32)
        p_states.append(rest[:4])
        big = tuple(rest[4:])
        xs, *sts = _mix_sample(layer, xs, sc_t, sp_t, sh_t, sf_t, mixer_wts, ns)
        s_states.append(sts)
        xs = _ffn(layer, xs, ffn_wts, "sample")
    xs = xs.reshape(ts, ns, D_MODEL).transpose(1, 0, 2)

    def stack(states, i):
        return jnp.stack([states[layer][i] for layer in range(DEPTH)], axis=1)

    def stack_t(i):
        return jnp.stack([s_states[layer][i] for layer in range(DEPTH)], axis=0).transpose(2, 0, 1, 3)

    hgrn_s = jnp.stack([s_states[layer][2] for layer in range(DEPTH)], axis=0)
    hgrn_s = hgrn_s.reshape(DEPTH, HEADS, DK, DK, ns).transpose(4, 0, 1, 2, 3)

    return (xp, xs,
            stack(p_states, 0), stack(p_states, 1), stack(p_states, 2), stack(p_states, 3),
            stack_t(0), stack_t(1), hgrn_s, stack_t(3))
```

```python
import functools

import jax
import jax.numpy as jnp
from jax import lax
from jax.experimental import pallas as pl
from jax.experimental.pallas import tpu as pltpu

F32 = jnp.float32
BF16 = jnp.bfloat16

D_MODEL = 1024
DEPTH = 2
PAST_LEN = 16384
G = 256
N_BLOCKS = 10
D_IN = N_BLOCKS * G
SC_WIDTH = 3
POOL_WINDOWS = (2, 4, 8, 16)
POOL_BUF = max(POOL_WINDOWS) - 1
POOL_CH = 64
HEADS = 4
DK = 64
CONF_WIDTH = 31
D_FF = 2816
EPS = 1e-6
F_MIN = 1e-20

SUBLANES = 8


def _round_up(n, m):
    return -(-n // m) * m


CONV_PAD = _round_up(SC_WIDTH - 1, SUBLANES)
POOL_PAD = _round_up(POOL_BUF, SUBLANES)
CONF_PAD = _round_up(CONF_WIDTH - 1, SUBLANES)

HGRN_CHUNK = 64
FAST_DECAY_LIMIT = 60.0
PROMPT_TILE = 256
SAMPLE_STEPS = 2
FFN_COLS = 256
N_FIRST_STEP_COPIES = 7
CAST_ROWS = 16
CONV_ROWS = 64
VMEM_LIMIT = 56 * 1024 * 1024


def _sigmoid(x):
    return jax.nn.sigmoid(x)


def _silu(x):
    return x * jax.nn.sigmoid(x)


def _rmsnorm(x, g):
    ms = jnp.mean(x * x, axis=-1, keepdims=True)
    return x * lax.rsqrt(ms + EPS) * g


def _head_block_mask(rows, cols, row_block, col_block):
    r = lax.broadcasted_iota(jnp.int32, (rows, cols), 0) // row_block
    c = lax.broadcasted_iota(jnp.int32, (rows, cols), 1) // col_block
    return r == c


def _cumsum_rows_mxu(x):
    n = x.shape[0]
    tri = (lax.broadcasted_iota(jnp.int32, (n, n), 0)
           >= lax.broadcasted_iota(jnp.int32, (n, n), 1))
    tri = jnp.where(tri, 1.0, 0.0).astype(BF16)
    hi = x.astype(BF16)
    r1 = x - hi.astype(F32)
    mid = r1.astype(BF16)
    lo = (r1 - mid.astype(F32)).astype(BF16)
    return (jnp.dot(tri, hi, preferred_element_type=F32)
            + jnp.dot(tri, mid, preferred_element_type=F32)
            + jnp.dot(tri, lo, preferred_element_type=F32))


def _lower_bound(lb_all, layer):
    m = jnp.max(lb_all, axis=0, keepdims=True)
    e = jnp.exp(lb_all - m)
    sm = e / jnp.sum(e, axis=0, keepdims=True)
    cs = sm[0:1]
    for i in range(1, layer + 1):
        cs = cs + sm[i:i + 1]
    return cs - sm[0:1]


def _hgrn_gates(zq, zf, zi, lower):
    q = _silu(zq)
    f = lower + (1.0 - lower) * _sigmoid(zf)
    logf = jnp.log(jnp.maximum(f, F_MIN))
    return q, 1.0 - f, zi, logf


def _hgrn_state_terms(q, kk, v, b, st_ref):
    TT = q.shape[0]
    b_end = b[TT - 1:TT, :]
    st = st_ref[...]
    qs = (q * jnp.exp(b)).astype(BF16)
    o_inter = lax.dot_general(qs, st.astype(BF16), (((1,), (1,)), ((), ())),
                              preferred_element_type=F32)
    kh = (kk * jnp.exp(b_end - b)).astype(BF16)
    upd = lax.dot_general(v.astype(BF16), kh, (((0,), (0,)), ((), ())),
                          preferred_element_type=F32)
    bd = _head_block_mask(G, G, DK, DK)
    st_ref[...] = st * jnp.exp(b_end) + jnp.where(bd, upd, 0.0)
    return o_inter


def _hgrn_refs(b, C):
    refs = []
    span = None
    for j in range(b.shape[0] // C):
        first = b[j * C:j * C + 1, :]
        last = b[(j + 1) * C - 1:(j + 1) * C, :]
        refs.append(0.5 * (first + last))
        half = jnp.max(0.5 * (first - last))
        span = half if span is None else jnp.maximum(span, half)
    return refs, span


def _hgrn_fast_chunk(q_tgt, b_tgt, kk_src, v_src, r, C):
    nt = q_tgt.shape[0]
    rows_mask = _head_block_mask(HEADS * C, G, C, DK)
    qz = (q_tgt * jnp.exp(b_tgt - r)).astype(BF16)
    ke = kk_src * jnp.exp(r - b_tgt[0:C])
    kebd = jnp.where(rows_mask, jnp.concatenate([ke] * HEADS, axis=0), 0.0).astype(BF16)
    attn = lax.dot_general(qz, kebd, (((1,), (1,)), ((), ())),
                           preferred_element_type=F32)
    t_idx = lax.broadcasted_iota(jnp.int32, (nt, HEADS * C), 0)
    s_idx = lax.broadcasted_iota(jnp.int32, (nt, HEADS * C), 1) % C
    attn = jnp.where(t_idx >= s_idx, attn, 0.0).astype(BF16)
    vbd = jnp.where(rows_mask, jnp.concatenate([v_src] * HEADS, axis=0), 0.0).astype(BF16)
    return jnp.dot(attn, vbd, preferred_element_type=F32)


def _hgrn_exact_attn(hb_ref, hq_ref, kk, v, o_ref, row0, TT):
    b = hb_ref[0:TT, :]
    ones_bd = jnp.where(_head_block_mask(G, G, DK, DK), 1.0, 0.0).astype(BF16)
    s_row = lax.broadcasted_iota(jnp.int32, (TT, G), 0)

    def body(t, carry):
        bt = hb_ref[pl.ds(t, 1), :]
        qt = hq_ref[pl.ds(t, 1), :]
        e = jnp.where(s_row <= t, qt * kk * jnp.exp(jnp.minimum(bt - b, 0.0)), 0.0)
        a = jnp.dot(e.astype(BF16), ones_bd, preferred_element_type=F32)
        o_ref[pl.ds(row0 + t, 1), :] = jnp.sum(a * v, axis=0, keepdims=True)
        return carry

    lax.fori_loop(0, TT, body, 0)


def _head_norm_gate(o, zg, hnorm):
    ones_bd = jnp.where(_head_block_mask(G, G, DK, DK), 1.0, 0.0).astype(BF16)
    o2 = o * o
    hi = o2.astype(BF16)
    lo = (o2 - hi.astype(F32)).astype(BF16)
    ssq = (jnp.dot(hi, ones_bd, preferred_element_type=F32)
           + jnp.dot(lo, ones_bd, preferred_element_type=F32))
    return o * lax.rsqrt(ssq * (1.0 / DK) + EPS) * hnorm * _silu(zg)


def _pool_select(sums, pos):
    shape = sums[POOL_WINDOWS[0]].shape
    grp = lax.broadcasted_iota(jnp.int32, shape, len(shape) - 1) // POOL_CH
    ssum = sums[POOL_WINDOWS[-1]]
    win = jnp.full(shape, POOL_WINDOWS[-1], jnp.int32)
    for gi in range(len(POOL_WINDOWS) - 2, -1, -1):
        ssum = jnp.where(grp == gi, sums[POOL_WINDOWS[gi]], ssum)
        win = jnp.where(grp == gi, POOL_WINDOWS[gi], win)
    cnt = jnp.minimum(pos + 1, win).astype(F32)
    return ssum / cnt


def _conf_tail(z, cb, lng, lnb):
    z = z + cb
    mu = jnp.mean(z, axis=-1, keepdims=True)
    zc = z - mu
    var = jnp.mean(zc * zc, axis=-1, keepdims=True)
    return _silu(zc * lax.rsqrt(var + EPS) * lng + lnb)


def _layer_prompt_kernel(layer, TT, nt, n_tiles, n_cast, *refs):
    (x_ref, npre_ref, win_ref, convw_ref, poolbd_ref, pscale_ref, lb_ref,
     hnorm_ref, cdw_ref, cb_ref, lng_ref, lnb_ref, wout_ref, npost_ref,
     fpre_ref, wg_ref, wu_ref, wd_ref, fpost_ref) = refs[:19]
    cast_in = refs[19:19 + n_cast]
    y_ref, oconv_ref, opool_ref, ohgrn_ref, oconf_ref = refs[19 + n_cast:24 + n_cast]
    cast_out = refs[24 + n_cast:24 + 2 * n_cast]
    (p_ref, ea_ref, eb_ref, ed_ref, sh_ref, st_ref, hb_ref, hq_ref, hk_ref, oi_ref,
     o_ref, cat_ref, x1_ref, hm_ref, hf_ref, a_ref, ff_ref, mix_ref) = refs[24 + 2 * n_cast:]
    i = pl.program_id(0)
    t = i % nt
    slot = i % 3

    @pl.when(i == 0)
    def _first():
        x1_ref[1] = jnp.zeros((TT, D_MODEL), F32)
        x1_ref[2] = jnp.zeros((TT, D_MODEL), F32)
        a_ref[0] = jnp.zeros((TT, D_FF), BF16)

    @pl.when((t == 0) & (i < n_tiles))
    def _new_sequence():
        ea_ref[0:CONV_PAD, :] = jnp.zeros((CONV_PAD, G), F32)
        eb_ref[0:POOL_PAD, :] = jnp.zeros((POOL_PAD, G), F32)
        ed_ref[0:CONF_PAD, :] = jnp.zeros((CONF_PAD, G), F32)
        st_ref[...] = jnp.zeros((G, G), F32)

    lower = _lower_bound(lb_ref[...], layer)
    n_chunks = TT // HGRN_CHUNK
    slot_up = (i + 2) % 3
    slot_down = (i + 1) % 3
    a_new = (i + 1) % 2
    a_old = i % 2

    def f_norm():
        hf_ref[...] = _rmsnorm(x1_ref[slot_up], fpre_ref[...]).astype(BF16)

    def f_gate_up(j):
        cols = slice(j * FFN_COLS, (j + 1) * FFN_COLS)
        hf = hf_ref[...]
        g = jnp.dot(hf, wg_ref[:, cols], preferred_element_type=F32)
        u = jnp.dot(hf, wu_ref[:, cols], preferred_element_type=F32)
        a_ref[a_new, :, cols] = (_silu(g) * u).astype(BF16)

    def f_down(k):
        cols = slice(k * G, (k + 1) * G)
        ff_ref[:, cols] = jnp.dot(a_ref[a_old], wd_ref[:, cols], preferred_element_type=F32)

    def f_out():
        y_ref[...] = x1_ref[slot_down] + _rmsnorm(ff_ref[...], fpost_ref[...])

    def cast_next():
        for src, dst in zip(cast_in, cast_out):
            dst[...] = src[...].astype(BF16)

    def m_norm():
        hm_ref[...] = _rmsnorm(x_ref[...], npre_ref[...]).astype(BF16)

    def m_proj(blk):
        cols = slice(blk * G, (blk + 1) * G)
        p_ref[:, cols] = jnp.dot(hm_ref[...], win_ref[:, cols], preferred_element_type=F32)

    def m_conv():
        cu = p_ref[:, G:2 * G] * p_ref[:, 2 * G:3 * G]
        ea_ref[CONV_PAD:CONV_PAD + TT, :] = cu
        ya = convw_ref[SC_WIDTH - 1:SC_WIDTH, :] * cu
        for back in range(1, SC_WIDTH):
            w = convw_ref[SC_WIDTH - 1 - back:SC_WIDTH - back, :]
            ya = ya + w * ea_ref[CONV_PAD - back:CONV_PAD - back + TT, :]
        cat_ref[:, 0:G] = (p_ref[:, 0:G] * ya).astype(BF16)
        oconv_ref[...] = ea_ref[TT + CONV_PAD - (SC_WIDTH - 1):TT + CONV_PAD, :]
        ea_ref[0:CONV_PAD, :] = ea_ref[TT:TT + CONV_PAD, :]

    def m_pool():
        pp = p_ref[:, 3 * G:4 * G]
        eb_ref[POOL_PAD:POOL_PAD + TT, :] = pp
        run = eb_ref[...]
        sums = {}
        w = 1
        while w < POOL_WINDOWS[-1]:
            run = run + pltpu.roll(run, w, 0)
            w *= 2
            sums[w] = run[POOL_PAD:]
        pos = t * TT + lax.broadcasted_iota(jnp.int32, (TT, G), 0)
        mean = _pool_select(sums, pos)
        yb = jnp.dot((mean - pp).astype(BF16), poolbd_ref[...], preferred_element_type=F32)
        cat_ref[:, G:2 * G] = (yb * pscale_ref[...]).astype(BF16)
        opool_ref[...] = eb_ref[TT + POOL_PAD - POOL_BUF:TT + POOL_PAD, :]
        eb_ref[0:POOL_PAD, :] = eb_ref[TT:TT + POOL_PAD, :]

    hg = {}

    def m_hgrn_gates():
        q, kk, _, logf = _hgrn_gates(p_ref[:, 4 * G:5 * G], p_ref[:, 5 * G:6 * G],
                                     p_ref[:, 6 * G:7 * G], lower)
        hq_ref[...] = q
        hk_ref[...] = kk
        hb_ref[...] = _cumsum_rows_mxu(logf)

    def m_hgrn_state():
        b = hb_ref[...]
        o_inter = _hgrn_state_terms(hq_ref[...], hk_ref[...], p_ref[:, 6 * G:7 * G], b, st_ref)
        oi_ref[...] = o_inter
        o_ref[...] = o_inter
        hg["refs"], hg["span"] = _hgrn_refs(b, HGRN_CHUNK)

    def m_hgrn_chunk(j):
        lo, hi = j * HGRN_CHUNK, (j + 1) * HGRN_CHUNK
        contrib = _hgrn_fast_chunk(hq_ref[lo:TT, :], hb_ref[lo:TT, :], hk_ref[lo:hi, :],
                                   p_ref[lo:hi, 6 * G:7 * G], hg["refs"][j], HGRN_CHUNK)
        o_ref[lo:TT, :] = o_ref[lo:TT, :] + contrib

    def m_hgrn_out():
        yc = _head_norm_gate(o_ref[...], p_ref[:, 7 * G:8 * G], hnorm_ref[...])
        cat_ref[:, 2 * G:3 * G] = yc.astype(BF16)

    def m_glu():
        ed_ref[CONF_PAD:CONF_PAD + TT, :] = (p_ref[:, 8 * G:9 * G]
                                             * _sigmoid(p_ref[:, 9 * G:10 * G]))
        ed = ed_ref[...]
        for r in range(1, SUBLANES):
            sh_ref[r - 1] = pltpu.roll(ed, TT + CONF_PAD - r, 0)

    def m_conf(rb):
        base = rb * CONV_ROWS
        first = CONF_PAD - (CONF_WIDTH - 1)
        acc = None
        for j in range(CONF_WIDTH):
            tiles, r = divmod(first + j, SUBLANES)
            lo = base + SUBLANES * tiles
            src = ed_ref[lo:lo + CONV_ROWS, :] if r == 0 else sh_ref[r - 1, lo:lo + CONV_ROWS, :]
            term = cdw_ref[j:j + 1, :] * src
            acc = term if acc is None else acc + term
        yd = _conf_tail(acc, cb_ref[...], lng_ref[...], lnb_ref[...])
        cat_ref[base:base + CONV_ROWS, 3 * G:4 * G] = yd.astype(BF16)

    def m_conf_tail():
        oconf_ref[...] = ed_ref[TT + CONF_PAD - (CONF_WIDTH - 1):TT + CONF_PAD, :]
        ed_ref[0:CONF_PAD, :] = ed_ref[TT:TT + CONF_PAD, :]

    def m_out():
        mix_ref[...] = jnp.dot(cat_ref[...], wout_ref[...], preferred_element_type=F32)

    def m_out_norm():
        x1_ref[slot] = x_ref[...] + _rmsnorm(mix_ref[...], npost_ref[...])

    n_gu = D_FF // FFN_COLS
    gate_up = [functools.partial(f_gate_up, j) for j in range(n_gu)]
    proj = [functools.partial(m_proj, blk) for blk in range(N_BLOCKS)]
    down = [functools.partial(f_down, k) for k in range(D_MODEL // G)]
    conf_all = [functools.partial(m_conf, rb) for rb in range(TT // CONV_ROWS)]
    per = len(conf_all) // 4
    conf = [conf_all[k * per:(k + 1) * per] for k in range(4)]
    chunk = [functools.partial(m_hgrn_chunk, j) for j in range(n_chunks)]
    g = gate_up
    schedule = [
        down[0], f_norm, down[1], m_norm, down[2], cast_next, down[3],
        proj[8], proj[9], f_out,
        g[0], m_glu, proj[3],
        g[1], *conf[0], proj[0], proj[1], proj[2],
        g[2], *conf[1], proj[4], proj[5], proj[6],
        g[3], *conf[2], proj[7],
        g[4], *conf[3], m_conf_tail,
        g[5], m_conv, m_pool,
        g[6], m_hgrn_gates,
        g[7], m_hgrn_state,
        g[8], chunk[0], chunk[1],
        g[9], chunk[2], chunk[3],
        m_hgrn_out, m_out, g[10], m_out_norm,
    ]
    assert n_gu == 11 and n_chunks == 4 and len(conf_all) == 4 * per and len(down) == 4
    for piece in schedule:
        piece()

    @pl.when(hg["span"] >= FAST_DECAY_LIMIT)
    def _redo_exact():
        f = lower + (1.0 - lower) * _sigmoid(p_ref[:, 5 * G:6 * G])
        _hgrn_exact_attn(hb_ref, hq_ref, 1.0 - f, p_ref[:, 6 * G:7 * G], o_ref, 0, TT)
        yce = _head_norm_gate(o_ref[...] + oi_ref[...], p_ref[:, 7 * G:8 * G], hnorm_ref[...])
        cat_ref[:, 2 * G:3 * G] = yce.astype(BF16)
        mixe = jnp.dot(cat_ref[...], wout_ref[...], preferred_element_type=F32)
        x1_ref[slot] = x_ref[...] + _rmsnorm(mixe, npost_ref[...])

    @pl.when((t == nt - 1) & (i < n_tiles))
    def _state_out():
        s = st_ref[...].T
        for hh in range(HEADS):
            ohgrn_ref[hh] = s[hh * DK:(hh + 1) * DK, hh * DK:(hh + 1) * DK]


def _layer_sample_kernel(layer, NS, TS, start_pos,
                         x_ref, sconv_hbm, spool_hbm, shgrn_hbm, sconf_hbm,
                         npre_ref, win_ref, convw_ref, poolbd_ref, pscale_ref, lb_ref,
                         hnorm_ref, cdw_ref, cb_ref, lng_ref, lnb_ref, wout_ref, npost_ref,
                         fpre_ref, wg_hbm, wu_hbm, wd_hbm, fpost_ref,
                         y_ref, oconv_ref, opool_ref, ohgrn_ref, oconf_ref,
                         p_ref, u_ref, pool_ref, qT_ref, fT_ref, kT_ref, vT_ref, oT_ref, cat_ref,
                         hm_ref, wg_ref, wu_ref, wd_ref, a_ref, ff_ref, sems):
    i = pl.program_id(0)
    copies = [
        (sconv_hbm.at[layer], oconv_ref), (spool_hbm.at[layer], opool_ref),
        (sconf_hbm.at[layer], oconf_ref), (shgrn_hbm.at[layer], ohgrn_ref),
        (wg_hbm.at[0], wg_ref), (wu_hbm.at[0], wu_ref), (wd_hbm.at[0], wd_ref),
    ]

    def copy(k):
        return pltpu.make_async_copy(copies[k][0], copies[k][1], sems.at[k])

    def wait_at_first_step(ks):
        @pl.when(i == 0)
        def _wait():
            for k in ks:
                copy(k).wait()

    @pl.when(i == 0)
    def _start_copies():
        for k in range(len(copies)):
            copy(k).start()

    wait_at_first_step([0, 1, 2])

    def slab(t):
        return slice(t * NS, (t + 1) * NS)

    def proj(blk):
        cols = slice(blk * G, (blk + 1) * G)
        p_ref[:, cols] = jnp.dot(hm_ref[...], win_ref[:, cols], preferred_element_type=F32)

    def conv_in(j):
        if j < SC_WIDTH - 1:
            return oconv_ref[j]
        rows = slab(j - (SC_WIDTH - 1))
        return p_ref[rows, G:2 * G] * p_ref[rows, 2 * G:3 * G]

    def m_conv():
        for t in range(TS):
            ya = (convw_ref[0:1, :] * conv_in(t) + convw_ref[1:2, :] * conv_in(t + 1)
                  + convw_ref[2:3, :] * conv_in(t + 2))
            cat_ref[slab(t), 0:G] = (p_ref[slab(t), 0:G] * ya).astype(BF16)
        for j in range(SC_WIDTH - 1):
            oconv_ref[j] = conv_in(j + TS)

    def pool_in(j):
        if j < POOL_BUF:
            return opool_ref[j]
        return p_ref[slab(j - POOL_BUF), 3 * G:4 * G]

    def m_pool():
        for t in range(TS):
            idx = POOL_BUF + t
            run = pool_in(idx)
            sums = {}
            for j in range(1, POOL_BUF + 1):
                run = run + pool_in(idx - j)
                if j + 1 in POOL_WINDOWS:
                    sums[j + 1] = run
            pos = jnp.full((NS, G), start_pos + i * TS + t, jnp.int32)
            mean = _pool_select(sums, pos)
            pool_ref[slab(t), :] = mean - pool_in(idx)
        yb = jnp.dot(pool_ref[...].astype(BF16), poolbd_ref[...], preferred_element_type=F32)
        cat_ref[:, G:2 * G] = (yb * pscale_ref[...]).astype(BF16)
        for j in range(POOL_BUF):
            opool_ref[j] = pool_in(j + TS)

    def m_hgrn_gates():
        lower = _lower_bound(lb_ref[...], layer)
        q = _silu(p_ref[:, 4 * G:5 * G])
        f = lower + (1.0 - lower) * _sigmoid(p_ref[:, 5 * G:6 * G])
        for t in range(TS):
            qT_ref[t] = q[slab(t)].T
            fT_ref[t] = jnp.maximum(f[slab(t)], F_MIN).T
            kT_ref[t] = (1.0 - f[slab(t)]).T
            vT_ref[t] = p_ref[slab(t), 6 * G:7 * G].T

    def m_hgrn_scan():
        wait_at_first_step([3])
        for hh in range(HEADS):
            head = slice(hh * DK, (hh + 1) * DK)
            vts = [vT_ref[t, head, :] for t in range(TS)]

            def body(k, accs, hh=hh, vts=vts):
                c = hh * DK + k
                s = ohgrn_ref[c]
                out = []
                for t in range(TS):
                    s = fT_ref[t, pl.ds(c, 1), :] * s + kT_ref[t, pl.ds(c, 1), :] * vts[t]
                    out.append(accs[t] + qT_ref[t, pl.ds(c, 1), :] * s)
                ohgrn_ref[c] = s
                return tuple(out)

            accs = lax.fori_loop(0, DK, body,
                                 tuple(jnp.zeros((DK, NS), F32) for _ in range(TS)), unroll=2)
            for t in range(TS):
                oT_ref[t, head, :] = accs[t]

    def m_hgrn_out():
        o = jnp.concatenate([oT_ref[t].T for t in range(TS)], axis=0)
        yc = _head_norm_gate(o, p_ref[:, 7 * G:8 * G], hnorm_ref[...])
        cat_ref[:, 2 * G:3 * G] = yc.astype(BF16)

    def m_glu():
        u_ref[...] = p_ref[:, 8 * G:9 * G] * _sigmoid(p_ref[:, 9 * G:10 * G])

    HALF = NS // 2

    def conf_in(j, rows):
        if j < CONF_WIDTH - 1:
            return oconf_ref[j, rows, :]
        base = (j - (CONF_WIDTH - 1)) * NS
        return u_ref[base + rows.start:base + rows.stop, :]

    def m_conf():
        for t in range(TS):
            for hf in range(2):
                rows = slice(hf * HALF, (hf + 1) * HALF)
                acc = None
                for j in range(CONF_WIDTH):
                    term = cdw_ref[j:j + 1, :] * conf_in(t + j, rows)
                    acc = term if acc is None else acc + term
                yd = _conf_tail(acc, cb_ref[...], lng_ref[...], lnb_ref[...])
                cat_ref[t * NS + hf * HALF:t * NS + (hf + 1) * HALF, 3 * G:4 * G] = (
                    yd.astype(BF16))
        for j in range(CONF_WIDTH - 1):
            oconf_ref[j] = conf_in(j + TS, slice(0, NS))

    hm_ref[...] = _rmsnorm(x_ref[...], npre_ref[...]).astype(BF16)
    for blk in range(N_BLOCKS):
        proj(blk)
    m_conv()
    m_pool()
    m_hgrn_gates()
    m_hgrn_scan()
    m_hgrn_out()
    m_glu()
    m_conf()
    mix = jnp.dot(cat_ref[...], wout_ref[...], preferred_element_type=F32)
    y_ref[...] = x_ref[...] + _rmsnorm(mix, npost_ref[...])

    wait_at_first_step([4, 5, 6])
    hm_ref[...] = _rmsnorm(y_ref[...], fpre_ref[...]).astype(BF16)
    for j in range(D_FF // FFN_COLS):
        cols = slice(j * FFN_COLS, (j + 1) * FFN_COLS)
        hf = hm_ref[...]
        g = jnp.dot(hf, wg_ref[:, cols], preferred_element_type=F32)
        u = jnp.dot(hf, wu_ref[:, cols], preferred_element_type=F32)
        a_ref[:, cols] = (_silu(g) * u).astype(BF16)
    for k in range(D_MODEL // G):
        cols = slice(k * G, (k + 1) * G)
        ff_ref[:, cols] = jnp.dot(a_ref[...], wd_ref[:, cols], preferred_element_type=F32)
    y_ref[...] = y_ref[...] + _rmsnorm(ff_ref[...], fpost_ref[...])


def _layer_spec(shape, layer, single_buffer=False):
    nd = len(shape)

    def imap(*_):
        return (layer,) + (0,) * nd

    if single_buffer:
        return pl.BlockSpec((None,) + tuple(shape), imap, pipeline_mode=pl.Buffered(1))
    return pl.BlockSpec((None,) + tuple(shape), imap)


def _mixer_weight_specs(layer):
    return [
        _layer_spec((1, D_MODEL), layer),
        _layer_spec((D_MODEL, D_IN), 0, True),
        _layer_spec((SC_WIDTH, G), layer),
        _layer_spec((G, G), layer),
        _layer_spec((1, G), layer),
        pl.BlockSpec((DEPTH, G), lambda *_: (0, 0)),
        _layer_spec((1, G), layer),
        _layer_spec((CONF_WIDTH, G), layer),
        _layer_spec((1, G), layer),
        _layer_spec((1, G), layer),
        _layer_spec((1, G), layer),
        _layer_spec((D_MODEL, D_MODEL), 0, True),
        _layer_spec((1, D_MODEL), layer),
    ]


def _ffn_weight_specs(layer):
    return [
        _layer_spec((1, D_MODEL), layer),
        _layer_spec((D_MODEL, D_FF), 0, True),
        _layer_spec((D_MODEL, D_FF), 0, True),
        _layer_spec((D_FF, D_MODEL), 0, True),
        _layer_spec((1, D_MODEL), layer),
    ]


def _layer_prompt(layer, x, mixer_wts, ffn_wts, next_f32):
    n, seq, _ = x.shape
    TT = PROMPT_TILE
    nt = seq // TT
    last = n * nt - 1
    n_steps = n * nt + 2

    def mix_tile(i):
        return jnp.minimum(i, last)

    def ffn_tile(i):
        return jnp.maximum(i - 2, 0)

    out_shape = [
        jax.ShapeDtypeStruct((n, seq, D_MODEL), F32),
        jax.ShapeDtypeStruct((n, SC_WIDTH - 1, G), F32),
        jax.ShapeDtypeStruct((n, POOL_BUF, G), F32),
        jax.ShapeDtypeStruct((n, HEADS, DK, DK), F32),
        jax.ShapeDtypeStruct((n, CONF_WIDTH - 1, G), F32),
    ]
    out_specs = [
        pl.BlockSpec((None, TT, D_MODEL), lambda i: (ffn_tile(i) // nt, ffn_tile(i) % nt, 0)),
        pl.BlockSpec((None, SC_WIDTH - 1, G), lambda i: (mix_tile(i) // nt, 0, 0)),
        pl.BlockSpec((None, POOL_BUF, G), lambda i: (mix_tile(i) // nt, 0, 0)),
        pl.BlockSpec((None, HEADS, DK, DK), lambda i: (mix_tile(i) // nt, 0, 0, 0)),
        pl.BlockSpec((None, CONF_WIDTH - 1, G), lambda i: (mix_tile(i) // nt, 0, 0)),
    ]
    cast_in_specs = []
    for w in next_f32:
        _, rows, cols = w.shape
        blk = next(b for b in range(CAST_ROWS, rows + 1, CAST_ROWS)
                   if rows % b == 0 and rows // b <= n_steps)
        n_blk = rows // blk

        def in_map(i, n_blk=n_blk):
            return (layer + 1, jnp.minimum(i, n_blk - 1), 0)

        def out_map(i, n_blk=n_blk):
            return (0, jnp.minimum(i, n_blk - 1), 0)

        cast_in_specs.append(pl.BlockSpec((None, blk, cols), in_map))
        out_specs.append(pl.BlockSpec((None, blk, cols), out_map))
        out_shape.append(jax.ShapeDtypeStruct((1, rows, cols), BF16))
    scratch = [
        pltpu.VMEM((TT, D_IN), F32),
        pltpu.VMEM((CONV_PAD + TT, G), F32),
        pltpu.VMEM((POOL_PAD + TT, G), F32),
        pltpu.VMEM((CONF_PAD + TT, G), F32),
        pltpu.VMEM((SUBLANES - 1, CONF_PAD + TT, G), F32),
        pltpu.VMEM((G, G), F32),
        pltpu.VMEM((TT, G), F32),
        pltpu.VMEM((TT, G), F32),
        pltpu.VMEM((TT, G), F32),
        pltpu.VMEM((TT, G), F32),
        pltpu.VMEM((TT, G), F32),
        pltpu.VMEM((TT, D_MODEL), BF16),
        pltpu.VMEM((3, TT, D_MODEL), F32),
        pltpu.VMEM((TT, D_MODEL), BF16),
        pltpu.VMEM((TT, D_MODEL), BF16),
        pltpu.VMEM((2, TT, D_FF), BF16),
        pltpu.VMEM((TT, D_MODEL), F32),
        pltpu.VMEM((TT, D_MODEL), F32),
    ]
    return pl.pallas_call(
        functools.partial(_layer_prompt_kernel, layer, TT, nt, n * nt, len(next_f32)),
        grid=(n_steps,),
        in_specs=[pl.BlockSpec((None, TT, D_MODEL),
                               lambda i: (mix_tile(i) // nt, mix_tile(i) % nt, 0))]
        + _mixer_weight_specs(layer) + _ffn_weight_specs(layer) + cast_in_specs,
        out_specs=tuple(out_specs),
        out_shape=tuple(out_shape),
        scratch_shapes=scratch,
        compiler_params=pltpu.CompilerParams(
            dimension_semantics=("arbitrary",), vmem_limit_bytes=VMEM_LIMIT),
        name=f"layer_prompt_l{layer}",
    )(x, *mixer_wts, *ffn_wts, *next_f32)


def _layer_sample(layer, x2d, s_conv, s_pool, s_hgrn, s_conf, mixer_wts, ffn_wts, n_seq):
    m = x2d.shape[0]
    NS = n_seq
    TS = SAMPLE_STEPS
    M = TS * NS

    def out_spec(shape):
        nd = len(shape)
        return pl.BlockSpec(tuple(shape), lambda i: (0,) * nd, pipeline_mode=pl.Buffered(1))

    in_hbm = pl.BlockSpec(memory_space=pl.ANY)
    state_shapes = [(SC_WIDTH - 1, NS, G), (POOL_BUF, NS, G), (G, DK, NS), (CONF_WIDTH - 1, NS, G)]
    in_specs = ([pl.BlockSpec((M, D_MODEL), lambda i: (i, 0))]
                + [in_hbm] * len(state_shapes) + _mixer_weight_specs(layer)
                + [_layer_spec((1, D_MODEL), layer), in_hbm, in_hbm, in_hbm,
                   _layer_spec((1, D_MODEL), layer)])
    out_specs = tuple([pl.BlockSpec((M, D_MODEL), lambda i: (i, 0))]
                      + [out_spec(s) for s in state_shapes])
    out_shape = tuple([jax.ShapeDtypeStruct((m, D_MODEL), F32)]
                      + [jax.ShapeDtypeStruct(s, F32) for s in state_shapes])
    scratch = [
        pltpu.VMEM((M, D_IN), F32),
        pltpu.VMEM((M, G), F32),
        pltpu.VMEM((M, G), F32),
        pltpu.VMEM((TS, G, NS), F32),
        pltpu.VMEM((TS, G, NS), F32),
        pltpu.VMEM((TS, G, NS), F32),
        pltpu.VMEM((TS, G, NS), F32),
        pltpu.VMEM((TS, G, NS), F32),
        pltpu.VMEM((M, D_MODEL), BF16),
        pltpu.VMEM((M, D_MODEL), BF16),
        pltpu.VMEM((D_MODEL, D_FF), BF16),
        pltpu.VMEM((D_MODEL, D_FF), BF16),
        pltpu.VMEM((D_FF, D_MODEL), BF16),
        pltpu.VMEM((M, D_FF), BF16),
        pltpu.VMEM((M, D_MODEL), F32),
        pltpu.SemaphoreType.DMA((N_FIRST_STEP_COPIES,)),
    ]
    return pl.pallas_call(
        functools.partial(_layer_sample_kernel, layer, NS, TS, PAST_LEN),
        grid=(m // M,),
        in_specs=in_specs,
        out_specs=out_specs,
        out_shape=out_shape,
        scratch_shapes=scratch,
        compiler_params=pltpu.CompilerParams(
            dimension_semantics=("arbitrary",), vmem_limit_bytes=VMEM_LIMIT),
        name=f"layer_sample_l{layer}",
    )(x2d, s_conv, s_pool, s_hgrn, s_conf, *mixer_wts, *ffn_wts)


def kernel(x_prompt, x_sample, state_conv, state_pool, state_hgrn, state_conf, norm_mix_pre, norm_mix_post, w_in, conv_w, pool_w, pool_scale, hgrn_lb, hgrn_norm, conf_dw, conf_b, conf_ln_g, conf_ln_b, w_out, norm_ffn_pre, norm_ffn_post, w_gate, w_up, w_down):
    def row(a):
        return a.reshape(DEPTH, 1, a.shape[-1])

    eye = jnp.eye(G // POOL_CH, dtype=pool_w.dtype)
    pool_bd = (pool_w[:, :, :, None, :] * eye[None, :, None, :, None]).reshape(DEPTH, G, G)
    pool_bd = pool_bd.astype(BF16)
    big_f32 = (w_in, w_out, w_gate, w_up, w_down)
    big = tuple(w[0:1].astype(BF16) for w in big_f32)

    def mixer_weights(big):
        return (row(norm_mix_pre), big[0], conv_w, pool_bd, row(pool_scale), hgrn_lb,
                row(hgrn_norm), conf_dw, row(conf_b), row(conf_ln_g), row(conf_ln_b), big[1],
                row(norm_mix_post))

    def ffn_weights(big):
        return (row(norm_ffn_pre), big[2], big[3], big[4], row(norm_ffn_post))

    ns, ts, _ = x_sample.shape
    xs = x_sample.transpose(1, 0, 2).reshape(ts * ns, D_MODEL)
    sc_t = state_conv.transpose(1, 2, 0, 3)
    sp_t = state_pool.transpose(1, 2, 0, 3)
    sf_t = state_conf.transpose(1, 2, 0, 3)
    sh_t = state_hgrn.transpose(1, 2, 3, 4, 0).reshape(DEPTH, G, DK, ns)
    xp = x_prompt
    p_states, s_states = [], []
    for layer in range(DEPTH):
        mixer_wts, ffn_wts = mixer_weights(big), ffn_weights(big)
        next_f32 = big_f32 if layer + 1 < DEPTH else ()
        xp, *rest = _layer_prompt(layer, xp, mixer_wts, ffn_wts, next_f32)
        p_states.append(rest[:4])
        big = tuple(rest[4:])
        xs, *sts = _layer_sample(layer, xs, sc_t, sp_t, sh_t, sf_t, mixer_wts, ffn_wts, ns)
        s_states.append(sts)
    xs = xs.reshape(ts, ns, D_MODEL).transpose(1, 0, 2)

    def stack(states, i):
        return jnp.stack([states[layer][i] for layer in range(DEPTH)], axis=1)

    def stack_t(i):
        return jnp.stack([s_states[layer][i] for layer in range(DEPTH)], axis=0).transpose(2, 0, 1, 3)

    hgrn_s = jnp.stack([s_states[layer][2] for layer in range(DEPTH)], axis=0)
    hgrn_s = hgrn_s.reshape(DEPTH, HEADS, DK, DK, ns).transpose(4, 0, 1, 2, 3)

    return (xp, xs,
            stack(p_states, 0), stack(p_states, 1), stack(p_states, 2), stack(p_states, 3),
            stack_t(0), stack_t(1), hgrn_s, stack_t(3))
```

```python
import functools

import jax
import jax.numpy as jnp
from jax import lax
from jax.experimental import pallas as pl
from jax.experimental.pallas import tpu as pltpu

F32 = jnp.float32
BF16 = jnp.bfloat16

D_MODEL = 1024
DEPTH = 2
PAST_LEN = 16384
G = 256
N_BLOCKS = 10
D_IN = N_BLOCKS * G
SC_WIDTH = 3
POOL_WINDOWS = (2, 4, 8, 16)
POOL_BUF = max(POOL_WINDOWS) - 1
POOL_CH = 64
HEADS = 4
DK = 64
CONF_WIDTH = 31
D_FF = 2816
EPS = 1e-6
F_MIN = 1e-20

SUBLANES = 8


def _round_up(n, m):
    return -(-n // m) * m


CONV_PAD = _round_up(SC_WIDTH - 1, SUBLANES)
POOL_PAD = _round_up(POOL_BUF, SUBLANES)
CONF_PAD = _round_up(CONF_WIDTH - 1, SUBLANES)

HGRN_CHUNK = 64
FAST_DECAY_LIMIT = 60.0
PROMPT_TILE = 256
SAMPLE_STEPS = 2
FFN_COLS = 256
N_FIRST_STEP_COPIES = 7
CAST_ROWS = 16
CONV_ROWS = 64
VMEM_LIMIT = 56 * 1024 * 1024


def _sigmoid(x):
    return jax.nn.sigmoid(x)


def _silu(x):
    return x * jax.nn.sigmoid(x)


def _rmsnorm(x, g):
    ms = jnp.mean(x * x, axis=-1, keepdims=True)
    return x * lax.rsqrt(ms + EPS) * g


def _head_block_mask(rows, cols, row_block, col_block):
    r = lax.broadcasted_iota(jnp.int32, (rows, cols), 0) // row_block
    c = lax.broadcasted_iota(jnp.int32, (rows, cols), 1) // col_block
    return r == c


def _cumsum_rows_mxu(x):
    n = x.shape[0]
    tri = (lax.broadcasted_iota(jnp.int32, (n, n), 0)
           >= lax.broadcasted_iota(jnp.int32, (n, n), 1))
    tri = jnp.where(tri, 1.0, 0.0).astype(BF16)
    hi = x.astype(BF16)
    r1 = x - hi.astype(F32)
    mid = r1.astype(BF16)
    lo = (r1 - mid.astype(F32)).astype(BF16)
    return (jnp.dot(tri, hi, preferred_element_type=F32)
            + jnp.dot(tri, mid, preferred_element_type=F32)
            + jnp.dot(tri, lo, preferred_element_type=F32))


def _lower_bound(lb_all, layer):
    m = jnp.max(lb_all, axis=0, keepdims=True)
    e = jnp.exp(lb_all - m)
    sm = e / jnp.sum(e, axis=0, keepdims=True)
    cs = sm[0:1]
    for i in range(1, layer + 1):
        cs = cs + sm[i:i + 1]
    return cs - sm[0:1]


def _hgrn_gates(zq, zf, zi, lower):
    q = _silu(zq)
    f = lower + (1.0 - lower) * _sigmoid(zf)
    logf = jnp.log(jnp.maximum(f, F_MIN))
    return q, 1.0 - f, zi, logf


def _hgrn_state_terms(q, kk, v, b, st_ref):
    TT = q.shape[0]
    b_end = b[TT - 1:TT, :]
    st = st_ref[...]
    qs = (q * jnp.exp(b)).astype(BF16)
    o_inter = lax.dot_general(qs, st.astype(BF16), (((1,), (1,)), ((), ())),
                              preferred_element_type=F32)
    kh = (kk * jnp.exp(b_end - b)).astype(BF16)
    upd = lax.dot_general(v.astype(BF16), kh, (((0,), (0,)), ((), ())),
                          preferred_element_type=F32)
    bd = _head_block_mask(G, G, DK, DK)
    st_ref[...] = st * jnp.exp(b_end) + jnp.where(bd, upd, 0.0)
    return o_inter


def _hgrn_refs(b, C):
    refs = []
    span = None
    for j in range(b.shape[0] // C):
        first = b[j * C:j * C + 1, :]
        last = b[(j + 1) * C - 1:(j + 1) * C, :]
        refs.append(0.5 * (first + last))
        half = jnp.max(0.5 * (first - last))
        span = half if span is None else jnp.maximum(span, half)
    return refs, span


def _hgrn_fast_chunk(q_tgt, b_tgt, kk_src, v_src, r, C):
    nt = q_tgt.shape[0]
    rows_mask = _head_block_mask(HEADS * C, G, C, DK)
    qz = (q_tgt * jnp.exp(b_tgt - r)).astype(BF16)
    ke = kk_src * jnp.exp(r - b_tgt[0:C])
    kebd = jnp.where(rows_mask, jnp.concatenate([ke] * HEADS, axis=0), 0.0).astype(BF16)
    attn = lax.dot_general(qz, kebd, (((1,), (1,)), ((), ())),
                           preferred_element_type=F32)
    t_idx = lax.broadcasted_iota(jnp.int32, (nt, HEADS * C), 0)
    s_idx = lax.broadcasted_iota(jnp.int32, (nt, HEADS * C), 1) % C
    attn = jnp.where(t_idx >= s_idx, attn, 0.0).astype(BF16)
    vbd = jnp.where(rows_mask, jnp.concatenate([v_src] * HEADS, axis=0), 0.0).astype(BF16)
    return jnp.dot(attn, vbd, preferred_element_type=F32)


def _hgrn_exact_attn(hb_ref, hq_ref, kk, v, o_ref, row0, TT):
    b = hb_ref[0:TT, :]
    ones_bd = jnp.where(_head_block_mask(G, G, DK, DK), 1.0, 0.0).astype(BF16)
    s_row = lax.broadcasted_iota(jnp.int32, (TT, G), 0)

    def body(t, carry):
        bt = hb_ref[pl.ds(t, 1), :]
        qt = hq_ref[pl.ds(t, 1), :]
        e = jnp.where(s_row <= t, qt * kk * jnp.exp(jnp.minimum(bt - b, 0.0)), 0.0)
        a = jnp.dot(e.astype(BF16), ones_bd, preferred_element_type=F32)
        o_ref[pl.ds(row0 + t, 1), :] = jnp.sum(a * v, axis=0, keepdims=True)
        return carry

    lax.fori_loop(0, TT, body, 0)


def _head_norm_gate(o, zg, hnorm):
    ones_bd = jnp.where(_head_block_mask(G, G, DK, DK), 1.0, 0.0).astype(BF16)
    o2 = o * o
    hi = o2.astype(BF16)
    lo = (o2 - hi.astype(F32)).astype(BF16)
    ssq = (jnp.dot(hi, ones_bd, preferred_element_type=F32)
           + jnp.dot(lo, ones_bd, preferred_element_type=F32))
    return o * lax.rsqrt(ssq * (1.0 / DK) + EPS) * hnorm * _silu(zg)


def _pool_select(sums, pos):
    shape = sums[POOL_WINDOWS[0]].shape
    grp = lax.broadcasted_iota(jnp.int32, shape, len(shape) - 1) // POOL_CH
    ssum = sums[POOL_WINDOWS[-1]]
    win = jnp.full(shape, POOL_WINDOWS[-1], jnp.int32)
    for gi in range(len(POOL_WINDOWS) - 2, -1, -1):
        ssum = jnp.where(grp == gi, sums[POOL_WINDOWS[gi]], ssum)
        win = jnp.where(grp == gi, POOL_WINDOWS[gi], win)
    cnt = jnp.minimum(pos + 1, win).astype(F32)
    return ssum / cnt


def _conf_tail(z, cb, lng, lnb):
    z = z + cb
    mu = jnp.mean(z, axis=-1, keepdims=True)
    zc = z - mu
    var = jnp.mean(zc * zc, axis=-1, keepdims=True)
    return _silu(zc * lax.rsqrt(var + EPS) * lng + lnb)


def _layer_prompt_kernel(layer, TT, nt, n_tiles, n_cast, *refs):
    (x_ref, npre_ref, win_ref, convw_ref, poolbd_ref, pscale_ref, lb_ref,
     hnorm_ref, cdw_ref, cb_ref, lng_ref, lnb_ref, wout_ref, npost_ref,
     fpre_ref, wg_ref, wu_ref, wd_ref, fpost_ref) = refs[:19]
    cast_in = refs[19:19 + n_cast]
    y_ref, oconv_ref, opool_ref, ohgrn_ref, oconf_ref = refs[19 + n_cast:24 + n_cast]
    cast_out = refs[24 + n_cast:24 + 2 * n_cast]
    (p_ref, ea_ref, eb_ref, ed_ref, sh_ref, st_ref, hb_ref, hq_ref, hk_ref, oi_ref,
     o_ref, cat_ref, x1_ref, hm_ref, hf_ref, a_ref, ff_ref, mix_ref) = refs[24 + 2 * n_cast:]
    i = pl.program_id(0)
    t = i % nt
    slot = i % 3

    @pl.when(i == 0)
    def _first():
        x1_ref[1] = jnp.zeros((TT, D_MODEL), F32)
        x1_ref[2] = jnp.zeros((TT, D_MODEL), F32)
        a_ref[0] = jnp.zeros((TT, D_FF), BF16)

    @pl.when((t == 0) & (i < n_tiles))
    def _new_sequence():
        ea_ref[0:CONV_PAD, :] = jnp.zeros((CONV_PAD, G), F32)
        eb_ref[0:POOL_PAD, :] = jnp.zeros((POOL_PAD, G), F32)
        ed_ref[0:CONF_PAD, :] = jnp.zeros((CONF_PAD, G), F32)
        st_ref[...] = jnp.zeros((G, G), F32)

    lower = _lower_bound(lb_ref[...], layer)
    n_chunks = TT // HGRN_CHUNK
    slot_up = (i + 2) % 3
    slot_down = (i + 1) % 3
    a_new = (i + 1) % 2
    a_old = i % 2

    def f_norm():
        hf_ref[...] = _rmsnorm(x1_ref[slot_up], fpre_ref[...]).astype(BF16)

    def f_gate_up(j):
        cols = slice(j * FFN_COLS, (j + 1) * FFN_COLS)
        hf = hf_ref[...]
        g = jnp.dot(hf, wg_ref[:, cols], preferred_element_type=F32)
        u = jnp.dot(hf, wu_ref[:, cols], preferred_element_type=F32)
        a_ref[a_new, :, cols] = (_silu(g) * u).astype(BF16)

    def f_down(k):
        cols = slice(k * G, (k + 1) * G)
        ff_ref[:, cols] = jnp.dot(a_ref[a_old], wd_ref[:, cols], preferred_element_type=F32)

    def f_out():
        y_ref[...] = x1_ref[slot_down] + _rmsnorm(ff_ref[...], fpost_ref[...])

    def cast_next():
        for src, dst in zip(cast_in, cast_out):
            dst[...] = src[...].astype(BF16)

    def m_norm():
        hm_ref[...] = _rmsnorm(x_ref[...], npre_ref[...]).astype(BF16)

    def m_proj(blk):
        cols = slice(blk * G, (blk + 1) * G)
        p_ref[:, cols] = jnp.dot(hm_ref[...], win_ref[:, cols], preferred_element_type=F32)

    def m_conv():
        cu = p_ref[:, G:2 * G] * p_ref[:, 2 * G:3 * G]
        ea_ref[CONV_PAD:CONV_PAD + TT, :] = cu
        ya = convw_ref[SC_WIDTH - 1:SC_WIDTH, :] * cu
        for back in range(1, SC_WIDTH):
            w = convw_ref[SC_WIDTH - 1 - back:SC_WIDTH - back, :]
            ya = ya + w * ea_ref[CONV_PAD - back:CONV_PAD - back + TT, :]
        cat_ref[:, 0:G] = (p_ref[:, 0:G] * ya).astype(BF16)
        oconv_ref[...] = ea_ref[TT + CONV_PAD - (SC_WIDTH - 1):TT + CONV_PAD, :]
        ea_ref[0:CONV_PAD, :] = ea_ref[TT:TT + CONV_PAD, :]

    def m_pool():
        pp = p_ref[:, 3 * G:4 * G]
        eb_ref[POOL_PAD:POOL_PAD + TT, :] = pp
        run = eb_ref[...]
        sums = {}
        w = 1
        while w < POOL_WINDOWS[-1]:
            run = run + pltpu.roll(run, w, 0)
            w *= 2
            sums[w] = run[POOL_PAD:]
        pos = t * TT + lax.broadcasted_iota(jnp.int32, (TT, G), 0)
        mean = _pool_select(sums, pos)
        yb = jnp.dot((mean - pp).astype(BF16), poolbd_ref[...], preferred_element_type=F32)
        cat_ref[:, G:2 * G] = (yb * pscale_ref[...]).astype(BF16)
        opool_ref[...] = eb_ref[TT + POOL_PAD - POOL_BUF:TT + POOL_PAD, :]
        eb_ref[0:POOL_PAD, :] = eb_ref[TT:TT + POOL_PAD, :]

    hg = {}

    def m_hgrn_gates():
        q, kk, _, logf = _hgrn_gates(p_ref[:, 4 * G:5 * G], p_ref[:, 5 * G:6 * G],
                                     p_ref[:, 6 * G:7 * G], lower)
        hq_ref[...] = q
        hk_ref[...] = kk
        hb_ref[...] = _cumsum_rows_mxu(logf)

    def m_hgrn_state():
        b = hb_ref[...]
        o_inter = _hgrn_state_terms(hq_ref[...], hk_ref[...], p_ref[:, 6 * G:7 * G], b, st_ref)
        oi_ref[...] = o_inter
        o_ref[...] = o_inter
        hg["refs"], hg["span"] = _hgrn_refs(b, HGRN_CHUNK)

    def m_hgrn_chunk(j):
        lo, hi = j * HGRN_CHUNK, (j + 1) * HGRN_CHUNK
        contrib = _hgrn_fast_chunk(hq_ref[lo:TT, :], hb_ref[lo:TT, :], hk_ref[lo:hi, :],
                                   p_ref[lo:hi, 6 * G:7 * G], hg["refs"][j], HGRN_CHUNK)
        o_ref[lo:TT, :] = o_ref[lo:TT, :] + contrib

    def m_hgrn_out():
        yc = _head_norm_gate(o_ref[...], p_ref[:, 7 * G:8 * G], hnorm_ref[...])
        cat_ref[:, 2 * G:3 * G] = yc.astype(BF16)

    def m_glu():
        ed_ref[CONF_PAD:CONF_PAD + TT, :] = (p_ref[:, 8 * G:9 * G]
                                             * _sigmoid(p_ref[:, 9 * G:10 * G]))
        ed = ed_ref[...]
        for r in range(1, SUBLANES):
            sh_ref[r - 1] = pltpu.roll(ed, TT + CONF_PAD - r, 0)

    def m_conf(rb):
        base = rb * CONV_ROWS
        first = CONF_PAD - (CONF_WIDTH - 1)
        acc = None
        for j in range(CONF_WIDTH):
            tiles, r = divmod(first + j, SUBLANES)
            lo = base + SUBLANES * tiles
            src = ed_ref[lo:lo + CONV_ROWS, :] if r == 0 else sh_ref[r - 1, lo:lo + CONV_ROWS, :]
            term = cdw_ref[j:j + 1, :] * src
            acc = term if acc is None else acc + term
        yd = _conf_tail(acc, cb_ref[...], lng_ref[...], lnb_ref[...])
        cat_ref[base:base + CONV_ROWS, 3 * G:4 * G] = yd.astype(BF16)

    def m_conf_tail():
        oconf_ref[...] = ed_ref[TT + CONF_PAD - (CONF_WIDTH - 1):TT + CONF_PAD, :]
        ed_ref[0:CONF_PAD, :] = ed_ref[TT:TT + CONF_PAD, :]

    def m_out():
        mix_ref[...] = jnp.dot(cat_ref[...], wout_ref[...], preferred_element_type=F32)

    def m_out_norm():
        x1_ref[slot] = x_ref[...] + _rmsnorm(mix_ref[...], npost_ref[...])

    n_gu = D_FF // FFN_COLS
    gate_up = [functools.partial(f_gate_up, j) for j in range(n_gu)]
    proj = [functools.partial(m_proj, blk) for blk in range(N_BLOCKS)]
    down = [functools.partial(f_down, k) for k in range(D_MODEL // G)]
    conf_all = [functools.partial(m_conf, rb) for rb in range(TT // CONV_ROWS)]
    per = len(conf_all) // 4
    conf = [conf_all[k * per:(k + 1) * per] for k in range(4)]
    chunk = [functools.partial(m_hgrn_chunk, j) for j in range(n_chunks)]
    g = gate_up
    schedule = [
        down[0], f_norm, down[1], m_norm, down[2], cast_next, down[3],
        proj[8], proj[9], f_out,
        g[0], m_glu, proj[3],
        g[1], *conf[0], proj[0], proj[1], proj[2],
        g[2], *conf[1], proj[4], proj[5], proj[6],
        g[3], *conf[2], proj[7],
        g[4], *conf[3], m_conf_tail,
        g[5], m_conv, m_pool,
        g[6], m_hgrn_gates,
        g[7], m_hgrn_state,
        g[8], chunk[0], chunk[1],
        g[9], chunk[2], chunk[3],
        m_hgrn_out, m_out, g[10], m_out_norm,
    ]
    assert n_gu == 11 and n_chunks == 4 and len(conf_all) == 4 * per and len(down) == 4
    for piece in schedule:
        piece()

    @pl.when(hg["span"] >= FAST_DECAY_LIMIT)
    def _redo_exact():
        f = lower + (1.0 - lower) * _sigmoid(p_ref[:, 5 * G:6 * G])
        _hgrn_exact_attn(hb_ref, hq_ref, 1.0 - f, p_ref[:, 6 * G:7 * G], o_ref, 0, TT)
        yce = _head_norm_gate(o_ref[...] + oi_ref[...], p_ref[:, 7 * G:8 * G], hnorm_ref[...])
        cat_ref[:, 2 * G:3 * G] = yce.astype(BF16)
        mixe = jnp.dot(cat_ref[...], wout_ref[...], preferred_element_type=F32)
        x1_ref[slot] = x_ref[...] + _rmsnorm(mixe, npost_ref[...])

    @pl.when((t == nt - 1) & (i < n_tiles))
    def _state_out():
        s = st_ref[...].T
        for hh in range(HEADS):
            ohgrn_ref[hh] = s[hh * DK:(hh + 1) * DK, hh * DK:(hh + 1) * DK]


def _layer_sample_kernel(layer, NS, TS, start_pos,
                         x_ref, sconv_hbm, spool_hbm, shgrn_hbm, sconf_hbm,
                         npre_ref, win_ref, convw_ref, poolbd_ref, pscale_ref, lb_ref,
                         hnorm_ref, cdw_ref, cb_ref, lng_ref, lnb_ref, wout_ref, npost_ref,
                         fpre_ref, wg_hbm, wu_hbm, wd_hbm, fpost_ref,
                         y_ref, oconv_ref, opool_ref, ohgrn_ref, oconf_ref,
                         p_ref, u_ref, pool_ref, qT_ref, fT_ref, kT_ref, vT_ref, oT_ref, cat_ref,
                         hm_ref, xt_ref, wg_ref, wu_ref, wd_ref, a_ref, ff_ref, sems):
    i = pl.program_id(0)
    copies = [
        (sconv_hbm.at[layer], oconv_ref), (spool_hbm.at[layer], opool_ref),
        (sconf_hbm.at[layer], oconf_ref), (shgrn_hbm.at[layer], ohgrn_ref),
        (wg_hbm.at[0], wg_ref), (wu_hbm.at[0], wu_ref), (wd_hbm.at[0], wd_ref),
    ]

    def copy(k):
        return pltpu.make_async_copy(copies[k][0], copies[k][1], sems.at[k])

    def wait_at_first_step(ks):
        @pl.when(i == 0)
        def _wait():
            for k in ks:
                copy(k).wait()

    @pl.when(i == 0)
    def _start_copies():
        for k in range(len(copies)):
            copy(k).start()

    wait_at_first_step([0, 1, 2])

    def slab(t):
        return slice(t * NS, (t + 1) * NS)

    def proj(blk):
        cols = slice(blk * G, (blk + 1) * G)
        p_ref[:, cols] = jnp.dot(hm_ref[...], win_ref[:, cols], preferred_element_type=F32)

    def conv_in(j):
        if j < SC_WIDTH - 1:
            return oconv_ref[j]
        rows = slab(j - (SC_WIDTH - 1))
        return p_ref[rows, G:2 * G] * p_ref[rows, 2 * G:3 * G]

    def m_conv():
        for t in range(TS):
            ya = (convw_ref[0:1, :] * conv_in(t) + convw_ref[1:2, :] * conv_in(t + 1)
                  + convw_ref[2:3, :] * conv_in(t + 2))
            cat_ref[slab(t), 0:G] = (p_ref[slab(t), 0:G] * ya).astype(BF16)
        for j in range(SC_WIDTH - 1):
            oconv_ref[j] = conv_in(j + TS)

    def pool_in(j):
        if j < POOL_BUF:
            return opool_ref[j]
        return p_ref[slab(j - POOL_BUF), 3 * G:4 * G]

    def m_pool():
        for t in range(TS):
            idx = POOL_BUF + t
            run = pool_in(idx)
            sums = {}
            for j in range(1, POOL_BUF + 1):
                run = run + pool_in(idx - j)
                if j + 1 in POOL_WINDOWS:
                    sums[j + 1] = run
            pos = jnp.full((NS, G), start_pos + i * TS + t, jnp.int32)
            mean = _pool_select(sums, pos)
            pool_ref[slab(t), :] = mean - pool_in(idx)
        yb = jnp.dot(pool_ref[...].astype(BF16), poolbd_ref[...], preferred_element_type=F32)
        cat_ref[:, G:2 * G] = (yb * pscale_ref[...]).astype(BF16)
        for j in range(POOL_BUF):
            opool_ref[j] = pool_in(j + TS)

    def m_hgrn_gates():
        lower = _lower_bound(lb_ref[...], layer)
        q = _silu(p_ref[:, 4 * G:5 * G])
        f = lower + (1.0 - lower) * _sigmoid(p_ref[:, 5 * G:6 * G])
        for t in range(TS):
            qT_ref[t] = q[slab(t)].T
            fT_ref[t] = jnp.maximum(f[slab(t)], F_MIN).T
            kT_ref[t] = (1.0 - f[slab(t)]).T
            vT_ref[t] = p_ref[slab(t), 6 * G:7 * G].T

    def m_hgrn_scan():
        wait_at_first_step([3])
        for hh in range(HEADS):
            head = slice(hh * DK, (hh + 1) * DK)
            vts = [vT_ref[t, head, :] for t in range(TS)]

            def body(k, accs, hh=hh, vts=vts):
                c = hh * DK + k
                s = ohgrn_ref[c]
                out = []
                for t in range(TS):
                    s = fT_ref[t, pl.ds(c, 1), :] * s + kT_ref[t, pl.ds(c, 1), :] * vts[t]
                    out.append(accs[t] + qT_ref[t, pl.ds(c, 1), :] * s)
                ohgrn_ref[c] = s
                return tuple(out)

            accs = lax.fori_loop(0, DK, body,
                                 tuple(jnp.zeros((DK, NS), F32) for _ in range(TS)), unroll=2)
            for t in range(TS):
                oT_ref[t, head, :] = accs[t]

    def m_hgrn_out():
        o = jnp.concatenate([oT_ref[t].T for t in range(TS)], axis=0)
        yc = _head_norm_gate(o, p_ref[:, 7 * G:8 * G], hnorm_ref[...])
        cat_ref[:, 2 * G:3 * G] = yc.astype(BF16)

    def m_glu():
        u_ref[...] = p_ref[:, 8 * G:9 * G] * _sigmoid(p_ref[:, 9 * G:10 * G])

    HALF = NS // 2

    def conf_in(j, rows):
        if j < CONF_WIDTH - 1:
            return oconf_ref[j, rows, :]
        base = (j - (CONF_WIDTH - 1)) * NS
        return u_ref[base + rows.start:base + rows.stop, :]

    def m_conf():
        for t in range(TS):
            for hf in range(2):
                rows = slice(hf * HALF, (hf + 1) * HALF)
                acc = None
                for j in range(CONF_WIDTH):
                    term = cdw_ref[j:j + 1, :] * conf_in(t + j, rows)
                    acc = term if acc is None else acc + term
                yd = _conf_tail(acc, cb_ref[...], lng_ref[...], lnb_ref[...])
                cat_ref[t * NS + hf * HALF:t * NS + (hf + 1) * HALF, 3 * G:4 * G] = (
                    yd.astype(BF16))
        for j in range(CONF_WIDTH - 1):
            oconf_ref[j] = conf_in(j + TS, slice(0, NS))

    for t in range(TS):
        xt_ref[slab(t), :] = x_ref[:, pl.ds(i * TS + t, 1), :].reshape(NS, D_MODEL)
    hm_ref[...] = _rmsnorm(xt_ref[...], npre_ref[...]).astype(BF16)
    for blk in range(N_BLOCKS):
        proj(blk)
    m_conv()
    m_pool()
    m_hgrn_gates()
    m_hgrn_scan()
    m_hgrn_out()
    m_glu()
    m_conf()
    mix = jnp.dot(cat_ref[...], wout_ref[...], preferred_element_type=F32)
    xt_ref[...] = xt_ref[...] + _rmsnorm(mix, npost_ref[...])

    wait_at_first_step([4, 5, 6])
    hm_ref[...] = _rmsnorm(xt_ref[...], fpre_ref[...]).astype(BF16)
    for j in range(D_FF // FFN_COLS):
        cols = slice(j * FFN_COLS, (j + 1) * FFN_COLS)
        hf = hm_ref[...]
        g = jnp.dot(hf, wg_ref[:, cols], preferred_element_type=F32)
        u = jnp.dot(hf, wu_ref[:, cols], preferred_element_type=F32)
        a_ref[:, cols] = (_silu(g) * u).astype(BF16)
    for k in range(D_MODEL // G):
        cols = slice(k * G, (k + 1) * G)
        ff_ref[:, cols] = jnp.dot(a_ref[...], wd_ref[:, cols], preferred_element_type=F32)
    y = xt_ref[...] + _rmsnorm(ff_ref[...], fpost_ref[...])
    for t in range(TS):
        y_ref[:, pl.ds(i * TS + t, 1), :] = y[slab(t)].reshape(NS, 1, D_MODEL)


def _layer_spec(shape, layer, single_buffer=False):
    nd = len(shape)

    def imap(*_):
        return (layer,) + (0,) * nd

    if single_buffer:
        return pl.BlockSpec((None,) + tuple(shape), imap, pipeline_mode=pl.Buffered(1))
    return pl.BlockSpec((None,) + tuple(shape), imap)


def _mixer_weight_specs(layer):
    return [
        _layer_spec((1, D_MODEL), layer),
        _layer_spec((D_MODEL, D_IN), 0, True),
        _layer_spec((SC_WIDTH, G), layer),
        _layer_spec((G, G), layer),
        _layer_spec((1, G), layer),
        pl.BlockSpec((DEPTH, G), lambda *_: (0, 0)),
        _layer_spec((1, G), layer),
        _layer_spec((CONF_WIDTH, G), layer),
        _layer_spec((1, G), layer),
        _layer_spec((1, G), layer),
        _layer_spec((1, G), layer),
        _layer_spec((D_MODEL, D_MODEL), 0, True),
        _layer_spec((1, D_MODEL), layer),
    ]


def _ffn_weight_specs(layer):
    return [
        _layer_spec((1, D_MODEL), layer),
        _layer_spec((D_MODEL, D_FF), 0, True),
        _layer_spec((D_MODEL, D_FF), 0, True),
        _layer_spec((D_FF, D_MODEL), 0, True),
        _layer_spec((1, D_MODEL), layer),
    ]


def _layer_prompt(layer, x, mixer_wts, ffn_wts, next_f32):
    n, seq, _ = x.shape
    TT = PROMPT_TILE
    nt = seq // TT
    last = n * nt - 1
    n_steps = n * nt + 2

    def mix_tile(i):
        return jnp.minimum(i, last)

    def ffn_tile(i):
        return jnp.maximum(i - 2, 0)

    out_shape = [
        jax.ShapeDtypeStruct((n, seq, D_MODEL), F32),
        jax.ShapeDtypeStruct((n, SC_WIDTH - 1, G), F32),
        jax.ShapeDtypeStruct((n, POOL_BUF, G), F32),
        jax.ShapeDtypeStruct((n, HEADS, DK, DK), F32),
        jax.ShapeDtypeStruct((n, CONF_WIDTH - 1, G), F32),
    ]
    out_specs = [
        pl.BlockSpec((None, TT, D_MODEL), lambda i: (ffn_tile(i) // nt, ffn_tile(i) % nt, 0)),
        pl.BlockSpec((None, SC_WIDTH - 1, G), lambda i: (mix_tile(i) // nt, 0, 0)),
        pl.BlockSpec((None, POOL_BUF, G), lambda i: (mix_tile(i) // nt, 0, 0)),
        pl.BlockSpec((None, HEADS, DK, DK), lambda i: (mix_tile(i) // nt, 0, 0, 0)),
        pl.BlockSpec((None, CONF_WIDTH - 1, G), lambda i: (mix_tile(i) // nt, 0, 0)),
    ]
    cast_in_specs = []
    for w in next_f32:
        _, rows, cols = w.shape
        blk = next(b for b in range(CAST_ROWS, rows + 1, CAST_ROWS)
                   if rows % b == 0 and rows // b <= n_steps)
        n_blk = rows // blk

        def in_map(i, n_blk=n_blk):
            return (layer + 1, jnp.minimum(i, n_blk - 1), 0)

        def out_map(i, n_blk=n_blk):
            return (0, jnp.minimum(i, n_blk - 1), 0)

        cast_in_specs.append(pl.BlockSpec((None, blk, cols), in_map))
        out_specs.append(pl.BlockSpec((None, blk, cols), out_map))
        out_shape.append(jax.ShapeDtypeStruct((1, rows, cols), BF16))
    scratch = [
        pltpu.VMEM((TT, D_IN), F32),
        pltpu.VMEM((CONV_PAD + TT, G), F32),
        pltpu.VMEM((POOL_PAD + TT, G), F32),
        pltpu.VMEM((CONF_PAD + TT, G), F32),
        pltpu.VMEM((SUBLANES - 1, CONF_PAD + TT, G), F32),
        pltpu.VMEM((G, G), F32),
        pltpu.VMEM((TT, G), F32),
        pltpu.VMEM((TT, G), F32),
        pltpu.VMEM((TT, G), F32),
        pltpu.VMEM((TT, G), F32),
        pltpu.VMEM((TT, G), F32),
        pltpu.VMEM((TT, D_MODEL), BF16),
        pltpu.VMEM((3, TT, D_MODEL), F32),
        pltpu.VMEM((TT, D_MODEL), BF16),
        pltpu.VMEM((TT, D_MODEL), BF16),
        pltpu.VMEM((2, TT, D_FF), BF16),
        pltpu.VMEM((TT, D_MODEL), F32),
        pltpu.VMEM((TT, D_MODEL), F32),
    ]
    return pl.pallas_call(
        functools.partial(_layer_prompt_kernel, layer, TT, nt, n * nt, len(next_f32)),
        grid=(n_steps,),
        in_specs=[pl.BlockSpec((None, TT, D_MODEL),
                               lambda i: (mix_tile(i) // nt, mix_tile(i) % nt, 0))]
        + _mixer_weight_specs(layer) + _ffn_weight_specs(layer) + cast_in_specs,
        out_specs=tuple(out_specs),
        out_shape=tuple(out_shape),
        scratch_shapes=scratch,
        compiler_params=pltpu.CompilerParams(
            dimension_semantics=("arbitrary",), vmem_limit_bytes=VMEM_LIMIT),
        name=f"layer_prompt_l{layer}",
    )(x, *mixer_wts, *ffn_wts, *next_f32)


def _layer_sample(layer, x, s_conv, s_pool, s_hgrn, s_conf, mixer_wts, ffn_wts):
    NS, ts, _ = x.shape
    TS = SAMPLE_STEPS
    M = TS * NS

    def out_spec(shape):
        nd = len(shape)
        return pl.BlockSpec(tuple(shape), lambda i: (0,) * nd, pipeline_mode=pl.Buffered(1))

    in_hbm = pl.BlockSpec(memory_space=pl.ANY)
    state_shapes = [(SC_WIDTH - 1, NS, G), (POOL_BUF, NS, G), (G, DK, NS), (CONF_WIDTH - 1, NS, G)]
    in_specs = ([out_spec(x.shape)]
                + [in_hbm] * len(state_shapes) + _mixer_weight_specs(layer)
                + [_layer_spec((1, D_MODEL), layer), in_hbm, in_hbm, in_hbm,
                   _layer_spec((1, D_MODEL), layer)])
    out_specs = tuple([out_spec(x.shape)] + [out_spec(s) for s in state_shapes])
    out_shape = tuple([jax.ShapeDtypeStruct(x.shape, F32)]
                      + [jax.ShapeDtypeStruct(s, F32) for s in state_shapes])
    scratch = [
        pltpu.VMEM((M, D_IN), F32),
        pltpu.VMEM((M, G), F32),
        pltpu.VMEM((M, G), F32),
        pltpu.VMEM((TS, G, NS), F32),
        pltpu.VMEM((TS, G, NS), F32),
        pltpu.VMEM((TS, G, NS), F32),
        pltpu.VMEM((TS, G, NS), F32),
        pltpu.VMEM((TS, G, NS), F32),
        pltpu.VMEM((M, D_MODEL), BF16),
        pltpu.VMEM((M, D_MODEL), BF16),
        pltpu.VMEM((M, D_MODEL), F32),
        pltpu.VMEM((D_MODEL, D_FF), BF16),
        pltpu.VMEM((D_MODEL, D_FF), BF16),
        pltpu.VMEM((D_FF, D_MODEL), BF16),
        pltpu.VMEM((M, D_FF), BF16),
        pltpu.VMEM((M, D_MODEL), F32),
        pltpu.SemaphoreType.DMA((N_FIRST_STEP_COPIES,)),
    ]
    return pl.pallas_call(
        functools.partial(_layer_sample_kernel, layer, NS, TS, PAST_LEN),
        grid=(ts // TS,),
        in_specs=in_specs,
        out_specs=out_specs,
        out_shape=out_shape,
        scratch_shapes=scratch,
        compiler_params=pltpu.CompilerParams(
            dimension_semantics=("arbitrary",), vmem_limit_bytes=VMEM_LIMIT),
        name=f"layer_sample_l{layer}",
    )(x, s_conv, s_pool, s_hgrn, s_conf, *mixer_wts, *ffn_wts)


def kernel(x_prompt, x_sample, state_conv, state_pool, state_hgrn, state_conf, norm_mix_pre, norm_mix_post, w_in, conv_w, pool_w, pool_scale, hgrn_lb, hgrn_norm, conf_dw, conf_b, conf_ln_g, conf_ln_b, w_out, norm_ffn_pre, norm_ffn_post, w_gate, w_up, w_down):
    def row(a):
        return a.reshape(DEPTH, 1, a.shape[-1])

    eye = jnp.eye(G // POOL_CH, dtype=pool_w.dtype)
    pool_bd = (pool_w[:, :, :, None, :] * eye[None, :, None, :, None]).reshape(DEPTH, G, G)
    pool_bd = pool_bd.astype(BF16)
    big_f32 = (w_in, w_out, w_gate, w_up, w_down)
    big = tuple(w[0:1].astype(BF16) for w in big_f32)

    def mixer_weights(big):
        return (row(norm_mix_pre), big[0], conv_w, pool_bd, row(pool_scale), hgrn_lb,
                row(hgrn_norm), conf_dw, row(conf_b), row(conf_ln_g), row(conf_ln_b), big[1],
                row(norm_mix_post))

    def ffn_weights(big):
        return (row(norm_ffn_pre), big[2], big[3], big[4], row(norm_ffn_post))

    ns = x_sample.shape[0]
    xs = x_sample
    sc_t = state_conv.transpose(1, 2, 0, 3)
    sp_t = state_pool.transpose(1, 2, 0, 3)
    sf_t = state_conf.transpose(1, 2, 0, 3)
    sh_t = state_hgrn.transpose(1, 2, 3, 4, 0).reshape(DEPTH, G, DK, ns)
    xp = x_prompt
    p_states, s_states = [], []
    for layer in range(DEPTH):
        mixer_wts, ffn_wts = mixer_weights(big), ffn_weights(big)
        next_f32 = big_f32 if layer + 1 < DEPTH else ()
        xp, *rest = _layer_prompt(layer, xp, mixer_wts, ffn_wts, next_f32)
        p_states.append(rest[:4])
        big = tuple(rest[4:])
        xs, *sts = _layer_sample(layer, xs, sc_t, sp_t, sh_t, sf_t, mixer_wts, ffn_wts)
        s_states.append(sts)

    def stack(states, i):
        return jnp.stack([states[layer][i] for layer in range(DEPTH)], axis=1)

    def stack_t(i):
        return jnp.stack([s_states[layer][i] for layer in range(DEPTH)], axis=0).transpose(2, 0, 1, 3)

    hgrn_s = jnp.stack([s_states[layer][2] for layer in range(DEPTH)], axis=0)
    hgrn_s = hgrn_s.reshape(DEPTH, HEADS, DK, DK, ns).transpose(4, 0, 1, 2, 3)

    return (xp, xs,
            stack(p_states, 0), stack(p_states, 1), stack(p_states, 2), stack(p_states, 3),
            stack_t(0), stack_t(1), hgrn_s, stack_t(3))
```

```python
import functools

import jax
import jax.numpy as jnp
from jax import lax
from jax.experimental import pallas as pl
from jax.experimental.pallas import tpu as pltpu

F32 = jnp.float32
BF16 = jnp.bfloat16

D_MODEL = 1024
DEPTH = 2
PAST_LEN = 16384
G = 256
N_BLOCKS = 10
D_IN = N_BLOCKS * G
SC_WIDTH = 3
POOL_WINDOWS = (2, 4, 8, 16)
POOL_BUF = max(POOL_WINDOWS) - 1
POOL_CH = 64
HEADS = 4
DK = 64
CONF_WIDTH = 31
D_FF = 2816
EPS = 1e-6
F_MIN = 1e-20

SUBLANES = 8


def _round_up(n, m):
    return -(-n // m) * m


CONV_PAD = _round_up(SC_WIDTH - 1, SUBLANES)
POOL_PAD = _round_up(POOL_BUF, SUBLANES)
CONF_PAD = _round_up(CONF_WIDTH - 1, SUBLANES)

HGRN_CHUNK = 64
FAST_DECAY_LIMIT = 60.0
PROMPT_TILE = 256
SAMPLE_STEPS = 2
FFN_COLS = 256
N_FIRST_STEP_COPIES = 7
CAST_ROWS = 16
CONV_ROWS = 64
VMEM_LIMIT = 56 * 1024 * 1024


def _sigmoid(x):
    return jax.nn.sigmoid(x)


def _silu(x):
    return x * jax.nn.sigmoid(x)


def _rmsnorm(x, g):
    ms = jnp.mean(x * x, axis=-1, keepdims=True)
    return x * lax.rsqrt(ms + EPS) * g


def _head_block_mask(rows, cols, row_block, col_block):
    r = lax.broadcasted_iota(jnp.int32, (rows, cols), 0) // row_block
    c = lax.broadcasted_iota(jnp.int32, (rows, cols), 1) // col_block
    return r == c


def _cumsum_rows_mxu(x):
    n = x.shape[0]
    tri = (lax.broadcasted_iota(jnp.int32, (n, n), 0)
           >= lax.broadcasted_iota(jnp.int32, (n, n), 1))
    tri = jnp.where(tri, 1.0, 0.0).astype(BF16)
    hi = x.astype(BF16)
    r1 = x - hi.astype(F32)
    mid = r1.astype(BF16)
    lo = (r1 - mid.astype(F32)).astype(BF16)
    return (jnp.dot(tri, hi, preferred_element_type=F32)
            + jnp.dot(tri, mid, preferred_element_type=F32)
            + jnp.dot(tri, lo, preferred_element_type=F32))


def _lower_bound(lb_all, layer):
    m = jnp.max(lb_all, axis=0, keepdims=True)
    e = jnp.exp(lb_all - m)
    sm = e / jnp.sum(e, axis=0, keepdims=True)
    cs = sm[0:1]
    for i in range(1, layer + 1):
        cs = cs + sm[i:i + 1]
    return cs - sm[0:1]


def _hgrn_gates(zq, zf, zi, lower):
    q = _silu(zq)
    f = lower + (1.0 - lower) * _sigmoid(zf)
    logf = jnp.log(jnp.maximum(f, F_MIN))
    return q, 1.0 - f, zi, logf


def _hgrn_state_terms(q, kk, v, b, st_ref):
    TT = q.shape[0]
    b_end = b[TT - 1:TT, :]
    st = st_ref[...]
    qs = (q * jnp.exp(b)).astype(BF16)
    o_inter = lax.dot_general(qs, st.astype(BF16), (((1,), (1,)), ((), ())),
                              preferred_element_type=F32)
    kh = (kk * jnp.exp(b_end - b)).astype(BF16)
    upd = lax.dot_general(v.astype(BF16), kh, (((0,), (0,)), ((), ())),
                          preferred_element_type=F32)
    bd = _head_block_mask(G, G, DK, DK)
    st_ref[...] = st * jnp.exp(b_end) + jnp.where(bd, upd, 0.0)
    return o_inter


def _hgrn_refs(b, C):
    refs = []
    span = None
    for j in range(b.shape[0] // C):
        first = b[j * C:j * C + 1, :]
        last = b[(j + 1) * C - 1:(j + 1) * C, :]
        refs.append(0.5 * (first + last))
        half = jnp.max(0.5 * (first - last))
        span = half if span is None else jnp.maximum(span, half)
    return refs, span


def _hgrn_fast_chunk(q_tgt, b_tgt, kk_src, v_src, r, C):
    nt = q_tgt.shape[0]
    rows_mask = _head_block_mask(HEADS * C, G, C, DK)
    qz = (q_tgt * jnp.exp(b_tgt - r)).astype(BF16)
    ke = kk_src * jnp.exp(r - b_tgt[0:C])
    kebd = jnp.where(rows_mask, jnp.concatenate([ke] * HEADS, axis=0), 0.0).astype(BF16)
    attn = lax.dot_general(qz, kebd, (((1,), (1,)), ((), ())),
                           preferred_element_type=F32)
    t_idx = lax.broadcasted_iota(jnp.int32, (nt, HEADS * C), 0)
    s_idx = lax.broadcasted_iota(jnp.int32, (nt, HEADS * C), 1) % C
    attn = jnp.where(t_idx >= s_idx, attn, 0.0).astype(BF16)
    vbd = jnp.where(rows_mask, jnp.concatenate([v_src] * HEADS, axis=0), 0.0).astype(BF16)
    return jnp.dot(attn, vbd, preferred_element_type=F32)


def _hgrn_exact_attn(hb_ref, hq_ref, kk, v, o_ref, row0, TT):
    b = hb_ref[0:TT, :]
    ones_bd = jnp.where(_head_block_mask(G, G, DK, DK), 1.0, 0.0).astype(BF16)
    s_row = lax.broadcasted_iota(jnp.int32, (TT, G), 0)

    def body(t, carry):
        bt = hb_ref[pl.ds(t, 1), :]
        qt = hq_ref[pl.ds(t, 1), :]
        e = jnp.where(s_row <= t, qt * kk * jnp.exp(jnp.minimum(bt - b, 0.0)), 0.0)
        a = jnp.dot(e.astype(BF16), ones_bd, preferred_element_type=F32)
        o_ref[pl.ds(row0 + t, 1), :] = jnp.sum(a * v, axis=0, keepdims=True)
        return carry

    lax.fori_loop(0, TT, body, 0)


def _head_norm_gate(o, zg, hnorm):
    ones_bd = jnp.where(_head_block_mask(G, G, DK, DK), 1.0, 0.0).astype(BF16)
    o2 = o * o
    hi = o2.astype(BF16)
    lo = (o2 - hi.astype(F32)).astype(BF16)
    ssq = (jnp.dot(hi, ones_bd, preferred_element_type=F32)
           + jnp.dot(lo, ones_bd, preferred_element_type=F32))
    return o * lax.rsqrt(ssq * (1.0 / DK) + EPS) * hnorm * _silu(zg)


def _pool_select(sums, pos):
    shape = sums[POOL_WINDOWS[0]].shape
    grp = lax.broadcasted_iota(jnp.int32, shape, len(shape) - 1) // POOL_CH
    ssum = sums[POOL_WINDOWS[-1]]
    win = jnp.full(shape, POOL_WINDOWS[-1], jnp.int32)
    for gi in range(len(POOL_WINDOWS) - 2, -1, -1):
        ssum = jnp.where(grp == gi, sums[POOL_WINDOWS[gi]], ssum)
        win = jnp.where(grp == gi, POOL_WINDOWS[gi], win)
    cnt = jnp.minimum(pos + 1, win).astype(F32)
    return ssum / cnt


def _conf_tail(z, cb, lng, lnb):
    z = z + cb
    mu = jnp.mean(z, axis=-1, keepdims=True)
    zc = z - mu
    var = jnp.mean(zc * zc, axis=-1, keepdims=True)
    return _silu(zc * lax.rsqrt(var + EPS) * lng + lnb)


def _layer_prompt_kernel(layer, TT, nt, n_tiles, n_cast, *refs):
    (x_ref, npre_ref, win_ref, convw_ref, poolbd_ref, pscale_ref, lb_ref,
     hnorm_ref, cdw_ref, cb_ref, lng_ref, lnb_ref, wout_ref, npost_ref,
     fpre_ref, wg_ref, wu_ref, wd_ref, fpost_ref) = refs[:19]
    cast_in = refs[19:19 + n_cast]
    y_ref, oconv_ref, opool_ref, ohgrn_ref, oconf_ref = refs[19 + n_cast:24 + n_cast]
    cast_out = refs[24 + n_cast:24 + 2 * n_cast]
    (p_ref, ea_ref, eb_ref, ed_ref, sh_ref, st_ref, hb_ref, hq_ref, hk_ref, oi_ref,
     o_ref, cat_ref, x1_ref, hm_ref, hf_ref, a_ref, ff_ref, mix_ref) = refs[24 + 2 * n_cast:]
    i = pl.program_id(0)
    t = i % nt
    slot = i % 3

    @pl.when(i == 0)
    def _first():
        x1_ref[1] = jnp.zeros((TT, D_MODEL), F32)
        x1_ref[2] = jnp.zeros((TT, D_MODEL), F32)
        a_ref[0] = jnp.zeros((TT, D_FF), BF16)

    @pl.when((t == 0) & (i < n_tiles))
    def _new_sequence():
        ea_ref[0:CONV_PAD, :] = jnp.zeros((CONV_PAD, G), F32)
        eb_ref[0:POOL_PAD, :] = jnp.zeros((POOL_PAD, G), F32)
        ed_ref[0:CONF_PAD, :] = jnp.zeros((CONF_PAD, G), F32)
        st_ref[...] = jnp.zeros((G, G), F32)

    lower = _lower_bound(lb_ref[...], layer)
    n_chunks = TT // HGRN_CHUNK
    slot_up = (i + 2) % 3
    slot_down = (i + 1) % 3
    a_new = (i + 1) % 2
    a_old = i % 2

    def f_norm():
        hf_ref[...] = _rmsnorm(x1_ref[slot_up], fpre_ref[...]).astype(BF16)

    def f_gate_up(j):
        cols = slice(j * FFN_COLS, (j + 1) * FFN_COLS)
        hf = hf_ref[...]
        g = jnp.dot(hf, wg_ref[:, cols], preferred_element_type=F32)
        u = jnp.dot(hf, wu_ref[:, cols], preferred_element_type=F32)
        a_ref[a_new, :, cols] = (_silu(g) * u).astype(BF16)

    def f_down(k):
        cols = slice(k * G, (k + 1) * G)
        ff_ref[:, cols] = jnp.dot(a_ref[a_old], wd_ref[:, cols], preferred_element_type=F32)

    def f_out():
        y_ref[...] = x1_ref[slot_down] + _rmsnorm(ff_ref[...], fpost_ref[...])

    def cast_next():
        for src, dst in zip(cast_in, cast_out):
            dst[...] = src[...].astype(BF16)

    def m_norm():
        hm_ref[...] = _rmsnorm(x_ref[...], npre_ref[...]).astype(BF16)

    def m_proj(blk):
        cols = slice(blk * G, (blk + 1) * G)
        p_ref[:, cols] = jnp.dot(hm_ref[...], win_ref[:, cols], preferred_element_type=F32)

    def m_conv():
        cu = p_ref[:, G:2 * G] * p_ref[:, 2 * G:3 * G]
        ea_ref[CONV_PAD:CONV_PAD + TT, :] = cu
        ya = convw_ref[SC_WIDTH - 1:SC_WIDTH, :] * cu
        for back in range(1, SC_WIDTH):
            w = convw_ref[SC_WIDTH - 1 - back:SC_WIDTH - back, :]
            ya = ya + w * ea_ref[CONV_PAD - back:CONV_PAD - back + TT, :]
        cat_ref[:, 0:G] = (p_ref[:, 0:G] * ya).astype(BF16)
        oconv_ref[...] = ea_ref[TT + CONV_PAD - (SC_WIDTH - 1):TT + CONV_PAD, :]
        ea_ref[0:CONV_PAD, :] = ea_ref[TT:TT + CONV_PAD, :]

    def m_pool():
        pp = p_ref[:, 3 * G:4 * G]
        eb_ref[POOL_PAD:POOL_PAD + TT, :] = pp
        run = eb_ref[...]
        sums = {}
        w = 1
        while w < POOL_WINDOWS[-1]:
            run = run + pltpu.roll(run, w, 0)
            w *= 2
            sums[w] = run[POOL_PAD:]
        pos = t * TT + lax.broadcasted_iota(jnp.int32, (TT, G), 0)
        mean = _pool_select(sums, pos)
        yb = jnp.dot((mean - pp).astype(BF16), poolbd_ref[...], preferred_element_type=F32)
        cat_ref[:, G:2 * G] = (yb * pscale_ref[...]).astype(BF16)
        opool_ref[...] = eb_ref[TT + POOL_PAD - POOL_BUF:TT + POOL_PAD, :]
        eb_ref[0:POOL_PAD, :] = eb_ref[TT:TT + POOL_PAD, :]

    hg = {}

    def m_hgrn_gates():
        q, kk, _, logf = _hgrn_gates(p_ref[:, 4 * G:5 * G], p_ref[:, 5 * G:6 * G],
                                     p_ref[:, 6 * G:7 * G], lower)
        hq_ref[...] = q
        hk_ref[...] = kk
        hb_ref[...] = _cumsum_rows_mxu(logf)

    def m_hgrn_state():
        b = hb_ref[...]
        o_inter = _hgrn_state_terms(hq_ref[...], hk_ref[...], p_ref[:, 6 * G:7 * G], b, st_ref)
        oi_ref[...] = o_inter
        o_ref[...] = o_inter
        hg["refs"], hg["span"] = _hgrn_refs(b, HGRN_CHUNK)

    def m_hgrn_chunk(j):
        lo, hi = j * HGRN_CHUNK, (j + 1) * HGRN_CHUNK
        contrib = _hgrn_fast_chunk(hq_ref[lo:TT, :], hb_ref[lo:TT, :], hk_ref[lo:hi, :],
                                   p_ref[lo:hi, 6 * G:7 * G], hg["refs"][j], HGRN_CHUNK)
        o_ref[lo:TT, :] = o_ref[lo:TT, :] + contrib

    def m_hgrn_out():
        yc = _head_norm_gate(o_ref[...], p_ref[:, 7 * G:8 * G], hnorm_ref[...])
        cat_ref[:, 2 * G:3 * G] = yc.astype(BF16)

    def m_glu():
        ed_ref[CONF_PAD:CONF_PAD + TT, :] = (p_ref[:, 8 * G:9 * G]
                                             * _sigmoid(p_ref[:, 9 * G:10 * G]))
        ed = ed_ref[...]
        for r in range(1, SUBLANES):
            sh_ref[r - 1] = pltpu.roll(ed, TT + CONF_PAD - r, 0)

    def m_conf(rb):
        base = rb * CONV_ROWS
        first = CONF_PAD - (CONF_WIDTH - 1)
        acc = None
        for j in range(CONF_WIDTH):
            tiles, r = divmod(first + j, SUBLANES)
            lo = base + SUBLANES * tiles
            src = ed_ref[lo:lo + CONV_ROWS, :] if r == 0 else sh_ref[r - 1, lo:lo + CONV_ROWS, :]
            term = cdw_ref[j:j + 1, :] * src
            acc = term if acc is None else acc + term
        yd = _conf_tail(acc, cb_ref[...], lng_ref[...], lnb_ref[...])
        cat_ref[base:base + CONV_ROWS, 3 * G:4 * G] = yd.astype(BF16)

    def m_conf_tail():
        oconf_ref[...] = ed_ref[TT + CONF_PAD - (CONF_WIDTH - 1):TT + CONF_PAD, :]
        ed_ref[0:CONF_PAD, :] = ed_ref[TT:TT + CONF_PAD, :]

    def m_out():
        mix_ref[...] = jnp.dot(cat_ref[...], wout_ref[...], preferred_element_type=F32)

    def m_out_norm():
        x1_ref[slot] = x_ref[...] + _rmsnorm(mix_ref[...], npost_ref[...])

    n_gu = D_FF // FFN_COLS
    gate_up = [functools.partial(f_gate_up, j) for j in range(n_gu)]
    proj = [functools.partial(m_proj, blk) for blk in range(N_BLOCKS)]
    down = [functools.partial(f_down, k) for k in range(D_MODEL // G)]
    conf_all = [functools.partial(m_conf, rb) for rb in range(TT // CONV_ROWS)]
    per = len(conf_all) // 4
    conf = [conf_all[k * per:(k + 1) * per] for k in range(4)]
    chunk = [functools.partial(m_hgrn_chunk, j) for j in range(n_chunks)]
    g = gate_up
    schedule = [
        down[0], f_norm, down[1], m_norm, down[2], cast_next, down[3],
        proj[8], proj[9], f_out,
        g[0], m_glu, proj[3],
        g[1], *conf[0], proj[0], proj[1], proj[2],
        g[2], *conf[1], proj[4], proj[5], proj[6],
        g[3], *conf[2], proj[7],
        g[4], *conf[3], m_conf_tail,
        g[5], m_conv, m_pool,
        g[6], m_hgrn_gates,
        g[7], m_hgrn_state,
        g[8], chunk[0], chunk[1],
        g[9], chunk[2], chunk[3],
        m_hgrn_out, m_out, g[10], m_out_norm,
    ]
    assert n_gu == 11 and n_chunks == 4 and len(conf_all) == 4 * per and len(down) == 4
    for piece in schedule:
        piece()

    @pl.when(hg["span"] >= FAST_DECAY_LIMIT)
    def _redo_exact():
        f = lower + (1.0 - lower) * _sigmoid(p_ref[:, 5 * G:6 * G])
        _hgrn_exact_attn(hb_ref, hq_ref, 1.0 - f, p_ref[:, 6 * G:7 * G], o_ref, 0, TT)
        yce = _head_norm_gate(o_ref[...] + oi_ref[...], p_ref[:, 7 * G:8 * G], hnorm_ref[...])
        cat_ref[:, 2 * G:3 * G] = yce.astype(BF16)
        mixe = jnp.dot(cat_ref[...], wout_ref[...], preferred_element_type=F32)
        x1_ref[slot] = x_ref[...] + _rmsnorm(mixe, npost_ref[...])

    @pl.when((t == nt - 1) & (i < n_tiles))
    def _state_out():
        s = st_ref[...].T
        for hh in range(HEADS):
            ohgrn_ref[hh] = s[hh * DK:(hh + 1) * DK, hh * DK:(hh + 1) * DK]


def _layer_sample_kernel(layer, NS, TS, start_pos, n_alias, *refs):
    (x_ref, sconv_hbm, spool_hbm, shgrn_hbm, sconf_hbm,
     npre_ref, win_ref, convw_ref, poolbd_ref, pscale_ref, lb_ref,
     hnorm_ref, cdw_ref, cb_ref, lng_ref, lnb_ref, wout_ref, npost_ref,
     fpre_ref, wg_hbm, wu_hbm, wd_hbm, fpost_ref) = refs[:23]
    y_ref, nconv_hbm, npool_hbm, nhgrn_hbm, nconf_hbm = refs[23 + n_alias:28 + n_alias]
    (oconv_ref, opool_ref, ohgrn_ref, oconf_ref,
     p_ref, u_ref, pool_ref, qT_ref, fT_ref, kT_ref, vT_ref, oT_ref, cat_ref,
     hm_ref, wg_ref, wu_ref, wd_ref, a_ref, ff_ref, sems) = refs[28 + n_alias:]
    i = pl.program_id(0)
    last = pl.num_programs(0) - 1
    copies = [
        (sconv_hbm.at[layer], oconv_ref), (spool_hbm.at[layer], opool_ref),
        (sconf_hbm.at[layer], oconf_ref), (shgrn_hbm.at[layer], ohgrn_ref),
        (wg_hbm.at[0], wg_ref), (wu_hbm.at[0], wu_ref), (wd_hbm.at[0], wd_ref),
        (oconv_ref, nconv_hbm.at[layer]), (opool_ref, npool_hbm.at[layer]),
        (oconf_ref, nconf_hbm.at[layer]), (ohgrn_ref, nhgrn_hbm.at[layer]),
    ]
    first_step_copies = list(range(N_FIRST_STEP_COPIES))
    last_step_copies = list(range(N_FIRST_STEP_COPIES, len(copies)))
    if n_alias == 0:
        for other in range(DEPTH):
            if other != layer:
                copies += [(sconv_hbm.at[other], nconv_hbm.at[other]),
                           (spool_hbm.at[other], npool_hbm.at[other]),
                           (sconf_hbm.at[other], nconf_hbm.at[other]),
                           (shgrn_hbm.at[other], nhgrn_hbm.at[other])]
    slab_fill_copies = list(range(N_FIRST_STEP_COPIES + 4, len(copies)))
    first_step_copies += slab_fill_copies

    def copy(k):
        return pltpu.make_async_copy(copies[k][0], copies[k][1], sems.at[k])

    def wait_at_first_step(ks):
        @pl.when(i == 0)
        def _wait():
            for k in ks:
                copy(k).wait()

    @pl.when(i == 0)
    def _start_copies():
        for k in first_step_copies:
            copy(k).start()

    wait_at_first_step([0, 1, 2])

    def slab(t):
        return slice(t * NS, (t + 1) * NS)

    def proj(blk):
        cols = slice(blk * G, (blk + 1) * G)
        p_ref[:, cols] = jnp.dot(hm_ref[...], win_ref[:, cols], preferred_element_type=F32)

    def conv_in(j):
        if j < SC_WIDTH - 1:
            return oconv_ref[j]
        rows = slab(j - (SC_WIDTH - 1))
        return p_ref[rows, G:2 * G] * p_ref[rows, 2 * G:3 * G]

    def m_conv():
        for t in range(TS):
            ya = (convw_ref[0:1, :] * conv_in(t) + convw_ref[1:2, :] * conv_in(t + 1)
                  + convw_ref[2:3, :] * conv_in(t + 2))
            cat_ref[slab(t), 0:G] = (p_ref[slab(t), 0:G] * ya).astype(BF16)
        for j in range(SC_WIDTH - 1):
            oconv_ref[j] = conv_in(j + TS)

    def pool_in(j):
        if j < POOL_BUF:
            return opool_ref[j]
        return p_ref[slab(j - POOL_BUF), 3 * G:4 * G]

    def m_pool():
        for t in range(TS):
            idx = POOL_BUF + t
            run = pool_in(idx)
            sums = {}
            for j in range(1, POOL_BUF + 1):
                run = run + pool_in(idx - j)
                if j + 1 in POOL_WINDOWS:
                    sums[j + 1] = run
            pos = jnp.full((NS, G), start_pos + i * TS + t, jnp.int32)
            mean = _pool_select(sums, pos)
            pool_ref[slab(t), :] = mean - pool_in(idx)
        yb = jnp.dot(pool_ref[...].astype(BF16), poolbd_ref[...], preferred_element_type=F32)
        cat_ref[:, G:2 * G] = (yb * pscale_ref[...]).astype(BF16)
        for j in range(POOL_BUF):
            opool_ref[j] = pool_in(j + TS)

    def m_hgrn_gates():
        lower = _lower_bound(lb_ref[...], layer)
        q = _silu(p_ref[:, 4 * G:5 * G])
        f = lower + (1.0 - lower) * _sigmoid(p_ref[:, 5 * G:6 * G])
        for t in range(TS):
            qT_ref[t] = q[slab(t)].T
            fT_ref[t] = jnp.maximum(f[slab(t)], F_MIN).T
            kT_ref[t] = (1.0 - f[slab(t)]).T
            vT_ref[t] = p_ref[slab(t), 6 * G:7 * G].T

    def m_hgrn_scan():
        wait_at_first_step([3])
        for hh in range(HEADS):
            head = slice(hh * DK, (hh + 1) * DK)
            vts = [vT_ref[t, head, :] for t in range(TS)]

            def body(k, accs, hh=hh, vts=vts):
                c = hh * DK + k
                s = ohgrn_ref[c]
                out = []
                for t in range(TS):
                    s = fT_ref[t, pl.ds(c, 1), :] * s + kT_ref[t, pl.ds(c, 1), :] * vts[t]
                    out.append(accs[t] + qT_ref[t, pl.ds(c, 1), :] * s)
                ohgrn_ref[c] = s
                return tuple(out)

            accs = lax.fori_loop(0, DK, body,
                                 tuple(jnp.zeros((DK, NS), F32) for _ in range(TS)), unroll=2)
            for t in range(TS):
                oT_ref[t, head, :] = accs[t]

    def m_hgrn_out():
        o = jnp.concatenate([oT_ref[t].T for t in range(TS)], axis=0)
        yc = _head_norm_gate(o, p_ref[:, 7 * G:8 * G], hnorm_ref[...])
        cat_ref[:, 2 * G:3 * G] = yc.astype(BF16)

    def m_glu():
        u_ref[...] = p_ref[:, 8 * G:9 * G] * _sigmoid(p_ref[:, 9 * G:10 * G])

    HALF = NS // 2

    def conf_in(j, rows):
        if j < CONF_WIDTH - 1:
            return oconf_ref[j, rows, :]
        base = (j - (CONF_WIDTH - 1)) * NS
        return u_ref[base + rows.start:base + rows.stop, :]

    def m_conf():
        for t in range(TS):
            for hf in range(2):
                rows = slice(hf * HALF, (hf + 1) * HALF)
                acc = None
                for j in range(CONF_WIDTH):
                    term = cdw_ref[j:j + 1, :] * conf_in(t + j, rows)
                    acc = term if acc is None else acc + term
                yd = _conf_tail(acc, cb_ref[...], lng_ref[...], lnb_ref[...])
                cat_ref[t * NS + hf * HALF:t * NS + (hf + 1) * HALF, 3 * G:4 * G] = (
                    yd.astype(BF16))
        for j in range(CONF_WIDTH - 1):
            oconf_ref[j] = conf_in(j + TS, slice(0, NS))

    hm_ref[...] = _rmsnorm(x_ref[...], npre_ref[...]).astype(BF16)
    for blk in range(N_BLOCKS):
        proj(blk)
    m_conv()
    m_pool()
    m_hgrn_gates()
    m_hgrn_scan()
    m_hgrn_out()
    m_glu()
    m_conf()
    mix = jnp.dot(cat_ref[...], wout_ref[...], preferred_element_type=F32)
    y_ref[...] = x_ref[...] + _rmsnorm(mix, npost_ref[...])

    wait_at_first_step([4, 5, 6])
    hm_ref[...] = _rmsnorm(y_ref[...], fpre_ref[...]).astype(BF16)
    for j in range(D_FF // FFN_COLS):
        cols = slice(j * FFN_COLS, (j + 1) * FFN_COLS)
        hf = hm_ref[...]
        g = jnp.dot(hf, wg_ref[:, cols], preferred_element_type=F32)
        u = jnp.dot(hf, wu_ref[:, cols], preferred_element_type=F32)
        a_ref[:, cols] = (_silu(g) * u).astype(BF16)
    for k in range(D_MODEL // G):
        cols = slice(k * G, (k + 1) * G)
        ff_ref[:, cols] = jnp.dot(a_ref[...], wd_ref[:, cols], preferred_element_type=F32)
    y_ref[...] = y_ref[...] + _rmsnorm(ff_ref[...], fpost_ref[...])

    @pl.when(i == last)
    def _states_out():
        for k in last_step_copies:
            copy(k).start()
        for k in last_step_copies + slab_fill_copies:
            copy(k).wait()


def _layer_spec(shape, layer, single_buffer=False):
    nd = len(shape)

    def imap(*_):
        return (layer,) + (0,) * nd

    if single_buffer:
        return pl.BlockSpec((None,) + tuple(shape), imap, pipeline_mode=pl.Buffered(1))
    return pl.BlockSpec((None,) + tuple(shape), imap)


def _mixer_weight_specs(layer):
    return [
        _layer_spec((1, D_MODEL), layer),
        _layer_spec((D_MODEL, D_IN), 0, True),
        _layer_spec((SC_WIDTH, G), layer),
        _layer_spec((G, G), layer),
        _layer_spec((1, G), layer),
        pl.BlockSpec((DEPTH, G), lambda *_: (0, 0)),
        _layer_spec((1, G), layer),
        _layer_spec((CONF_WIDTH, G), layer),
        _layer_spec((1, G), layer),
        _layer_spec((1, G), layer),
        _layer_spec((1, G), layer),
        _layer_spec((D_MODEL, D_MODEL), 0, True),
        _layer_spec((1, D_MODEL), layer),
    ]


def _ffn_weight_specs(layer):
    return [
        _layer_spec((1, D_MODEL), layer),
        _layer_spec((D_MODEL, D_FF), 0, True),
        _layer_spec((D_MODEL, D_FF), 0, True),
        _layer_spec((D_FF, D_MODEL), 0, True),
        _layer_spec((1, D_MODEL), layer),
    ]


def _layer_prompt(layer, x, mixer_wts, ffn_wts, next_f32):
    n, seq, _ = x.shape
    TT = PROMPT_TILE
    nt = seq // TT
    last = n * nt - 1
    n_steps = n * nt + 2

    def mix_tile(i):
        return jnp.minimum(i, last)

    def ffn_tile(i):
        return jnp.maximum(i - 2, 0)

    out_shape = [
        jax.ShapeDtypeStruct((n, seq, D_MODEL), F32),
        jax.ShapeDtypeStruct((n, SC_WIDTH - 1, G), F32),
        jax.ShapeDtypeStruct((n, POOL_BUF, G), F32),
        jax.ShapeDtypeStruct((n, HEADS, DK, DK), F32),
        jax.ShapeDtypeStruct((n, CONF_WIDTH - 1, G), F32),
    ]
    out_specs = [
        pl.BlockSpec((None, TT, D_MODEL), lambda i: (ffn_tile(i) // nt, ffn_tile(i) % nt, 0)),
        pl.BlockSpec((None, SC_WIDTH - 1, G), lambda i: (mix_tile(i) // nt, 0, 0)),
        pl.BlockSpec((None, POOL_BUF, G), lambda i: (mix_tile(i) // nt, 0, 0)),
        pl.BlockSpec((None, HEADS, DK, DK), lambda i: (mix_tile(i) // nt, 0, 0, 0)),
        pl.BlockSpec((None, CONF_WIDTH - 1, G), lambda i: (mix_tile(i) // nt, 0, 0)),
    ]
    cast_in_specs = []
    for w in next_f32:
        _, rows, cols = w.shape
        blk = next(b for b in range(CAST_ROWS, rows + 1, CAST_ROWS)
                   if rows % b == 0 and rows // b <= n_steps)
        n_blk = rows // blk

        def in_map(i, n_blk=n_blk):
            return (layer + 1, jnp.minimum(i, n_blk - 1), 0)

        def out_map(i, n_blk=n_blk):
            return (0, jnp.minimum(i, n_blk - 1), 0)

        cast_in_specs.append(pl.BlockSpec((None, blk, cols), in_map))
        out_specs.append(pl.BlockSpec((None, blk, cols), out_map))
        out_shape.append(jax.ShapeDtypeStruct((1, rows, cols), BF16))
    scratch = [
        pltpu.VMEM((TT, D_IN), F32),
        pltpu.VMEM((CONV_PAD + TT, G), F32),
        pltpu.VMEM((POOL_PAD + TT, G), F32),
        pltpu.VMEM((CONF_PAD + TT, G), F32),
        pltpu.VMEM((SUBLANES - 1, CONF_PAD + TT, G), F32),
        pltpu.VMEM((G, G), F32),
        pltpu.VMEM((TT, G), F32),
        pltpu.VMEM((TT, G), F32),
        pltpu.VMEM((TT, G), F32),
        pltpu.VMEM((TT, G), F32),
        pltpu.VMEM((TT, G), F32),
        pltpu.VMEM((TT, D_MODEL), BF16),
        pltpu.VMEM((3, TT, D_MODEL), F32),
        pltpu.VMEM((TT, D_MODEL), BF16),
        pltpu.VMEM((TT, D_MODEL), BF16),
        pltpu.VMEM((2, TT, D_FF), BF16),
        pltpu.VMEM((TT, D_MODEL), F32),
        pltpu.VMEM((TT, D_MODEL), F32),
    ]
    return pl.pallas_call(
        functools.partial(_layer_prompt_kernel, layer, TT, nt, n * nt, len(next_f32)),
        grid=(n_steps,),
        in_specs=[pl.BlockSpec((None, TT, D_MODEL),
                               lambda i: (mix_tile(i) // nt, mix_tile(i) % nt, 0))]
        + _mixer_weight_specs(layer) + _ffn_weight_specs(layer) + cast_in_specs,
        out_specs=tuple(out_specs),
        out_shape=tuple(out_shape),
        scratch_shapes=scratch,
        compiler_params=pltpu.CompilerParams(
            dimension_semantics=("arbitrary",), vmem_limit_bytes=VMEM_LIMIT),
        name=f"layer_prompt_l{layer}",
    )(x, *mixer_wts, *ffn_wts, *next_f32)


def _layer_sample(layer, x2d, s_conv, s_pool, s_hgrn, s_conf, mixer_wts, ffn_wts, n_seq,
                  new_states):
    m = x2d.shape[0]
    NS = n_seq
    TS = SAMPLE_STEPS
    M = TS * NS
    n_alias = len(new_states)

    in_hbm = pl.BlockSpec(memory_space=pl.ANY)
    state_shapes = [(SC_WIDTH - 1, NS, G), (POOL_BUF, NS, G), (G, DK, NS), (CONF_WIDTH - 1, NS, G)]
    in_specs = ([pl.BlockSpec((M, D_MODEL), lambda i: (i, 0))]
                + [in_hbm] * len(state_shapes) + _mixer_weight_specs(layer)
                + [_layer_spec((1, D_MODEL), layer), in_hbm, in_hbm, in_hbm,
                   _layer_spec((1, D_MODEL), layer)]
                + [in_hbm] * n_alias)
    n_in = len(in_specs) - n_alias
    out_specs = tuple([pl.BlockSpec((M, D_MODEL), lambda i: (i, 0))]
                      + [in_hbm] * len(state_shapes))
    out_shape = tuple([jax.ShapeDtypeStruct((m, D_MODEL), F32)]
                      + [jax.ShapeDtypeStruct((DEPTH,) + s, F32) for s in state_shapes])
    n_copies = N_FIRST_STEP_COPIES + 4 + (0 if n_alias else 4 * (DEPTH - 1))
    scratch = [
        pltpu.VMEM(state_shapes[0], F32),
        pltpu.VMEM(state_shapes[1], F32),
        pltpu.VMEM(state_shapes[2], F32),
        pltpu.VMEM(state_shapes[3], F32),
        pltpu.VMEM((M, D_IN), F32),
        pltpu.VMEM((M, G), F32),
        pltpu.VMEM((M, G), F32),
        pltpu.VMEM((TS, G, NS), F32),
        pltpu.VMEM((TS, G, NS), F32),
        pltpu.VMEM((TS, G, NS), F32),
        pltpu.VMEM((TS, G, NS), F32),
        pltpu.VMEM((TS, G, NS), F32),
        pltpu.VMEM((M, D_MODEL), BF16),
        pltpu.VMEM((M, D_MODEL), BF16),
        pltpu.VMEM((D_MODEL, D_FF), BF16),
        pltpu.VMEM((D_MODEL, D_FF), BF16),
        pltpu.VMEM((D_FF, D_MODEL), BF16),
        pltpu.VMEM((M, D_FF), BF16),
        pltpu.VMEM((M, D_MODEL), F32),
        pltpu.SemaphoreType.DMA((n_copies,)),
    ]
    return pl.pallas_call(
        functools.partial(_layer_sample_kernel, layer, NS, TS, PAST_LEN, n_alias),
        grid=(m // M,),
        in_specs=in_specs,
        out_specs=out_specs,
        out_shape=out_shape,
        scratch_shapes=scratch,
        input_output_aliases={n_in + k: 1 + k for k in range(n_alias)},
        compiler_params=pltpu.CompilerParams(
            dimension_semantics=("arbitrary",), vmem_limit_bytes=VMEM_LIMIT),
        name=f"layer_sample_l{layer}",
    )(x2d, s_conv, s_pool, s_hgrn, s_conf, *mixer_wts, *ffn_wts, *new_states)


def kernel(x_prompt, x_sample, state_conv, state_pool, state_hgrn, state_conf, norm_mix_pre, norm_mix_post, w_in, conv_w, pool_w, pool_scale, hgrn_lb, hgrn_norm, conf_dw, conf_b, conf_ln_g, conf_ln_b, w_out, norm_ffn_pre, norm_ffn_post, w_gate, w_up, w_down):
    def row(a):
        return a.reshape(DEPTH, 1, a.shape[-1])

    eye = jnp.eye(G // POOL_CH, dtype=pool_w.dtype)
    pool_bd = (pool_w[:, :, :, None, :] * eye[None, :, None, :, None]).reshape(DEPTH, G, G)
    pool_bd = pool_bd.astype(BF16)
    big_f32 = (w_in, w_out, w_gate, w_up, w_down)
    big = tuple(w[0:1].astype(BF16) for w in big_f32)

    def mixer_weights(big):
        return (row(norm_mix_pre), big[0], conv_w, pool_bd, row(pool_scale), hgrn_lb,
                row(hgrn_norm), conf_dw, row(conf_b), row(conf_ln_g), row(conf_ln_b), big[1],
                row(norm_mix_post))

    def ffn_weights(big):
        return (row(norm_ffn_pre), big[2], big[3], big[4], row(norm_ffn_post))

    ns, ts, _ = x_sample.shape
    xs = x_sample.transpose(1, 0, 2).reshape(ts * ns, D_MODEL)
    sc_t = state_conv.transpose(1, 2, 0, 3)
    sp_t = state_pool.transpose(1, 2, 0, 3)
    sf_t = state_conf.transpose(1, 2, 0, 3)
    sh_t = state_hgrn.transpose(1, 2, 3, 4, 0).reshape(DEPTH, G, DK, ns)
    xp = x_prompt
    p_states, s_states = [], []
    for layer in range(DEPTH):
        mixer_wts, ffn_wts = mixer_weights(big), ffn_weights(big)
        next_f32 = big_f32 if layer + 1 < DEPTH else ()
        xp, *rest = _layer_prompt(layer, xp, mixer_wts, ffn_wts, next_f32)
        p_states.append(rest[:4])
        big = tuple(rest[4:])
        xs, *s_states = _layer_sample(layer, xs, sc_t, sp_t, sh_t, sf_t, mixer_wts, ffn_wts, ns,
                                      s_states)
    xs = xs.reshape(ts, ns, D_MODEL).transpose(1, 0, 2)

    def stack(states, i):
        return jnp.stack([states[layer][i] for layer in range(DEPTH)], axis=1)

    def seq_major(i):
        return s_states[i].transpose(2, 0, 1, 3)

    hgrn_s = s_states[2].reshape(DEPTH, HEADS, DK, DK, ns).transpose(4, 0, 1, 2, 3)

    return (xp, xs,
            stack(p_states, 0), stack(p_states, 1), stack(p_states, 2), stack(p_states, 3),
            seq_major(0), seq_major(1), hgrn_s, seq_major(3))
```

```python
import functools

import jax
import jax.numpy as jnp
from jax import lax
from jax.experimental import pallas as pl
from jax.experimental.pallas import tpu as pltpu

F32 = jnp.float32
BF16 = jnp.bfloat16

D_MODEL = 1024
DEPTH = 2
PAST_LEN = 16384
G = 256
N_BLOCKS = 10
D_IN = N_BLOCKS * G
SC_WIDTH = 3
POOL_WINDOWS = (2, 4, 8, 16)
POOL_BUF = max(POOL_WINDOWS) - 1
POOL_CH = 64
HEADS = 4
DK = 64
CONF_WIDTH = 31
D_FF = 2816
EPS = 1e-6
F_MIN = 1e-20

SUBLANES = 8


def _round_up(n, m):
    return -(-n // m) * m


CONV_PAD = _round_up(SC_WIDTH - 1, SUBLANES)
POOL_PAD = _round_up(POOL_BUF, SUBLANES)
CONF_PAD = _round_up(CONF_WIDTH - 1, SUBLANES)

HGRN_CHUNK = 64
FAST_DECAY_LIMIT = 60.0
PROMPT_TILE = 256
SAMPLE_STEPS = 2
FFN_COLS = 256
N_FIRST_STEP_COPIES = 7
CAST_ROWS = 16
CONV_ROWS = 64
VMEM_LIMIT = 56 * 1024 * 1024


def _sigmoid(x):
    return jax.nn.sigmoid(x)


def _silu(x):
    return x * jax.nn.sigmoid(x)


def _rmsnorm(x, g):
    ms = jnp.mean(x * x, axis=-1, keepdims=True)
    return x * lax.rsqrt(ms + EPS) * g


def _head_block_mask(rows, cols, row_block, col_block):
    r = lax.broadcasted_iota(jnp.int32, (rows, cols), 0) // row_block
    c = lax.broadcasted_iota(jnp.int32, (rows, cols), 1) // col_block
    return r == c


def _cumsum_rows_mxu(x):
    n = x.shape[0]
    tri = (lax.broadcasted_iota(jnp.int32, (n, n), 0)
           >= lax.broadcasted_iota(jnp.int32, (n, n), 1))
    tri = jnp.where(tri, 1.0, 0.0).astype(BF16)
    hi = x.astype(BF16)
    r1 = x - hi.astype(F32)
    mid = r1.astype(BF16)
    lo = (r1 - mid.astype(F32)).astype(BF16)
    return (jnp.dot(tri, hi, preferred_element_type=F32)
            + jnp.dot(tri, mid, preferred_element_type=F32)
            + jnp.dot(tri, lo, preferred_element_type=F32))


def _lower_bound(lb_all, layer):
    m = jnp.max(lb_all, axis=0, keepdims=True)
    e = jnp.exp(lb_all - m)
    sm = e / jnp.sum(e, axis=0, keepdims=True)
    cs = sm[0:1]
    for i in range(1, layer + 1):
        cs = cs + sm[i:i + 1]
    return cs - sm[0:1]


def _hgrn_gates(zq, zf, zi, lower):
    q = _silu(zq)
    f = lower + (1.0 - lower) * _sigmoid(zf)
    logf = jnp.log(jnp.maximum(f, F_MIN))
    return q, 1.0 - f, zi, logf


def _hgrn_state_terms(q, kk, v, b, st_ref):
    TT = q.shape[0]
    b_end = b[TT - 1:TT, :]
    st = st_ref[...]
    qs = (q * jnp.exp(b)).astype(BF16)
    o_inter = lax.dot_general(qs, st.astype(BF16), (((1,), (1,)), ((), ())),
                              preferred_element_type=F32)
    kh = (kk * jnp.exp(b_end - b)).astype(BF16)
    upd = lax.dot_general(v.astype(BF16), kh, (((0,), (0,)), ((), ())),
                          preferred_element_type=F32)
    bd = _head_block_mask(G, G, DK, DK)
    st_ref[...] = st * jnp.exp(b_end) + jnp.where(bd, upd, 0.0)
    return o_inter


def _hgrn_refs(b, C):
    refs = []
    span = None
    for j in range(b.shape[0] // C):
        first = b[j * C:j * C + 1, :]
        last = b[(j + 1) * C - 1:(j + 1) * C, :]
        refs.append(0.5 * (first + last))
        half = jnp.max(0.5 * (first - last))
        span = half if span is None else jnp.maximum(span, half)
    return refs, span


def _hgrn_fast_chunk(q_tgt, b_tgt, kk_src, v_src, r, C):
    nt = q_tgt.shape[0]
    rows_mask = _head_block_mask(HEADS * C, G, C, DK)
    qz = (q_tgt * jnp.exp(b_tgt - r)).astype(BF16)
    ke = kk_src * jnp.exp(r - b_tgt[0:C])
    kebd = jnp.where(rows_mask, jnp.concatenate([ke] * HEADS, axis=0), 0.0).astype(BF16)
    attn = lax.dot_general(qz, kebd, (((1,), (1,)), ((), ())),
                           preferred_element_type=F32)
    t_idx = lax.broadcasted_iota(jnp.int32, (nt, HEADS * C), 0)
    s_idx = lax.broadcasted_iota(jnp.int32, (nt, HEADS * C), 1) % C
    attn = jnp.where(t_idx >= s_idx, attn, 0.0).astype(BF16)
    vbd = jnp.where(rows_mask, jnp.concatenate([v_src] * HEADS, axis=0), 0.0).astype(BF16)
    return jnp.dot(attn, vbd, preferred_element_type=F32)


def _hgrn_exact_attn(hb_ref, hq_ref, kk, v, o_ref, row0, TT):
    b = hb_ref[0:TT, :]
    ones_bd = jnp.where(_head_block_mask(G, G, DK, DK), 1.0, 0.0).astype(BF16)
    s_row = lax.broadcasted_iota(jnp.int32, (TT, G), 0)

    def body(t, carry):
        bt = hb_ref[pl.ds(t, 1), :]
        qt = hq_ref[pl.ds(t, 1), :]
        e = jnp.where(s_row <= t, qt * kk * jnp.exp(jnp.minimum(bt - b, 0.0)), 0.0)
        a = jnp.dot(e.astype(BF16), ones_bd, preferred_element_type=F32)
        o_ref[pl.ds(row0 + t, 1), :] = jnp.sum(a * v, axis=0, keepdims=True)
        return carry

    lax.fori_loop(0, TT, body, 0)


def _head_norm_gate(o, zg, hnorm):
    ones_bd = jnp.where(_head_block_mask(G, G, DK, DK), 1.0, 0.0).astype(BF16)
    o2 = o * o
    hi = o2.astype(BF16)
    lo = (o2 - hi.astype(F32)).astype(BF16)
    ssq = (jnp.dot(hi, ones_bd, preferred_element_type=F32)
           + jnp.dot(lo, ones_bd, preferred_element_type=F32))
    return o * lax.rsqrt(ssq * (1.0 / DK) + EPS) * hnorm * _silu(zg)


def _pool_select(sums, pos):
    shape = sums[POOL_WINDOWS[0]].shape
    grp = lax.broadcasted_iota(jnp.int32, shape, len(shape) - 1) // POOL_CH
    ssum = sums[POOL_WINDOWS[-1]]
    win = jnp.full(shape, POOL_WINDOWS[-1], jnp.int32)
    for gi in range(len(POOL_WINDOWS) - 2, -1, -1):
        ssum = jnp.where(grp == gi, sums[POOL_WINDOWS[gi]], ssum)
        win = jnp.where(grp == gi, POOL_WINDOWS[gi], win)
    cnt = jnp.minimum(pos + 1, win).astype(F32)
    return ssum / cnt


def _conf_tail(z, cb, lng, lnb):
    z = z + cb
    mu = jnp.mean(z, axis=-1, keepdims=True)
    zc = z - mu
    var = jnp.mean(zc * zc, axis=-1, keepdims=True)
    return _silu(zc * lax.rsqrt(var + EPS) * lng + lnb)


def _layer_prompt_kernel(layer, TT, nt, n_tiles, n_cast, *refs):
    (x_ref, npre_ref, win_ref, convw_ref, poolbd_ref, pscale_ref, lb_ref,
     hnorm_ref, cdw_ref, cb_ref, lng_ref, lnb_ref, wout_ref, npost_ref,
     fpre_ref, wg_ref, wu_ref, wd_ref, fpost_ref) = refs[:19]
    cast_in = refs[19:19 + n_cast]
    y_ref, oconv_ref, opool_ref, ohgrn_ref, oconf_ref = refs[19 + n_cast:24 + n_cast]
    cast_out = refs[24 + n_cast:24 + 2 * n_cast]
    (p_ref, ea_ref, eb_ref, ed_ref, sh_ref, st_ref, hb_ref, hq_ref, hk_ref, oi_ref,
     o_ref, cat_ref, x1_ref, hm_ref, hf_ref, a_ref, ff_ref, mix_ref) = refs[24 + 2 * n_cast:]
    i = pl.program_id(0)
    t = i % nt
    slot = i % 3

    @pl.when(i == 0)
    def _first():
        x1_ref[1] = jnp.zeros((TT, D_MODEL), F32)
        x1_ref[2] = jnp.zeros((TT, D_MODEL), F32)
        a_ref[0] = jnp.zeros((TT, D_FF), BF16)

    @pl.when((t == 0) & (i < n_tiles))
    def _new_sequence():
        ea_ref[0:CONV_PAD, :] = jnp.zeros((CONV_PAD, G), F32)
        eb_ref[0:POOL_PAD, :] = jnp.zeros((POOL_PAD, G), F32)
        ed_ref[0:CONF_PAD, :] = jnp.zeros((CONF_PAD, G), F32)
        st_ref[...] = jnp.zeros((G, G), F32)

    lower = _lower_bound(lb_ref[...], layer)
    n_chunks = TT // HGRN_CHUNK
    slot_up = (i + 2) % 3
    slot_down = (i + 1) % 3
    a_new = (i + 1) % 2
    a_old = i % 2

    def f_norm():
        hf_ref[...] = _rmsnorm(x1_ref[slot_up], fpre_ref[...]).astype(BF16)

    def f_gate_up(j):
        cols = slice(j * FFN_COLS, (j + 1) * FFN_COLS)
        hf = hf_ref[...]
        g = jnp.dot(hf, wg_ref[:, cols], preferred_element_type=F32)
        u = jnp.dot(hf, wu_ref[:, cols], preferred_element_type=F32)
        a_ref[a_new, :, cols] = (_silu(g) * u).astype(BF16)

    def f_down(k):
        cols = slice(k * G, (k + 1) * G)
        ff_ref[:, cols] = jnp.dot(a_ref[a_old], wd_ref[:, cols], preferred_element_type=F32)

    def f_out():
        y_ref[...] = x1_ref[slot_down] + _rmsnorm(ff_ref[...], fpost_ref[...])

    def cast_next():
        for src, dst in zip(cast_in, cast_out):
            dst[...] = src[...].astype(BF16)

    def m_norm():
        hm_ref[...] = _rmsnorm(x_ref[...], npre_ref[...]).astype(BF16)

    def m_proj(blk):
        cols = slice(blk * G, (blk + 1) * G)
        p_ref[:, cols] = jnp.dot(hm_ref[...], win_ref[:, cols], preferred_element_type=F32)

    def m_conv():
        cu = p_ref[:, G:2 * G] * p_ref[:, 2 * G:3 * G]
        ea_ref[CONV_PAD:CONV_PAD + TT, :] = cu
        ya = convw_ref[SC_WIDTH - 1:SC_WIDTH, :] * cu
        for back in range(1, SC_WIDTH):
            w = convw_ref[SC_WIDTH - 1 - back:SC_WIDTH - back, :]
            ya = ya + w * ea_ref[CONV_PAD - back:CONV_PAD - back + TT, :]
        cat_ref[:, 0:G] = (p_ref[:, 0:G] * ya).astype(BF16)
        oconv_ref[...] = ea_ref[TT + CONV_PAD - (SC_WIDTH - 1):TT + CONV_PAD, :]
        ea_ref[0:CONV_PAD, :] = ea_ref[TT:TT + CONV_PAD, :]

    def m_pool():
        pp = p_ref[:, 3 * G:4 * G]
        eb_ref[POOL_PAD:POOL_PAD + TT, :] = pp
        run = eb_ref[...]
        sums = {}
        w = 1
        while w < POOL_WINDOWS[-1]:
            run = run + pltpu.roll(run, w, 0)
            w *= 2
            sums[w] = run[POOL_PAD:]
        pos = t * TT + lax.broadcasted_iota(jnp.int32, (TT, G), 0)
        mean = _pool_select(sums, pos)
        yb = jnp.dot((mean - pp).astype(BF16), poolbd_ref[...], preferred_element_type=F32)
        cat_ref[:, G:2 * G] = (yb * pscale_ref[...]).astype(BF16)
        opool_ref[...] = eb_ref[TT + POOL_PAD - POOL_BUF:TT + POOL_PAD, :]
        eb_ref[0:POOL_PAD, :] = eb_ref[TT:TT + POOL_PAD, :]

    hg = {}

    def m_hgrn_gates():
        q, kk, _, logf = _hgrn_gates(p_ref[:, 4 * G:5 * G], p_ref[:, 5 * G:6 * G],
                                     p_ref[:, 6 * G:7 * G], lower)
        hq_ref[...] = q
        hk_ref[...] = kk
        hb_ref[...] = _cumsum_rows_mxu(logf)

    def m_hgrn_state():
        b = hb_ref[...]
        o_inter = _hgrn_state_terms(hq_ref[...], hk_ref[...], p_ref[:, 6 * G:7 * G], b, st_ref)
        oi_ref[...] = o_inter
        o_ref[...] = o_inter
        hg["refs"], hg["span"] = _hgrn_refs(b, HGRN_CHUNK)

    def m_hgrn_chunk(j):
        lo, hi = j * HGRN_CHUNK, (j + 1) * HGRN_CHUNK
        contrib = _hgrn_fast_chunk(hq_ref[lo:TT, :], hb_ref[lo:TT, :], hk_ref[lo:hi, :],
                                   p_ref[lo:hi, 6 * G:7 * G], hg["refs"][j], HGRN_CHUNK)
        o_ref[lo:TT, :] = o_ref[lo:TT, :] + contrib

    def m_hgrn_out():
        yc = _head_norm_gate(o_ref[...], p_ref[:, 7 * G:8 * G], hnorm_ref[...])
        cat_ref[:, 2 * G:3 * G] = yc.astype(BF16)

    def m_glu():
        ed_ref[CONF_PAD:CONF_PAD + TT, :] = (p_ref[:, 8 * G:9 * G]
                                             * _sigmoid(p_ref[:, 9 * G:10 * G]))
        ed = ed_ref[...]
        for r in range(1, SUBLANES):
            sh_ref[r - 1] = pltpu.roll(ed, TT + CONF_PAD - r, 0)

    def m_conf(rb):
        base = rb * CONV_ROWS
        first = CONF_PAD - (CONF_WIDTH - 1)
        acc = None
        for j in range(CONF_WIDTH):
            tiles, r = divmod(first + j, SUBLANES)
            lo = base + SUBLANES * tiles
            src = ed_ref[lo:lo + CONV_ROWS, :] if r == 0 else sh_ref[r - 1, lo:lo + CONV_ROWS, :]
            term = cdw_ref[j:j + 1, :] * src
            acc = term if acc is None else acc + term
        yd = _conf_tail(acc, cb_ref[...], lng_ref[...], lnb_ref[...])
        cat_ref[base:base + CONV_ROWS, 3 * G:4 * G] = yd.astype(BF16)

    def m_conf_tail():
        oconf_ref[...] = ed_ref[TT + CONF_PAD - (CONF_WIDTH - 1):TT + CONF_PAD, :]
        ed_ref[0:CONF_PAD, :] = ed_ref[TT:TT + CONF_PAD, :]

    def m_out():
        mix_ref[...] = jnp.dot(cat_ref[...], wout_ref[...], preferred_element_type=F32)

    def m_out_norm():
        x1_ref[slot] = x_ref[...] + _rmsnorm(mix_ref[...], npost_ref[...])

    n_gu = D_FF // FFN_COLS
    gate_up = [functools.partial(f_gate_up, j) for j in range(n_gu)]
    proj = [functools.partial(m_proj, blk) for blk in range(N_BLOCKS)]
    down = [functools.partial(f_down, k) for k in range(D_MODEL // G)]
    conf_all = [functools.partial(m_conf, rb) for rb in range(TT // CONV_ROWS)]
    per = len(conf_all) // 4
    conf = [conf_all[k * per:(k + 1) * per] for k in range(4)]
    chunk = [functools.partial(m_hgrn_chunk, j) for j in range(n_chunks)]
    g = gate_up
    schedule = [
        down[0], f_norm, down[1], m_norm, down[2], cast_next, down[3],
        proj[8], proj[9], f_out,
        g[0], m_glu, proj[3],
        g[1], *conf[0], proj[0], proj[1], proj[2],
        g[2], *conf[1], proj[4], proj[5], proj[6],
        g[3], *conf[2], proj[7],
        g[4], *conf[3], m_conf_tail,
        g[5], m_conv, m_pool,
        g[6], m_hgrn_gates,
        g[7], m_hgrn_state,
        g[8], chunk[0], chunk[1],
        g[9], chunk[2], chunk[3],
        m_hgrn_out, m_out, g[10], m_out_norm,
    ]
    assert n_gu == 11 and n_chunks == 4 and len(conf_all) == 4 * per and len(down) == 4
    for piece in schedule:
        piece()

    @pl.when(hg["span"] >= FAST_DECAY_LIMIT)
    def _redo_exact():
        f = lower + (1.0 - lower) * _sigmoid(p_ref[:, 5 * G:6 * G])
        _hgrn_exact_attn(hb_ref, hq_ref, 1.0 - f, p_ref[:, 6 * G:7 * G], o_ref, 0, TT)
        yce = _head_norm_gate(o_ref[...] + oi_ref[...], p_ref[:, 7 * G:8 * G], hnorm_ref[...])
        cat_ref[:, 2 * G:3 * G] = yce.astype(BF16)
        mixe = jnp.dot(cat_ref[...], wout_ref[...], preferred_element_type=F32)
        x1_ref[slot] = x_ref[...] + _rmsnorm(mixe, npost_ref[...])

    @pl.when((t == nt - 1) & (i < n_tiles))
    def _state_out():
        s = st_ref[...].T
        for hh in range(HEADS):
            ohgrn_ref[hh] = s[hh * DK:(hh + 1) * DK, hh * DK:(hh + 1) * DK]


def _layer_sample_kernel(layer, NS, TS, start_pos, n_alias, *refs):
    (x_ref, sconv_hbm, spool_hbm, shgrn_hbm, sconf_hbm,
     npre_ref, win_ref, convw_ref, poolbd_ref, pscale_ref, lb_ref,
     hnorm_ref, cdw_ref, cb_ref, lng_ref, lnb_ref, wout_ref, npost_ref,
     fpre_ref, wg_hbm, wu_hbm, wd_hbm, fpost_ref) = refs[:23]
    y_ref, nconv_hbm, npool_hbm, nhgrn_hbm, nconf_hbm = refs[23 + n_alias:28 + n_alias]
    (oconv_ref, opool_ref, ohgrn_ref, oconf_ref,
     p_ref, u_ref, pool_ref, qT_ref, fT_ref, kT_ref, vT_ref, oT_ref, cat_ref,
     hm_ref, wg_ref, wu_ref, wd_ref, a_ref, ff_ref, sems) = refs[28 + n_alias:]
    i = pl.program_id(0)
    last = pl.num_programs(0) - 1
    copies = [
        (sconv_hbm.at[layer], oconv_ref), (spool_hbm.at[layer], opool_ref),
        (sconf_hbm.at[layer], oconf_ref), (shgrn_hbm.at[layer], ohgrn_ref),
        (wg_hbm.at[0], wg_ref), (wu_hbm.at[0], wu_ref), (wd_hbm.at[0], wd_ref),
        (oconv_ref, nconv_hbm.at[layer]), (opool_ref, npool_hbm.at[layer]),
        (oconf_ref, nconf_hbm.at[layer]), (ohgrn_ref, nhgrn_hbm.at[layer]),
    ]
    first_step_copies = list(range(N_FIRST_STEP_COPIES))
    if n_alias == 0:
        for other in range(DEPTH):
            if other != layer:
                copies += [(oconv_ref, nconv_hbm.at[other]), (opool_ref, npool_hbm.at[other]),
                           (oconf_ref, nconf_hbm.at[other]), (ohgrn_ref, nhgrn_hbm.at[other])]
    last_step_copies = list(range(N_FIRST_STEP_COPIES, len(copies)))

    def copy(k):
        return pltpu.make_async_copy(copies[k][0], copies[k][1], sems.at[k])

    def wait_at_first_step(ks):
        @pl.when(i == 0)
        def _wait():
            for k in ks:
                copy(k).wait()

    @pl.when(i == 0)
    def _start_copies():
        for k in first_step_copies:
            copy(k).start()

    wait_at_first_step([0, 1, 2])

    def slab(t):
        return slice(t * NS, (t + 1) * NS)

    def proj(blk):
        cols = slice(blk * G, (blk + 1) * G)
        p_ref[:, cols] = jnp.dot(hm_ref[...], win_ref[:, cols], preferred_element_type=F32)

    def conv_in(j):
        if j < SC_WIDTH - 1:
            return oconv_ref[j]
        rows = slab(j - (SC_WIDTH - 1))
        return p_ref[rows, G:2 * G] * p_ref[rows, 2 * G:3 * G]

    def m_conv():
        for t in range(TS):
            ya = (convw_ref[0:1, :] * conv_in(t) + convw_ref[1:2, :] * conv_in(t + 1)
                  + convw_ref[2:3, :] * conv_in(t + 2))
            cat_ref[slab(t), 0:G] = (p_ref[slab(t), 0:G] * ya).astype(BF16)
        for j in range(SC_WIDTH - 1):
            oconv_ref[j] = conv_in(j + TS)

    def pool_in(j):
        if j < POOL_BUF:
            return opool_ref[j]
        return p_ref[slab(j - POOL_BUF), 3 * G:4 * G]

    def m_pool():
        for t in range(TS):
            idx = POOL_BUF + t
            run = pool_in(idx)
            sums = {}
            for j in range(1, POOL_BUF + 1):
                run = run + pool_in(idx - j)
                if j + 1 in POOL_WINDOWS:
                    sums[j + 1] = run
            pos = jnp.full((NS, G), start_pos + i * TS + t, jnp.int32)
            mean = _pool_select(sums, pos)
            pool_ref[slab(t), :] = mean - pool_in(idx)
        yb = jnp.dot(pool_ref[...].astype(BF16), poolbd_ref[...], preferred_element_type=F32)
        cat_ref[:, G:2 * G] = (yb * pscale_ref[...]).astype(BF16)
        for j in range(POOL_BUF):
            opool_ref[j] = pool_in(j + TS)

    def m_hgrn_gates():
        lower = _lower_bound(lb_ref[...], layer)
        q = _silu(p_ref[:, 4 * G:5 * G])
        f = lower + (1.0 - lower) * _sigmoid(p_ref[:, 5 * G:6 * G])
        for t in range(TS):
            qT_ref[t] = q[slab(t)].T
            fT_ref[t] = jnp.maximum(f[slab(t)], F_MIN).T
            kT_ref[t] = (1.0 - f[slab(t)]).T
            vT_ref[t] = p_ref[slab(t), 6 * G:7 * G].T

    def m_hgrn_scan():
        wait_at_first_step([3])
        for hh in range(HEADS):
            head = slice(hh * DK, (hh + 1) * DK)
            vts = [vT_ref[t, head, :] for t in range(TS)]

            def body(k, accs, hh=hh, vts=vts):
                c = hh * DK + k
                s = ohgrn_ref[c]
                out = []
                for t in range(TS):
                    s = fT_ref[t, pl.ds(c, 1), :] * s + kT_ref[t, pl.ds(c, 1), :] * vts[t]
                    out.append(accs[t] + qT_ref[t, pl.ds(c, 1), :] * s)
                ohgrn_ref[c] = s
                return tuple(out)

            accs = lax.fori_loop(0, DK, body,
                                 tuple(jnp.zeros((DK, NS), F32) for _ in range(TS)), unroll=2)
            for t in range(TS):
                oT_ref[t, head, :] = accs[t]

    def m_hgrn_out():
        o = jnp.concatenate([oT_ref[t].T for t in range(TS)], axis=0)
        yc = _head_norm_gate(o, p_ref[:, 7 * G:8 * G], hnorm_ref[...])
        cat_ref[:, 2 * G:3 * G] = yc.astype(BF16)

    def m_glu():
        u_ref[...] = p_ref[:, 8 * G:9 * G] * _sigmoid(p_ref[:, 9 * G:10 * G])

    HALF = NS // 2

    def conf_in(j, rows):
        if j < CONF_WIDTH - 1:
            return oconf_ref[j, rows, :]
        base = (j - (CONF_WIDTH - 1)) * NS
        return u_ref[base + rows.start:base + rows.stop, :]

    def m_conf():
        for t in range(TS):
            for hf in range(2):
                rows = slice(hf * HALF, (hf + 1) * HALF)
                acc = None
                for j in range(CONF_WIDTH):
                    term = cdw_ref[j:j + 1, :] * conf_in(t + j, rows)
                    acc = term if acc is None else acc + term
                yd = _conf_tail(acc, cb_ref[...], lng_ref[...], lnb_ref[...])
                cat_ref[t * NS + hf * HALF:t * NS + (hf + 1) * HALF, 3 * G:4 * G] = (
                    yd.astype(BF16))
        for j in range(CONF_WIDTH - 1):
            oconf_ref[j] = conf_in(j + TS, slice(0, NS))

    hm_ref[...] = _rmsnorm(x_ref[...], npre_ref[...]).astype(BF16)
    for blk in range(N_BLOCKS):
        proj(blk)
    m_conv()
    m_pool()
    m_hgrn_gates()
    m_hgrn_scan()
    m_hgrn_out()
    m_glu()
    m_conf()
    mix = jnp.dot(cat_ref[...], wout_ref[...], preferred_element_type=F32)
    y_ref[...] = x_ref[...] + _rmsnorm(mix, npost_ref[...])

    wait_at_first_step([4, 5, 6])
    hm_ref[...] = _rmsnorm(y_ref[...], fpre_ref[...]).astype(BF16)
    for j in range(D_FF // FFN_COLS):
        cols = slice(j * FFN_COLS, (j + 1) * FFN_COLS)
        hf = hm_ref[...]
        g = jnp.dot(hf, wg_ref[:, cols], preferred_element_type=F32)
        u = jnp.dot(hf, wu_ref[:, cols], preferred_element_type=F32)
        a_ref[:, cols] = (_silu(g) * u).astype(BF16)
    for k in range(D_MODEL // G):
        cols = slice(k * G, (k + 1) * G)
        ff_ref[:, cols] = jnp.dot(a_ref[...], wd_ref[:, cols], preferred_element_type=F32)
    y_ref[...] = y_ref[...] + _rmsnorm(ff_ref[...], fpost_ref[...])

    @pl.when(i == last)
    def _states_out():
        for k in last_step_copies:
            copy(k).start()
        for k in last_step_copies:
            copy(k).wait()


def _layer_spec(shape, layer, single_buffer=False):
    nd = len(shape)

    def imap(*_):
        return (layer,) + (0,) * nd

    if single_buffer:
        return pl.BlockSpec((None,) + tuple(shape), imap, pipeline_mode=pl.Buffered(1))
    return pl.BlockSpec((None,) + tuple(shape), imap)


def _mixer_weight_specs(layer):
    return [
        _layer_spec((1, D_MODEL), layer),
        _layer_spec((D_MODEL, D_IN), 0, True),
        _layer_spec((SC_WIDTH, G), layer),
        _layer_spec((G, G), layer),
        _layer_spec((1, G), layer),
        pl.BlockSpec((DEPTH, G), lambda *_: (0, 0)),
        _layer_spec((1, G), layer),
        _layer_spec((CONF_WIDTH, G), layer),
        _layer_spec((1, G), layer),
        _layer_spec((1, G), layer),
        _layer_spec((1, G), layer),
        _layer_spec((D_MODEL, D_MODEL), 0, True),
        _layer_spec((1, D_MODEL), layer),
    ]


def _ffn_weight_specs(layer):
    return [
        _layer_spec((1, D_MODEL), layer),
        _layer_spec((D_MODEL, D_FF), 0, True),
        _layer_spec((D_MODEL, D_FF), 0, True),
        _layer_spec((D_FF, D_MODEL), 0, True),
        _layer_spec((1, D_MODEL), layer),
    ]


def _layer_prompt(layer, x, mixer_wts, ffn_wts, next_f32):
    n, seq, _ = x.shape
    TT = PROMPT_TILE
    nt = seq // TT
    last = n * nt - 1
    n_steps = n * nt + 2

    def mix_tile(i):
        return jnp.minimum(i, last)

    def ffn_tile(i):
        return jnp.maximum(i - 2, 0)

    out_shape = [
        jax.ShapeDtypeStruct((n, seq, D_MODEL), F32),
        jax.ShapeDtypeStruct((n, SC_WIDTH - 1, G), F32),
        jax.ShapeDtypeStruct((n, POOL_BUF, G), F32),
        jax.ShapeDtypeStruct((n, HEADS, DK, DK), F32),
        jax.ShapeDtypeStruct((n, CONF_WIDTH - 1, G), F32),
    ]
    out_specs = [
        pl.BlockSpec((None, TT, D_MODEL), lambda i: (ffn_tile(i) // nt, ffn_tile(i) % nt, 0)),
        pl.BlockSpec((None, SC_WIDTH - 1, G), lambda i: (mix_tile(i) // nt, 0, 0)),
        pl.BlockSpec((None, POOL_BUF, G), lambda i: (mix_tile(i) // nt, 0, 0)),
        pl.BlockSpec((None, HEADS, DK, DK), lambda i: (mix_tile(i) // nt, 0, 0, 0)),
        pl.BlockSpec((None, CONF_WIDTH - 1, G), lambda i: (mix_tile(i) // nt, 0, 0)),
    ]
    cast_in_specs = []
    for w in next_f32:
        _, rows, cols = w.shape
        blk = next(b for b in range(CAST_ROWS, rows + 1, CAST_ROWS)
                   if rows % b == 0 and rows // b <= n_steps)
        n_blk = rows // blk

        def in_map(i, n_blk=n_blk):
            return (layer + 1, jnp.minimum(i, n_blk - 1), 0)

        def out_map(i, n_blk=n_blk):
            return (0, jnp.minimum(i, n_blk - 1), 0)

        cast_in_specs.append(pl.BlockSpec((None, blk, cols), in_map))
        out_specs.append(pl.BlockSpec((None, blk, cols), out_map))
        out_shape.append(jax.ShapeDtypeStruct((1, rows, cols), BF16))
    scratch = [
        pltpu.VMEM((TT, D_IN), F32),
        pltpu.VMEM((CONV_PAD + TT, G), F32),
        pltpu.VMEM((POOL_PAD + TT, G), F32),
        pltpu.VMEM((CONF_PAD + TT, G), F32),
        pltpu.VMEM((SUBLANES - 1, CONF_PAD + TT, G), F32),
        pltpu.VMEM((G, G), F32),
        pltpu.VMEM((TT, G), F32),
        pltpu.VMEM((TT, G), F32),
        pltpu.VMEM((TT, G), F32),
        pltpu.VMEM((TT, G), F32),
        pltpu.VMEM((TT, G), F32),
        pltpu.VMEM((TT, D_MODEL), BF16),
        pltpu.VMEM((3, TT, D_MODEL), F32),
        pltpu.VMEM((TT, D_MODEL), BF16),
        pltpu.VMEM((TT, D_MODEL), BF16),
        pltpu.VMEM((2, TT, D_FF), BF16),
        pltpu.VMEM((TT, D_MODEL), F32),
        pltpu.VMEM((TT, D_MODEL), F32),
    ]
    return pl.pallas_call(
        functools.partial(_layer_prompt_kernel, layer, TT, nt, n * nt, len(next_f32)),
        grid=(n_steps,),
        in_specs=[pl.BlockSpec((None, TT, D_MODEL),
                               lambda i: (mix_tile(i) // nt, mix_tile(i) % nt, 0))]
        + _mixer_weight_specs(layer) + _ffn_weight_specs(layer) + cast_in_specs,
        out_specs=tuple(out_specs),
        out_shape=tuple(out_shape),
        scratch_shapes=scratch,
        compiler_params=pltpu.CompilerParams(
            dimension_semantics=("arbitrary",), vmem_limit_bytes=VMEM_LIMIT),
        name=f"layer_prompt_l{layer}",
    )(x, *mixer_wts, *ffn_wts, *next_f32)


def _layer_sample(layer, x2d, s_conv, s_pool, s_hgrn, s_conf, mixer_wts, ffn_wts, n_seq,
                  new_states):
    m = x2d.shape[0]
    NS = n_seq
    TS = SAMPLE_STEPS
    M = TS * NS
    n_alias = len(new_states)

    in_hbm = pl.BlockSpec(memory_space=pl.ANY)
    state_shapes = [(SC_WIDTH - 1, NS, G), (POOL_BUF, NS, G), (G, DK, NS), (CONF_WIDTH - 1, NS, G)]
    in_specs = ([pl.BlockSpec((M, D_MODEL), lambda i: (i, 0))]
                + [in_hbm] * len(state_shapes) + _mixer_weight_specs(layer)
                + [_layer_spec((1, D_MODEL), layer), in_hbm, in_hbm, in_hbm,
                   _layer_spec((1, D_MODEL), layer)]
                + [in_hbm] * n_alias)
    n_in = len(in_specs) - n_alias
    out_specs = tuple([pl.BlockSpec((M, D_MODEL), lambda i: (i, 0))]
                      + [in_hbm] * len(state_shapes))
    out_shape = tuple([jax.ShapeDtypeStruct((m, D_MODEL), F32)]
                      + [jax.ShapeDtypeStruct((DEPTH,) + s, F32) for s in state_shapes])
    n_copies = N_FIRST_STEP_COPIES + 4 + (0 if n_alias else 4 * (DEPTH - 1))
    scratch = [
        pltpu.VMEM(state_shapes[0], F32),
        pltpu.VMEM(state_shapes[1], F32),
        pltpu.VMEM(state_shapes[2], F32),
        pltpu.VMEM(state_shapes[3], F32),
        pltpu.VMEM((M, D_IN), F32),
        pltpu.VMEM((M, G), F32),
        pltpu.VMEM((M, G), F32),
        pltpu.VMEM((TS, G, NS), F32),
        pltpu.VMEM((TS, G, NS), F32),
        pltpu.VMEM((TS, G, NS), F32),
        pltpu.VMEM((TS, G, NS), F32),
        pltpu.VMEM((TS, G, NS), F32),
        pltpu.VMEM((M, D_MODEL), BF16),
        pltpu.VMEM((M, D_MODEL), BF16),
        pltpu.VMEM((D_MODEL, D_FF), BF16),
        pltpu.VMEM((D_MODEL, D_FF), BF16),
        pltpu.VMEM((D_FF, D_MODEL), BF16),
        pltpu.VMEM((M, D_FF), BF16),
        pltpu.VMEM((M, D_MODEL), F32),
        pltpu.SemaphoreType.DMA((n_copies,)),
    ]
    return pl.pallas_call(
        functools.partial(_layer_sample_kernel, layer, NS, TS, PAST_LEN, n_alias),
        grid=(m // M,),
        in_specs=in_specs,
        out_specs=out_specs,
        out_shape=out_shape,
        scratch_shapes=scratch,
        input_output_aliases={n_in + k: 1 + k for k in range(n_alias)},
        compiler_params=pltpu.CompilerParams(
            dimension_semantics=("arbitrary",), vmem_limit_bytes=VMEM_LIMIT),
        name=f"layer_sample_l{layer}",
    )(x2d, s_conv, s_pool, s_hgrn, s_conf, *mixer_wts, *ffn_wts, *new_states)


def kernel(x_prompt, x_sample, state_conv, state_pool, state_hgrn, state_conf, norm_mix_pre, norm_mix_post, w_in, conv_w, pool_w, pool_scale, hgrn_lb, hgrn_norm, conf_dw, conf_b, conf_ln_g, conf_ln_b, w_out, norm_ffn_pre, norm_ffn_post, w_gate, w_up, w_down):
    def row(a):
        return a.reshape(DEPTH, 1, a.shape[-1])

    eye = jnp.eye(G // POOL_CH, dtype=pool_w.dtype)
    pool_bd = (pool_w[:, :, :, None, :] * eye[None, :, None, :, None]).reshape(DEPTH, G, G)
    pool_bd = pool_bd.astype(BF16)
    big_f32 = (w_in, w_out, w_gate, w_up, w_down)
    big = tuple(w[0:1].astype(BF16) for w in big_f32)

    def mixer_weights(big):
        return (row(norm_mix_pre), big[0], conv_w, pool_bd, row(pool_scale), hgrn_lb,
                row(hgrn_norm), conf_dw, row(conf_b), row(conf_ln_g), row(conf_ln_b), big[1],
                row(norm_mix_post))

    def ffn_weights(big):
        return (row(norm_ffn_pre), big[2], big[3], big[4], row(norm_ffn_post))

    ns, ts, _ = x_sample.shape
    xs = x_sample.transpose(1, 0, 2).reshape(ts * ns, D_MODEL)
    sc_t = state_conv.transpose(1, 2, 0, 3)
    sp_t = state_pool.transpose(1, 2, 0, 3)
    sf_t = state_conf.transpose(1, 2, 0, 3)
    sh_t = state_hgrn.transpose(1, 2, 3, 4, 0).reshape(DEPTH, G, DK, ns)
    xp = x_prompt
    p_states, s_states = [], []
    for layer in range(DEPTH):
        mixer_wts, ffn_wts = mixer_weights(big), ffn_weights(big)
        next_f32 = big_f32 if layer + 1 < DEPTH else ()
        xp, *rest = _layer_prompt(layer, xp, mixer_wts, ffn_wts, next_f32)
        p_states.append(rest[:4])
        big = tuple(rest[4:])
        xs, *s_states = _layer_sample(layer, xs, sc_t, sp_t, sh_t, sf_t, mixer_wts, ffn_wts, ns,
                                      s_states)
    xs = xs.reshape(ts, ns, D_MODEL).transpose(1, 0, 2)

    def stack(states, i):
        return jnp.stack([states[layer][i] for layer in range(DEPTH)], axis=1)

    def seq_major(i):
        return s_states[i].transpose(2, 0, 1, 3)

    hgrn_s = s_states[2].reshape(DEPTH, HEADS, DK, DK, ns).transpose(4, 0, 1, 2, 3)

    return (xp, xs,
            stack(p_states, 0), stack(p_states, 1), stack(p_states, 2), stack(p_states, 3),
            seq_major(0), seq_major(1), hgrn_s, seq_major(3))
```

```python
import functools

import jax
import jax.numpy as jnp
from jax import lax
from jax.experimental import pallas as pl
from jax.experimental.pallas import tpu as pltpu

F32 = jnp.float32
BF16 = jnp.bfloat16

D_MODEL = 1024
DEPTH = 2
PAST_LEN = 16384
G = 256
N_BLOCKS = 10
D_IN = N_BLOCKS * G
SC_WIDTH = 3
POOL_WINDOWS = (2, 4, 8, 16)
POOL_BUF = max(POOL_WINDOWS) - 1
POOL_CH = 64
HEADS = 4
DK = 64
CONF_WIDTH = 31
D_FF = 2816
EPS = 1e-6
F_MIN = 1e-20

SUBLANES = 8


def _round_up(n, m):
    return -(-n // m) * m


CONV_PAD = _round_up(SC_WIDTH - 1, SUBLANES)
POOL_PAD = _round_up(POOL_BUF, SUBLANES)
CONF_PAD = _round_up(CONF_WIDTH - 1, SUBLANES)

HGRN_CHUNK = 64
FAST_DECAY_LIMIT = 60.0
PROMPT_TILE = 256
SAMPLE_STEPS = 2
FFN_COLS = 256
N_FIRST_STEP_COPIES = 7
CAST_ROWS = 16
CAST_STEPS = 8
CONV_ROWS = 64
VMEM_LIMIT = 56 * 1024 * 1024


def _sigmoid(x):
    return jax.nn.sigmoid(x)


def _silu(x):
    return x * jax.nn.sigmoid(x)


def _rmsnorm(x, g):
    ms = jnp.mean(x * x, axis=-1, keepdims=True)
    return x * lax.rsqrt(ms + EPS) * g


def _head_block_mask(rows, cols, row_block, col_block):
    r = lax.broadcasted_iota(jnp.int32, (rows, cols), 0) // row_block
    c = lax.broadcasted_iota(jnp.int32, (rows, cols), 1) // col_block
    return r == c


def _cumsum_rows_mxu(x):
    n = x.shape[0]
    tri = (lax.broadcasted_iota(jnp.int32, (n, n), 0)
           >= lax.broadcasted_iota(jnp.int32, (n, n), 1))
    tri = jnp.where(tri, 1.0, 0.0).astype(BF16)
    hi = x.astype(BF16)
    r1 = x - hi.astype(F32)
    mid = r1.astype(BF16)
    lo = (r1 - mid.astype(F32)).astype(BF16)
    return (jnp.dot(tri, hi, preferred_element_type=F32)
            + jnp.dot(tri, mid, preferred_element_type=F32)
            + jnp.dot(tri, lo, preferred_element_type=F32))


def _lower_bound(lb_all, layer):
    m = jnp.max(lb_all, axis=0, keepdims=True)
    e = jnp.exp(lb_all - m)
    sm = e / jnp.sum(e, axis=0, keepdims=True)
    cs = sm[0:1]
    for i in range(1, layer + 1):
        cs = cs + sm[i:i + 1]
    return cs - sm[0:1]


def _hgrn_gates(zq, zf, zi, lower):
    q = _silu(zq)
    f = lower + (1.0 - lower) * _sigmoid(zf)
    logf = jnp.log(jnp.maximum(f, F_MIN))
    return q, 1.0 - f, zi, logf


def _hgrn_state_terms(q, kk, v, b, st_ref):
    TT = q.shape[0]
    b_end = b[TT - 1:TT, :]
    st = st_ref[...]
    qs = (q * jnp.exp(b)).astype(BF16)
    o_inter = lax.dot_general(qs, st.astype(BF16), (((1,), (1,)), ((), ())),
                              preferred_element_type=F32)
    kh = (kk * jnp.exp(b_end - b)).astype(BF16)
    upd = lax.dot_general(v.astype(BF16), kh, (((0,), (0,)), ((), ())),
                          preferred_element_type=F32)
    bd = _head_block_mask(G, G, DK, DK)
    st_ref[...] = st * jnp.exp(b_end) + jnp.where(bd, upd, 0.0)
    return o_inter


def _hgrn_refs(b, C):
    refs = []
    span = None
    for j in range(b.shape[0] // C):
        first = b[j * C:j * C + 1, :]
        last = b[(j + 1) * C - 1:(j + 1) * C, :]
        refs.append(0.5 * (first + last))
        half = jnp.max(0.5 * (first - last))
        span = half if span is None else jnp.maximum(span, half)
    return refs, span


def _hgrn_fast_chunk(q_tgt, b_tgt, kk_src, v_src, r, C):
    nt = q_tgt.shape[0]
    rows_mask = _head_block_mask(HEADS * C, G, C, DK)
    qz = (q_tgt * jnp.exp(b_tgt - r)).astype(BF16)
    ke = kk_src * jnp.exp(r - b_tgt[0:C])
    kebd = jnp.where(rows_mask, jnp.concatenate([ke] * HEADS, axis=0), 0.0).astype(BF16)
    attn = lax.dot_general(qz, kebd, (((1,), (1,)), ((), ())),
                           preferred_element_type=F32)
    t_idx = lax.broadcasted_iota(jnp.int32, (nt, HEADS * C), 0)
    s_idx = lax.broadcasted_iota(jnp.int32, (nt, HEADS * C), 1) % C
    attn = jnp.where(t_idx >= s_idx, attn, 0.0).astype(BF16)
    vbd = jnp.where(rows_mask, jnp.concatenate([v_src] * HEADS, axis=0), 0.0).astype(BF16)
    return jnp.dot(attn, vbd, preferred_element_type=F32)


def _hgrn_exact_attn(hb_ref, hq_ref, kk, v, o_ref, row0, TT):
    b = hb_ref[0:TT, :]
    ones_bd = jnp.where(_head_block_mask(G, G, DK, DK), 1.0, 0.0).astype(BF16)
    s_row = lax.broadcasted_iota(jnp.int32, (TT, G), 0)

    def body(t, carry):
        bt = hb_ref[pl.ds(t, 1), :]
        qt = hq_ref[pl.ds(t, 1), :]
        e = jnp.where(s_row <= t, qt * kk * jnp.exp(jnp.minimum(bt - b, 0.0)), 0.0)
        a = jnp.dot(e.astype(BF16), ones_bd, preferred_element_type=F32)
        o_ref[pl.ds(row0 + t, 1), :] = jnp.sum(a * v, axis=0, keepdims=True)
        return carry

    lax.fori_loop(0, TT, body, 0)


def _head_norm_gate(o, zg, hnorm):
    ones_bd = jnp.where(_head_block_mask(G, G, DK, DK), 1.0, 0.0).astype(BF16)
    o2 = o * o
    hi = o2.astype(BF16)
    lo = (o2 - hi.astype(F32)).astype(BF16)
    ssq = (jnp.dot(hi, ones_bd, preferred_element_type=F32)
           + jnp.dot(lo, ones_bd, preferred_element_type=F32))
    return o * lax.rsqrt(ssq * (1.0 / DK) + EPS) * hnorm * _silu(zg)


def _pool_select(sums, pos):
    shape = sums[POOL_WINDOWS[0]].shape
    grp = lax.broadcasted_iota(jnp.int32, shape, len(shape) - 1) // POOL_CH
    ssum = sums[POOL_WINDOWS[-1]]
    win = jnp.full(shape, POOL_WINDOWS[-1], jnp.int32)
    for gi in range(len(POOL_WINDOWS) - 2, -1, -1):
        ssum = jnp.where(grp == gi, sums[POOL_WINDOWS[gi]], ssum)
        win = jnp.where(grp == gi, POOL_WINDOWS[gi], win)
    cnt = jnp.minimum(pos + 1, win).astype(F32)
    return ssum / cnt


def _conf_tail(z, cb, lng, lnb):
    z = z + cb
    mu = jnp.mean(z, axis=-1, keepdims=True)
    zc = z - mu
    var = jnp.mean(zc * zc, axis=-1, keepdims=True)
    return _silu(zc * lax.rsqrt(var + EPS) * lng + lnb)


def _layer_prompt_kernel(layer, TT, nt, n_tiles, n_cast, *refs):
    (x_ref, npre_ref, win_ref, convw_ref, poolbd_ref, pscale_ref, lb_ref,
     hnorm_ref, cdw_ref, cb_ref, lng_ref, lnb_ref, wout_ref, npost_ref,
     fpre_ref, wg_ref, wu_ref, wd_ref, fpost_ref) = refs[:19]
    cast_in = refs[19:19 + n_cast]
    y_ref, oconv_ref, opool_ref, ohgrn_ref, oconf_ref = refs[19 + n_cast:24 + n_cast]
    cast_out = refs[24 + n_cast:24 + 2 * n_cast]
    (p_ref, ea_ref, eb_ref, ed_ref, sh_ref, st_ref, hb_ref, hq_ref, hk_ref, oi_ref,
     o_ref, cat_ref, x1_ref, hm_ref, hf_ref, a_ref, ff_ref, mix_ref) = refs[24 + 2 * n_cast:]
    i = pl.program_id(0)
    t = i % nt
    slot = i % 3

    @pl.when(i == 0)
    def _first():
        x1_ref[1] = jnp.zeros((TT, D_MODEL), F32)
        x1_ref[2] = jnp.zeros((TT, D_MODEL), F32)
        a_ref[0] = jnp.zeros((TT, D_FF), BF16)

    @pl.when((t == 0) & (i < n_tiles))
    def _new_sequence():
        ea_ref[0:CONV_PAD, :] = jnp.zeros((CONV_PAD, G), F32)
        eb_ref[0:POOL_PAD, :] = jnp.zeros((POOL_PAD, G), F32)
        ed_ref[0:CONF_PAD, :] = jnp.zeros((CONF_PAD, G), F32)
        st_ref[...] = jnp.zeros((G, G), F32)

    lower = _lower_bound(lb_ref[...], layer)
    n_chunks = TT // HGRN_CHUNK
    slot_up = (i + 2) % 3
    slot_down = (i + 1) % 3
    a_new = (i + 1) % 2
    a_old = i % 2

    def f_norm():
        hf_ref[...] = _rmsnorm(x1_ref[slot_up], fpre_ref[...]).astype(BF16)

    def f_gate_up(j):
        cols = slice(j * FFN_COLS, (j + 1) * FFN_COLS)
        hf = hf_ref[...]
        g = jnp.dot(hf, wg_ref[:, cols], preferred_element_type=F32)
        u = jnp.dot(hf, wu_ref[:, cols], preferred_element_type=F32)
        a_ref[a_new, :, cols] = (_silu(g) * u).astype(BF16)

    def f_down(k):
        cols = slice(k * G, (k + 1) * G)
        ff_ref[:, cols] = jnp.dot(a_ref[a_old], wd_ref[:, cols], preferred_element_type=F32)

    def f_out():
        y_ref[...] = x1_ref[slot_down] + _rmsnorm(ff_ref[...], fpost_ref[...])

    def cast_next():
        for src, dst in zip(cast_in, cast_out):
            dst[...] = src[...].astype(BF16)

    def m_norm():
        hm_ref[...] = _rmsnorm(x_ref[...], npre_ref[...]).astype(BF16)

    def m_proj(blk):
        cols = slice(blk * G, (blk + 1) * G)
        p_ref[:, cols] = jnp.dot(hm_ref[...], win_ref[:, cols], preferred_element_type=F32)

    def m_conv():
        cu = p_ref[:, G:2 * G] * p_ref[:, 2 * G:3 * G]
        ea_ref[CONV_PAD:CONV_PAD + TT, :] = cu
        ya = convw_ref[SC_WIDTH - 1:SC_WIDTH, :] * cu
        for back in range(1, SC_WIDTH):
            w = convw_ref[SC_WIDTH - 1 - back:SC_WIDTH - back, :]
            ya = ya + w * ea_ref[CONV_PAD - back:CONV_PAD - back + TT, :]
        cat_ref[:, 0:G] = (p_ref[:, 0:G] * ya).astype(BF16)
        oconv_ref[...] = ea_ref[TT + CONV_PAD - (SC_WIDTH - 1):TT + CONV_PAD, :]
        ea_ref[0:CONV_PAD, :] = ea_ref[TT:TT + CONV_PAD, :]

    def m_pool():
        pp = p_ref[:, 3 * G:4 * G]
        eb_ref[POOL_PAD:POOL_PAD + TT, :] = pp
        run = eb_ref[...]
        sums = {}
        w = 1
        while w < POOL_WINDOWS[-1]:
            run = run + pltpu.roll(run, w, 0)
            w *= 2
            sums[w] = run[POOL_PAD:]
        pos = t * TT + lax.broadcasted_iota(jnp.int32, (TT, G), 0)
        mean = _pool_select(sums, pos)
        yb = jnp.dot((mean - pp).astype(BF16), poolbd_ref[...], preferred_element_type=F32)
        cat_ref[:, G:2 * G] = (yb * pscale_ref[...]).astype(BF16)
        opool_ref[...] = eb_ref[TT + POOL_PAD - POOL_BUF:TT + POOL_PAD, :]
        eb_ref[0:POOL_PAD, :] = eb_ref[TT:TT + POOL_PAD, :]

    hg = {}

    def m_hgrn_gates():
        q, kk, _, logf = _hgrn_gates(p_ref[:, 4 * G:5 * G], p_ref[:, 5 * G:6 * G],
                                     p_ref[:, 6 * G:7 * G], lower)
        hq_ref[...] = q
        hk_ref[...] = kk
        hb_ref[...] = _cumsum_rows_mxu(logf)

    def m_hgrn_state():
        b = hb_ref[...]
        o_inter = _hgrn_state_terms(hq_ref[...], hk_ref[...], p_ref[:, 6 * G:7 * G], b, st_ref)
        oi_ref[...] = o_inter
        o_ref[...] = o_inter
        hg["refs"], hg["span"] = _hgrn_refs(b, HGRN_CHUNK)

    def m_hgrn_chunk(j):
        lo, hi = j * HGRN_CHUNK, (j + 1) * HGRN_CHUNK
        contrib = _hgrn_fast_chunk(hq_ref[lo:TT, :], hb_ref[lo:TT, :], hk_ref[lo:hi, :],
                                   p_ref[lo:hi, 6 * G:7 * G], hg["refs"][j], HGRN_CHUNK)
        o_ref[lo:TT, :] = o_ref[lo:TT, :] + contrib

    def m_hgrn_out():
        yc = _head_norm_gate(o_ref[...], p_ref[:, 7 * G:8 * G], hnorm_ref[...])
        cat_ref[:, 2 * G:3 * G] = yc.astype(BF16)

    def m_glu():
        ed_ref[CONF_PAD:CONF_PAD + TT, :] = (p_ref[:, 8 * G:9 * G]
                                             * _sigmoid(p_ref[:, 9 * G:10 * G]))
        ed = ed_ref[...]
        for r in range(1, SUBLANES):
            sh_ref[r - 1] = pltpu.roll(ed, TT + CONF_PAD - r, 0)

    def m_conf(rb):
        base = rb * CONV_ROWS
        first = CONF_PAD - (CONF_WIDTH - 1)
        acc = None
        for j in range(CONF_WIDTH):
            tiles, r = divmod(first + j, SUBLANES)
            lo = base + SUBLANES * tiles
            src = ed_ref[lo:lo + CONV_ROWS, :] if r == 0 else sh_ref[r - 1, lo:lo + CONV_ROWS, :]
            term = cdw_ref[j:j + 1, :] * src
            acc = term if acc is None else acc + term
        yd = _conf_tail(acc, cb_ref[...], lng_ref[...], lnb_ref[...])
        cat_ref[base:base + CONV_ROWS, 3 * G:4 * G] = yd.astype(BF16)

    def m_conf_tail():
        oconf_ref[...] = ed_ref[TT + CONF_PAD - (CONF_WIDTH - 1):TT + CONF_PAD, :]
        ed_ref[0:CONF_PAD, :] = ed_ref[TT:TT + CONF_PAD, :]

    def m_out():
        mix_ref[...] = jnp.dot(cat_ref[...], wout_ref[...], preferred_element_type=F32)

    def m_out_norm():
        x1_ref[slot] = x_ref[...] + _rmsnorm(mix_ref[...], npost_ref[...])

    n_gu = D_FF // FFN_COLS
    gate_up = [functools.partial(f_gate_up, j) for j in range(n_gu)]
    proj = [functools.partial(m_proj, blk) for blk in range(N_BLOCKS)]
    down = [functools.partial(f_down, k) for k in range(D_MODEL // G)]
    conf_all = [functools.partial(m_conf, rb) for rb in range(TT // CONV_ROWS)]
    per = len(conf_all) // 4
    conf = [conf_all[k * per:(k + 1) * per] for k in range(4)]
    chunk = [functools.partial(m_hgrn_chunk, j) for j in range(n_chunks)]
    g = gate_up
    schedule = [
        down[0], f_norm, down[1], m_norm, down[2], cast_next, down[3],
        proj[8], proj[9], f_out,
        g[0], m_glu, proj[3],
        g[1], *conf[0], proj[0], proj[1], proj[2],
        g[2], *conf[1], proj[4], proj[5], proj[6],
        g[3], *conf[2], proj[7],
        g[4], *conf[3], m_conf_tail,
        g[5], m_conv, m_pool,
        g[6], m_hgrn_gates,
        g[7], m_hgrn_state,
        g[8], chunk[0], chunk[1],
        g[9], chunk[2], chunk[3],
        m_hgrn_out, m_out, g[10], m_out_norm,
    ]
    assert n_gu == 11 and n_chunks == 4 and len(conf_all) == 4 * per and len(down) == 4
    for piece in schedule:
        piece()

    @pl.when(hg["span"] >= FAST_DECAY_LIMIT)
    def _redo_exact():
        f = lower + (1.0 - lower) * _sigmoid(p_ref[:, 5 * G:6 * G])
        _hgrn_exact_attn(hb_ref, hq_ref, 1.0 - f, p_ref[:, 6 * G:7 * G], o_ref, 0, TT)
        yce = _head_norm_gate(o_ref[...] + oi_ref[...], p_ref[:, 7 * G:8 * G], hnorm_ref[...])
        cat_ref[:, 2 * G:3 * G] = yce.astype(BF16)
        mixe = jnp.dot(cat_ref[...], wout_ref[...], preferred_element_type=F32)
        x1_ref[slot] = x_ref[...] + _rmsnorm(mixe, npost_ref[...])

    @pl.when((t == nt - 1) & (i < n_tiles))
    def _state_out():
        s = st_ref[...].T
        for hh in range(HEADS):
            ohgrn_ref[hh] = s[hh * DK:(hh + 1) * DK, hh * DK:(hh + 1) * DK]


def _layer_sample_kernel(layer, NS, TS, start_pos,
                         x_ref, sconv_hbm, spool_hbm, shgrn_hbm, sconf_hbm,
                         npre_ref, win_ref, convw_ref, poolbd_ref, pscale_ref, lb_ref,
                         hnorm_ref, cdw_ref, cb_ref, lng_ref, lnb_ref, wout_ref, npost_ref,
                         fpre_ref, wg_hbm, wu_hbm, wd_hbm, fpost_ref,
                         y_ref, oconv_ref, opool_ref, ohgrn_ref, oconf_ref,
                         p_ref, u_ref, pool_ref, qT_ref, fT_ref, kT_ref, vT_ref, oT_ref, cat_ref,
                         hm_ref, wg_ref, wu_ref, wd_ref, a_ref, ff_ref, sems):
    i = pl.program_id(0)
    copies = [
        (sconv_hbm.at[layer], oconv_ref), (spool_hbm.at[layer], opool_ref),
        (sconf_hbm.at[layer], oconf_ref), (shgrn_hbm.at[layer], ohgrn_ref),
        (wg_hbm.at[0], wg_ref), (wu_hbm.at[0], wu_ref), (wd_hbm.at[0], wd_ref),
    ]

    def copy(k):
        return pltpu.make_async_copy(copies[k][0], copies[k][1], sems.at[k])

    def wait_at_first_step(ks):
        @pl.when(i == 0)
        def _wait():
            for k in ks:
                copy(k).wait()

    @pl.when(i == 0)
    def _start_copies():
        for k in range(len(copies)):
            copy(k).start()

    wait_at_first_step([0, 1, 2])

    def slab(t):
        return slice(t * NS, (t + 1) * NS)

    def proj(blk):
        cols = slice(blk * G, (blk + 1) * G)
        p_ref[:, cols] = jnp.dot(hm_ref[...], win_ref[:, cols], preferred_element_type=F32)

    def conv_in(j):
        if j < SC_WIDTH - 1:
            return oconv_ref[j]
        rows = slab(j - (SC_WIDTH - 1))
        return p_ref[rows, G:2 * G] * p_ref[rows, 2 * G:3 * G]

    def m_conv():
        for t in range(TS):
            ya = (convw_ref[0:1, :] * conv_in(t) + convw_ref[1:2, :] * conv_in(t + 1)
                  + convw_ref[2:3, :] * conv_in(t + 2))
            cat_ref[slab(t), 0:G] = (p_ref[slab(t), 0:G] * ya).astype(BF16)
        for j in range(SC_WIDTH - 1):
            oconv_ref[j] = conv_in(j + TS)

    def pool_in(j):
        if j < POOL_BUF:
            return opool_ref[j]
        return p_ref[slab(j - POOL_BUF), 3 * G:4 * G]

    def m_pool():
        for t in range(TS):
            idx = POOL_BUF + t
            run = pool_in(idx)
            sums = {}
            for j in range(1, POOL_BUF + 1):
                run = run + pool_in(idx - j)
                if j + 1 in POOL_WINDOWS:
                    sums[j + 1] = run
            pos = jnp.full((NS, G), start_pos + i * TS + t, jnp.int32)
            mean = _pool_select(sums, pos)
            pool_ref[slab(t), :] = mean - pool_in(idx)
        yb = jnp.dot(pool_ref[...].astype(BF16), poolbd_ref[...], preferred_element_type=F32)
        cat_ref[:, G:2 * G] = (yb * pscale_ref[...]).astype(BF16)
        for j in range(POOL_BUF):
            opool_ref[j] = pool_in(j + TS)

    def m_hgrn_gates():
        lower = _lower_bound(lb_ref[...], layer)
        q = _silu(p_ref[:, 4 * G:5 * G])
        f = lower + (1.0 - lower) * _sigmoid(p_ref[:, 5 * G:6 * G])
        for t in range(TS):
            qT_ref[t] = q[slab(t)].T
            fT_ref[t] = jnp.maximum(f[slab(t)], F_MIN).T
            kT_ref[t] = (1.0 - f[slab(t)]).T
            vT_ref[t] = p_ref[slab(t), 6 * G:7 * G].T

    def m_hgrn_scan():
        wait_at_first_step([3])
        for hh in range(HEADS):
            head = slice(hh * DK, (hh + 1) * DK)
            vts = [vT_ref[t, head, :] for t in range(TS)]

            def body(k, accs, hh=hh, vts=vts):
                c = hh * DK + k
                s = ohgrn_ref[c]
                out = []
                for t in range(TS):
                    s = fT_ref[t, pl.ds(c, 1), :] * s + kT_ref[t, pl.ds(c, 1), :] * vts[t]
                    out.append(accs[t] + qT_ref[t, pl.ds(c, 1), :] * s)
                ohgrn_ref[c] = s
                return tuple(out)

            accs = lax.fori_loop(0, DK, body,
                                 tuple(jnp.zeros((DK, NS), F32) for _ in range(TS)), unroll=2)
            for t in range(TS):
                oT_ref[t, head, :] = accs[t]

    def m_hgrn_out():
        o = jnp.concatenate([oT_ref[t].T for t in range(TS)], axis=0)
        yc = _head_norm_gate(o, p_ref[:, 7 * G:8 * G], hnorm_ref[...])
        cat_ref[:, 2 * G:3 * G] = yc.astype(BF16)

    def m_glu():
        u_ref[...] = p_ref[:, 8 * G:9 * G] * _sigmoid(p_ref[:, 9 * G:10 * G])

    HALF = NS // 2

    def conf_in(j, rows):
        if j < CONF_WIDTH - 1:
            return oconf_ref[j, rows, :]
        base = (j - (CONF_WIDTH - 1)) * NS
        return u_ref[base + rows.start:base + rows.stop, :]

    def m_conf():
        for t in range(TS):
            for hf in range(2):
                rows = slice(hf * HALF, (hf + 1) * HALF)
                acc = None
                for j in range(CONF_WIDTH):
                    term = cdw_ref[j:j + 1, :] * conf_in(t + j, rows)
                    acc = term if acc is None else acc + term
                yd = _conf_tail(acc, cb_ref[...], lng_ref[...], lnb_ref[...])
                cat_ref[t * NS + hf * HALF:t * NS + (hf + 1) * HALF, 3 * G:4 * G] = (
                    yd.astype(BF16))
        for j in range(CONF_WIDTH - 1):
            oconf_ref[j] = conf_in(j + TS, slice(0, NS))

    hm_ref[...] = _rmsnorm(x_ref[...], npre_ref[...]).astype(BF16)
    for blk in range(N_BLOCKS):
        proj(blk)
    m_conv()
    m_pool()
    m_hgrn_gates()
    m_hgrn_scan()
    m_hgrn_out()
    m_glu()
    m_conf()
    mix = jnp.dot(cat_ref[...], wout_ref[...], preferred_element_type=F32)
    y_ref[...] = x_ref[...] + _rmsnorm(mix, npost_ref[...])

    wait_at_first_step([4, 5, 6])
    hm_ref[...] = _rmsnorm(y_ref[...], fpre_ref[...]).astype(BF16)
    for j in range(D_FF // FFN_COLS):
        cols = slice(j * FFN_COLS, (j + 1) * FFN_COLS)
        hf = hm_ref[...]
        g = jnp.dot(hf, wg_ref[:, cols], preferred_element_type=F32)
        u = jnp.dot(hf, wu_ref[:, cols], preferred_element_type=F32)
        a_ref[:, cols] = (_silu(g) * u).astype(BF16)
    for k in range(D_MODEL // G):
        cols = slice(k * G, (k + 1) * G)
        ff_ref[:, cols] = jnp.dot(a_ref[...], wd_ref[:, cols], preferred_element_type=F32)
    y_ref[...] = y_ref[...] + _rmsnorm(ff_ref[...], fpost_ref[...])


def _layer_spec(shape, layer, single_buffer=False):
    nd = len(shape)

    def imap(*_):
        return (layer,) + (0,) * nd

    if single_buffer:
        return pl.BlockSpec((None,) + tuple(shape), imap, pipeline_mode=pl.Buffered(1))
    return pl.BlockSpec((None,) + tuple(shape), imap)


def _mixer_weight_specs(layer):
    return [
        _layer_spec((1, D_MODEL), layer),
        _layer_spec((D_MODEL, D_IN), 0, True),
        _layer_spec((SC_WIDTH, G), layer),
        _layer_spec((G, G), layer),
        _layer_spec((1, G), layer),
        pl.BlockSpec((DEPTH, G), lambda *_: (0, 0)),
        _layer_spec((1, G), layer),
        _layer_spec((CONF_WIDTH, G), layer),
        _layer_spec((1, G), layer),
        _layer_spec((1, G), layer),
        _layer_spec((1, G), layer),
        _layer_spec((D_MODEL, D_MODEL), 0, True),
        _layer_spec((1, D_MODEL), layer),
    ]


def _ffn_weight_specs(layer):
    return [
        _layer_spec((1, D_MODEL), layer),
        _layer_spec((D_MODEL, D_FF), 0, True),
        _layer_spec((D_MODEL, D_FF), 0, True),
        _layer_spec((D_FF, D_MODEL), 0, True),
        _layer_spec((1, D_MODEL), layer),
    ]


def _cast_kernel(*refs):
    n = len(refs) // 2
    for src, dst in zip(refs[:n], refs[n:]):
        dst[...] = src[...].astype(BF16)


def _cast_first_layer(weights_f32):
    in_specs, out_specs, out_shape = [], [], []
    for w in weights_f32:
        _, rows, cols = w.shape
        blk = rows // CAST_STEPS
        assert rows % CAST_STEPS == 0 and blk % CAST_ROWS == 0
        in_specs.append(pl.BlockSpec((None, blk, cols), lambda i: (0, i, 0)))
        out_specs.append(pl.BlockSpec((None, blk, cols), lambda i: (0, i, 0)))
        out_shape.append(jax.ShapeDtypeStruct((1, rows, cols), BF16))
    return pl.pallas_call(
        _cast_kernel,
        grid=(CAST_STEPS,),
        in_specs=in_specs,
        out_specs=tuple(out_specs),
        out_shape=tuple(out_shape),
        compiler_params=pltpu.CompilerParams(
            dimension_semantics=("arbitrary",), vmem_limit_bytes=VMEM_LIMIT),
        name="cast_layer0",
    )(*weights_f32)


def _layer_prompt(layer, x, mixer_wts, ffn_wts, next_f32):
    n, seq, _ = x.shape
    TT = PROMPT_TILE
    nt = seq // TT
    last = n * nt - 1
    n_steps = n * nt + 2

    def mix_tile(i):
        return jnp.minimum(i, last)

    def ffn_tile(i):
        return jnp.maximum(i - 2, 0)

    out_shape = [
        jax.ShapeDtypeStruct((n, seq, D_MODEL), F32),
        jax.ShapeDtypeStruct((n, SC_WIDTH - 1, G), F32),
        jax.ShapeDtypeStruct((n, POOL_BUF, G), F32),
        jax.ShapeDtypeStruct((n, HEADS, DK, DK), F32),
        jax.ShapeDtypeStruct((n, CONF_WIDTH - 1, G), F32),
    ]
    out_specs = [
        pl.BlockSpec((None, TT, D_MODEL), lambda i: (ffn_tile(i) // nt, ffn_tile(i) % nt, 0)),
        pl.BlockSpec((None, SC_WIDTH - 1, G), lambda i: (mix_tile(i) // nt, 0, 0)),
        pl.BlockSpec((None, POOL_BUF, G), lambda i: (mix_tile(i) // nt, 0, 0)),
        pl.BlockSpec((None, HEADS, DK, DK), lambda i: (mix_tile(i) // nt, 0, 0, 0)),
        pl.BlockSpec((None, CONF_WIDTH - 1, G), lambda i: (mix_tile(i) // nt, 0, 0)),
    ]
    cast_in_specs = []
    for w in next_f32:
        _, rows, cols = w.shape
        blk = next(b for b in range(CAST_ROWS, rows + 1, CAST_ROWS)
                   if rows % b == 0 and rows // b <= n_steps)
        n_blk = rows // blk

        def in_map(i, n_blk=n_blk):
            return (layer + 1, jnp.minimum(i, n_blk - 1), 0)

        def out_map(i, n_blk=n_blk):
            return (0, jnp.minimum(i, n_blk - 1), 0)

        cast_in_specs.append(pl.BlockSpec((None, blk, cols), in_map))
        out_specs.append(pl.BlockSpec((None, blk, cols), out_map))
        out_shape.append(jax.ShapeDtypeStruct((1, rows, cols), BF16))
    scratch = [
        pltpu.VMEM((TT, D_IN), F32),
        pltpu.VMEM((CONV_PAD + TT, G), F32),
        pltpu.VMEM((POOL_PAD + TT, G), F32),
        pltpu.VMEM((CONF_PAD + TT, G), F32),
        pltpu.VMEM((SUBLANES - 1, CONF_PAD + TT, G), F32),
        pltpu.VMEM((G, G), F32),
        pltpu.VMEM((TT, G), F32),
        pltpu.VMEM((TT, G), F32),
        pltpu.VMEM((TT, G), F32),
        pltpu.VMEM((TT, G), F32),
        pltpu.VMEM((TT, G), F32),
        pltpu.VMEM((TT, D_MODEL), BF16),
        pltpu.VMEM((3, TT, D_MODEL), F32),
        pltpu.VMEM((TT, D_MODEL), BF16),
        pltpu.VMEM((TT, D_MODEL), BF16),
        pltpu.VMEM((2, TT, D_FF), BF16),
        pltpu.VMEM((TT, D_MODEL), F32),
        pltpu.VMEM((TT, D_MODEL), F32),
    ]
    return pl.pallas_call(
        functools.partial(_layer_prompt_kernel, layer, TT, nt, n * nt, len(next_f32)),
        grid=(n_steps,),
        in_specs=[pl.BlockSpec((None, TT, D_MODEL),
                               lambda i: (mix_tile(i) // nt, mix_tile(i) % nt, 0))]
        + _mixer_weight_specs(layer) + _ffn_weight_specs(layer) + cast_in_specs,
        out_specs=tuple(out_specs),
        out_shape=tuple(out_shape),
        scratch_shapes=scratch,
        compiler_params=pltpu.CompilerParams(
            dimension_semantics=("arbitrary",), vmem_limit_bytes=VMEM_LIMIT),
        name=f"layer_prompt_l{layer}",
    )(x, *mixer_wts, *ffn_wts, *next_f32)


def _layer_sample(layer, x2d, s_conv, s_pool, s_hgrn, s_conf, mixer_wts, ffn_wts, n_seq):
    m = x2d.shape[0]
    NS = n_seq
    TS = SAMPLE_STEPS
    M = TS * NS

    def out_spec(shape):
        nd = len(shape)
        return pl.BlockSpec(tuple(shape), lambda i: (0,) * nd, pipeline_mode=pl.Buffered(1))

    in_hbm = pl.BlockSpec(memory_space=pl.ANY)
    state_shapes = [(SC_WIDTH - 1, NS, G), (POOL_BUF, NS, G), (G, DK, NS), (CONF_WIDTH - 1, NS, G)]
    in_specs = ([pl.BlockSpec((M, D_MODEL), lambda i: (i, 0))]
                + [in_hbm] * len(state_shapes) + _mixer_weight_specs(layer)
                + [_layer_spec((1, D_MODEL), layer), in_hbm, in_hbm, in_hbm,
                   _layer_spec((1, D_MODEL), layer)])
    out_specs = tuple([pl.BlockSpec((M, D_MODEL), lambda i: (i, 0))]
                      + [out_spec(s) for s in state_shapes])
    out_shape = tuple([jax.ShapeDtypeStruct((m, D_MODEL), F32)]
                      + [jax.ShapeDtypeStruct(s, F32) for s in state_shapes])
    scratch = [
        pltpu.VMEM((M, D_IN), F32),
        pltpu.VMEM((M, G), F32),
        pltpu.VMEM((M, G), F32),
        pltpu.VMEM((TS, G, NS), F32),
        pltpu.VMEM((TS, G, NS), F32),
        pltpu.VMEM((TS, G, NS), F32),
        pltpu.VMEM((TS, G, NS), F32),
        pltpu.VMEM((TS, G, NS), F32),
        pltpu.VMEM((M, D_MODEL), BF16),
        pltpu.VMEM((M, D_MODEL), BF16),
        pltpu.VMEM((D_MODEL, D_FF), BF16),
        pltpu.VMEM((D_MODEL, D_FF), BF16),
        pltpu.VMEM((D_FF, D_MODEL), BF16),
        pltpu.VMEM((M, D_FF), BF16),
        pltpu.VMEM((M, D_MODEL), F32),
        pltpu.SemaphoreType.DMA((N_FIRST_STEP_COPIES,)),
    ]
    return pl.pallas_call(
        functools.partial(_layer_sample_kernel, layer, NS, TS, PAST_LEN),
        grid=(m // M,),
        in_specs=in_specs,
        out_specs=out_specs,
        out_shape=out_shape,
        scratch_shapes=scratch,
        compiler_params=pltpu.CompilerParams(
            dimension_semantics=("arbitrary",), vmem_limit_bytes=VMEM_LIMIT),
        name=f"layer_sample_l{layer}",
    )(x2d, s_conv, s_pool, s_hgrn, s_conf, *mixer_wts, *ffn_wts)


def kernel(x_prompt, x_sample, state_conv, state_pool, state_hgrn, state_conf, norm_mix_pre, norm_mix_post, w_in, conv_w, pool_w, pool_scale, hgrn_lb, hgrn_norm, conf_dw, conf_b, conf_ln_g, conf_ln_b, w_out, norm_ffn_pre, norm_ffn_post, w_gate, w_up, w_down):
    def row(a):
        return a.reshape(DEPTH, 1, a.shape[-1])

    eye = jnp.eye(G // POOL_CH, dtype=pool_w.dtype)
    pool_bd = (pool_w[:, :, :, None, :] * eye[None, :, None, :, None]).reshape(DEPTH, G, G)
    pool_bd = pool_bd.astype(BF16)
    big_f32 = (w_in, w_out, w_gate, w_up, w_down)
    big = _cast_first_layer(big_f32)

    def mixer_weights(big):
        return (row(norm_mix_pre), big[0], conv_w, pool_bd, row(pool_scale), hgrn_lb,
                row(hgrn_norm), conf_dw, row(conf_b), row(conf_ln_g), row(conf_ln_b), big[1],
                row(norm_mix_post))

    def ffn_weights(big):
        return (row(norm_ffn_pre), big[2], big[3], big[4], row(norm_ffn_post))

    ns, ts, _ = x_sample.shape
    xs = x_sample.transpose(1, 0, 2).reshape(ts * ns, D_MODEL)
    sc_t = state_conv.transpose(1, 2, 0, 3)
    sp_t = state_pool.transpose(1, 2, 0, 3)
    sf_t = state_conf.transpose(1, 2, 0, 3)
    sh_t = state_hgrn.transpose(1, 2, 3, 4, 0).reshape(DEPTH, G, DK, ns)
    xp = x_prompt
    p_states, s_states = [], []
    for layer in range(DEPTH):
        mixer_wts, ffn_wts = mixer_weights(big), ffn_weights(big)
        next_f32 = big_f32 if layer + 1 < DEPTH else ()
        xp, *rest = _layer_prompt(layer, xp, mixer_wts, ffn_wts, next_f32)
        p_states.append(rest[:4])
        big = tuple(rest[4:])
        xs, *sts = _layer_sample(layer, xs, sc_t, sp_t, sh_t, sf_t, mixer_wts, ffn_wts, ns)
        s_states.append(sts)
    xs = xs.reshape(ts, ns, D_MODEL).transpose(1, 0, 2)

    def stack(states, i):
        return jnp.stack([states[layer][i] for layer in range(DEPTH)], axis=1)

    def stack_t(i):
        return jnp.stack([s_states[layer][i] for layer in range(DEPTH)], axis=0).transpose(2, 0, 1, 3)

    hgrn_s = jnp.stack([s_states[layer][2] for layer in range(DEPTH)], axis=0)
    hgrn_s = hgrn_s.reshape(DEPTH, HEADS, DK, DK, ns).transpose(4, 0, 1, 2, 3)

    return (xp, xs,
            stack(p_states, 0), stack(p_states, 1), stack(p_states, 2), stack(p_states, 3),
            stack_t(0), stack_t(1), hgrn_s, stack_t(3))
```

```python
import functools

import jax
import jax.numpy as jnp
from jax import lax
from jax.experimental import pallas as pl
from jax.experimental.pallas import tpu as pltpu

F32 = jnp.float32
BF16 = jnp.bfloat16

D_MODEL = 1024
DEPTH = 2
PAST_LEN = 16384
G = 256
N_BLOCKS = 10
D_IN = N_BLOCKS * G
SC_WIDTH = 3
POOL_WINDOWS = (2, 4, 8, 16)
POOL_BUF = max(POOL_WINDOWS) - 1
POOL_CH = 64
HEADS = 4
DK = 64
CONF_WIDTH = 31
D_FF = 2816
EPS = 1e-6
F_MIN = 1e-20

SUBLANES = 8


def _round_up(n, m):
    return -(-n // m) * m


CONV_PAD = _round_up(SC_WIDTH - 1, SUBLANES)
POOL_PAD = _round_up(POOL_BUF, SUBLANES)
CONF_PAD = _round_up(CONF_WIDTH - 1, SUBLANES)

HGRN_CHUNK = 64
FAST_DECAY_LIMIT = 60.0
PROMPT_TILE = 256
SAMPLE_STEPS = 2
FFN_COLS = 256
N_FIRST_STEP_COPIES = 7
CAST_ROWS = 16
CONV_ROWS = 64
VMEM_LIMIT = 56 * 1024 * 1024


def _sigmoid(x):
    return jax.nn.sigmoid(x)


def _silu(x):
    return x * jax.nn.sigmoid(x)


def _rmsnorm(x, g):
    ms = jnp.mean(x * x, axis=-1, keepdims=True)
    return x * lax.rsqrt(ms + EPS) * g


def _head_block_mask(rows, cols, row_block, col_block):
    r = lax.broadcasted_iota(jnp.int32, (rows, cols), 0) // row_block
    c = lax.broadcasted_iota(jnp.int32, (rows, cols), 1) // col_block
    return r == c


def _cumsum_rows_mxu(x):
    n = x.shape[0]
    tri = (lax.broadcasted_iota(jnp.int32, (n, n), 0)
           >= lax.broadcasted_iota(jnp.int32, (n, n), 1))
    tri = jnp.where(tri, 1.0, 0.0).astype(BF16)
    hi = x.astype(BF16)
    lo = (x - hi.astype(F32)).astype(BF16)
    return (jnp.dot(tri, hi, preferred_element_type=F32)
            + jnp.dot(tri, lo, preferred_element_type=F32))


def _lower_bound(lb_all, layer):
    m = jnp.max(lb_all, axis=0, keepdims=True)
    e = jnp.exp(lb_all - m)
    sm = e / jnp.sum(e, axis=0, keepdims=True)
    cs = sm[0:1]
    for i in range(1, layer + 1):
        cs = cs + sm[i:i + 1]
    return cs - sm[0:1]


def _hgrn_gates(zq, zf, zi, lower):
    q = _silu(zq)
    f = lower + (1.0 - lower) * _sigmoid(zf)
    logf = jnp.log(jnp.maximum(f, F_MIN))
    return q, 1.0 - f, zi, logf


def _hgrn_state_terms(q, kk, v, b, st_ref):
    TT = q.shape[0]
    b_end = b[TT - 1:TT, :]
    st = st_ref[...]
    qs = (q * jnp.exp(b)).astype(BF16)
    o_inter = lax.dot_general(qs, st.astype(BF16), (((1,), (1,)), ((), ())),
                              preferred_element_type=F32)
    kh = (kk * jnp.exp(b_end - b)).astype(BF16)
    upd = lax.dot_general(v.astype(BF16), kh, (((0,), (0,)), ((), ())),
                          preferred_element_type=F32)
    bd = _head_block_mask(G, G, DK, DK)
    st_ref[...] = st * jnp.exp(b_end) + jnp.where(bd, upd, 0.0)
    return o_inter


def _hgrn_refs(b, C):
    refs = []
    span = None
    for j in range(b.shape[0] // C):
        first = b[j * C:j * C + 1, :]
        last = b[(j + 1) * C - 1:(j + 1) * C, :]
        refs.append(0.5 * (first + last))
        half = jnp.max(0.5 * (first - last))
        span = half if span is None else jnp.maximum(span, half)
    return refs, span


def _hgrn_fast_chunk(q_tgt, b_tgt, kk_src, v_src, r, C):
    nt = q_tgt.shape[0]
    rows_mask = _head_block_mask(HEADS * C, G, C, DK)
    qz = (q_tgt * jnp.exp(b_tgt - r)).astype(BF16)
    ke = kk_src * jnp.exp(r - b_tgt[0:C])
    kebd = jnp.where(rows_mask, jnp.concatenate([ke] * HEADS, axis=0), 0.0).astype(BF16)
    attn = lax.dot_general(qz, kebd, (((1,), (1,)), ((), ())),
                           preferred_element_type=F32)
    t_idx = lax.broadcasted_iota(jnp.int32, (nt, HEADS * C), 0)
    s_idx = lax.broadcasted_iota(jnp.int32, (nt, HEADS * C), 1) % C
    attn = jnp.where(t_idx >= s_idx, attn, 0.0).astype(BF16)
    vbd = jnp.where(rows_mask, jnp.concatenate([v_src] * HEADS, axis=0), 0.0).astype(BF16)
    return jnp.dot(attn, vbd, preferred_element_type=F32)


def _hgrn_exact_attn(hb_ref, hq_ref, kk, v, o_ref, row0, TT):
    b = hb_ref[0:TT, :]
    ones_bd = jnp.where(_head_block_mask(G, G, DK, DK), 1.0, 0.0).astype(BF16)
    s_row = lax.broadcasted_iota(jnp.int32, (TT, G), 0)

    def body(t, carry):
        bt = hb_ref[pl.ds(t, 1), :]
        qt = hq_ref[pl.ds(t, 1), :]
        e = jnp.where(s_row <= t, qt * kk * jnp.exp(jnp.minimum(bt - b, 0.0)), 0.0)
        a = jnp.dot(e.astype(BF16), ones_bd, preferred_element_type=F32)
        o_ref[pl.ds(row0 + t, 1), :] = jnp.sum(a * v, axis=0, keepdims=True)
        return carry

    lax.fori_loop(0, TT, body, 0)


def _head_norm_gate(o, zg, hnorm):
    ones_bd = jnp.where(_head_block_mask(G, G, DK, DK), 1.0, 0.0).astype(BF16)
    ssq = jnp.dot((o * o).astype(BF16), ones_bd, preferred_element_type=F32)
    return o * lax.rsqrt(ssq * (1.0 / DK) + EPS) * hnorm * _silu(zg)


def _pool_select(sums, pos):
    shape = sums[POOL_WINDOWS[0]].shape
    grp = lax.broadcasted_iota(jnp.int32, shape, len(shape) - 1) // POOL_CH
    ssum = sums[POOL_WINDOWS[-1]]
    win = jnp.full(shape, POOL_WINDOWS[-1], jnp.int32)
    for gi in range(len(POOL_WINDOWS) - 2, -1, -1):
        ssum = jnp.where(grp == gi, sums[POOL_WINDOWS[gi]], ssum)
        win = jnp.where(grp == gi, POOL_WINDOWS[gi], win)
    cnt = jnp.minimum(pos + 1, win).astype(F32)
    return ssum / cnt


def _conf_tail(z, cb, lng, lnb):
    z = z + cb
    mu = jnp.mean(z, axis=-1, keepdims=True)
    zc = z - mu
    var = jnp.mean(zc * zc, axis=-1, keepdims=True)
    return _silu(zc * lax.rsqrt(var + EPS) * lng + lnb)


def _layer_prompt_kernel(layer, TT, nt, n_tiles, n_cast, *refs):
    (x_ref, npre_ref, win_ref, convw_ref, poolbd_ref, pscale_ref, lb_ref,
     hnorm_ref, cdw_ref, cb_ref, lng_ref, lnb_ref, wout_ref, npost_ref,
     fpre_ref, wg_ref, wu_ref, wd_ref, fpost_ref) = refs[:19]
    cast_in = refs[19:19 + n_cast]
    y_ref, oconv_ref, opool_ref, ohgrn_ref, oconf_ref = refs[19 + n_cast:24 + n_cast]
    cast_out = refs[24 + n_cast:24 + 2 * n_cast]
    (p_ref, ea_ref, eb_ref, ed_ref, sh_ref, st_ref, hb_ref, hq_ref, hk_ref, oi_ref,
     o_ref, cat_ref, x1_ref, hm_ref, hf_ref, a_ref, ff_ref, mix_ref) = refs[24 + 2 * n_cast:]
    i = pl.program_id(0)
    t = i % nt
    slot = i % 3

    @pl.when(i == 0)
    def _first():
        x1_ref[1] = jnp.zeros((TT, D_MODEL), F32)
        x1_ref[2] = jnp.zeros((TT, D_MODEL), F32)
        a_ref[0] = jnp.zeros((TT, D_FF), BF16)

    @pl.when((t == 0) & (i < n_tiles))
    def _new_sequence():
        ea_ref[0:CONV_PAD, :] = jnp.zeros((CONV_PAD, G), F32)
        eb_ref[0:POOL_PAD, :] = jnp.zeros((POOL_PAD, G), F32)
        ed_ref[0:CONF_PAD, :] = jnp.zeros((CONF_PAD, G), F32)
        st_ref[...] = jnp.zeros((G, G), F32)

    lower = _lower_bound(lb_ref[...], layer)
    n_chunks = TT // HGRN_CHUNK
    slot_up = (i + 2) % 3
    slot_down = (i + 1) % 3
    a_new = (i + 1) % 2
    a_old = i % 2

    def f_norm():
        hf_ref[...] = _rmsnorm(x1_ref[slot_up], fpre_ref[...]).astype(BF16)

    def f_gate_up(j):
        cols = slice(j * FFN_COLS, (j + 1) * FFN_COLS)
        hf = hf_ref[...]
        g = jnp.dot(hf, wg_ref[:, cols], preferred_element_type=F32)
        u = jnp.dot(hf, wu_ref[:, cols], preferred_element_type=F32)
        a_ref[a_new, :, cols] = (_silu(g) * u).astype(BF16)

    def f_down(k):
        cols = slice(k * G, (k + 1) * G)
        ff_ref[:, cols] = jnp.dot(a_ref[a_old], wd_ref[:, cols], preferred_element_type=F32)

    def f_out():
        y_ref[...] = x1_ref[slot_down] + _rmsnorm(ff_ref[...], fpost_ref[...])

    def cast_next():
        for src, dst in zip(cast_in, cast_out):
            dst[...] = src[...].astype(BF16)

    def m_norm():
        hm_ref[...] = _rmsnorm(x_ref[...], npre_ref[...]).astype(BF16)

    def m_proj(blk):
        cols = slice(blk * G, (blk + 1) * G)
        p_ref[:, cols] = jnp.dot(hm_ref[...], win_ref[:, cols], preferred_element_type=F32)

    def m_conv():
        cu = p_ref[:, G:2 * G] * p_ref[:, 2 * G:3 * G]
        ea_ref[CONV_PAD:CONV_PAD + TT, :] = cu
        ya = convw_ref[SC_WIDTH - 1:SC_WIDTH, :] * cu
        for back in range(1, SC_WIDTH):
            w = convw_ref[SC_WIDTH - 1 - back:SC_WIDTH - back, :]
            ya = ya + w * ea_ref[CONV_PAD - back:CONV_PAD - back + TT, :]
        cat_ref[:, 0:G] = (p_ref[:, 0:G] * ya).astype(BF16)
        oconv_ref[...] = ea_ref[TT + CONV_PAD - (SC_WIDTH - 1):TT + CONV_PAD, :]
        ea_ref[0:CONV_PAD, :] = ea_ref[TT:TT + CONV_PAD, :]

    def m_pool():
        pp = p_ref[:, 3 * G:4 * G]
        eb_ref[POOL_PAD:POOL_PAD + TT, :] = pp
        run = eb_ref[...]
        sums = {}
        w = 1
        while w < POOL_WINDOWS[-1]:
            run = run + pltpu.roll(run, w, 0)
            w *= 2
            sums[w] = run[POOL_PAD:]
        pos = t * TT + lax.broadcasted_iota(jnp.int32, (TT, G), 0)
        mean = _pool_select(sums, pos)
        yb = jnp.dot((mean - pp).astype(BF16), poolbd_ref[...], preferred_element_type=F32)
        cat_ref[:, G:2 * G] = (yb * pscale_ref[...]).astype(BF16)
        opool_ref[...] = eb_ref[TT + POOL_PAD - POOL_BUF:TT + POOL_PAD, :]
        eb_ref[0:POOL_PAD, :] = eb_ref[TT:TT + POOL_PAD, :]

    hg = {}

    def m_hgrn_gates():
        q, kk, _, logf = _hgrn_gates(p_ref[:, 4 * G:5 * G], p_ref[:, 5 * G:6 * G],
                                     p_ref[:, 6 * G:7 * G], lower)
        hq_ref[...] = q
        hk_ref[...] = kk
        hb_ref[...] = _cumsum_rows_mxu(logf)

    def m_hgrn_state():
        b = hb_ref[...]
        o_inter = _hgrn_state_terms(hq_ref[...], hk_ref[...], p_ref[:, 6 * G:7 * G], b, st_ref)
        oi_ref[...] = o_inter
        o_ref[...] = o_inter
        hg["refs"], hg["span"] = _hgrn_refs(b, HGRN_CHUNK)

    def m_hgrn_chunk(j):
        lo, hi = j * HGRN_CHUNK, (j + 1) * HGRN_CHUNK
        contrib = _hgrn_fast_chunk(hq_ref[lo:TT, :], hb_ref[lo:TT, :], hk_ref[lo:hi, :],
                                   p_ref[lo:hi, 6 * G:7 * G], hg["refs"][j], HGRN_CHUNK)
        o_ref[lo:TT, :] = o_ref[lo:TT, :] + contrib

    def m_hgrn_out():
        yc = _head_norm_gate(o_ref[...], p_ref[:, 7 * G:8 * G], hnorm_ref[...])
        cat_ref[:, 2 * G:3 * G] = yc.astype(BF16)

    def m_glu():
        ed_ref[CONF_PAD:CONF_PAD + TT, :] = (p_ref[:, 8 * G:9 * G]
                                             * _sigmoid(p_ref[:, 9 * G:10 * G]))
        ed = ed_ref[...]
        for r in range(1, SUBLANES):
            sh_ref[r - 1] = pltpu.roll(ed, TT + CONF_PAD - r, 0)

    def m_conf(rb):
        base = rb * CONV_ROWS
        first = CONF_PAD - (CONF_WIDTH - 1)
        acc = None
        for j in range(CONF_WIDTH):
            tiles, r = divmod(first + j, SUBLANES)
            lo = base + SUBLANES * tiles
            src = ed_ref[lo:lo + CONV_ROWS, :] if r == 0 else sh_ref[r - 1, lo:lo + CONV_ROWS, :]
            term = cdw_ref[j:j + 1, :] * src
            acc = term if acc is None else acc + term
        yd = _conf_tail(acc, cb_ref[...], lng_ref[...], lnb_ref[...])
        cat_ref[base:base + CONV_ROWS, 3 * G:4 * G] = yd.astype(BF16)

    def m_conf_tail():
        oconf_ref[...] = ed_ref[TT + CONF_PAD - (CONF_WIDTH - 1):TT + CONF_PAD, :]
        ed_ref[0:CONF_PAD, :] = ed_ref[TT:TT + CONF_PAD, :]

    def m_out():
        mix_ref[...] = jnp.dot(cat_ref[...], wout_ref[...], preferred_element_type=F32)

    def m_out_norm():
        x1_ref[slot] = x_ref[...] + _rmsnorm(mix_ref[...], npost_ref[...])

    n_gu = D_FF // FFN_COLS
    gate_up = [functools.partial(f_gate_up, j) for j in range(n_gu)]
    proj = [functools.partial(m_proj, blk) for blk in range(N_BLOCKS)]
    down = [functools.partial(f_down, k) for k in range(D_MODEL // G)]
    conf_all = [functools.partial(m_conf, rb) for rb in range(TT // CONV_ROWS)]
    per = len(conf_all) // 4
    conf = [conf_all[k * per:(k + 1) * per] for k in range(4)]
    chunk = [functools.partial(m_hgrn_chunk, j) for j in range(n_chunks)]
    g = gate_up
    schedule = [
        down[0], f_norm, down[1], m_norm, down[2], cast_next, down[3],
        proj[8], proj[9], f_out,
        g[0], m_glu, proj[3],
        g[1], *conf[0], proj[0], proj[1], proj[2],
        g[2], *conf[1], proj[4], proj[5], proj[6],
        g[3], *conf[2], proj[7],
        g[4], *conf[3], m_conf_tail,
        g[5], m_conv, m_pool,
        g[6], m_hgrn_gates,
        g[7], m_hgrn_state,
        g[8], chunk[0], chunk[1],
        g[9], chunk[2], chunk[3],
        m_hgrn_out, m_out, g[10], m_out_norm,
    ]
    assert n_gu == 11 and n_chunks == 4 and len(conf_all) == 4 * per and len(down) == 4
    for piece in schedule:
        piece()

    @pl.when(hg["span"] >= FAST_DECAY_LIMIT)
    def _redo_exact():
        f = lower + (1.0 - lower) * _sigmoid(p_ref[:, 5 * G:6 * G])
        _hgrn_exact_attn(hb_ref, hq_ref, 1.0 - f, p_ref[:, 6 * G:7 * G], o_ref, 0, TT)
        yce = _head_norm_gate(o_ref[...] + oi_ref[...], p_ref[:, 7 * G:8 * G], hnorm_ref[...])
        cat_ref[:, 2 * G:3 * G] = yce.astype(BF16)
        mixe = jnp.dot(cat_ref[...], wout_ref[...], preferred_element_type=F32)
        x1_ref[slot] = x_ref[...] + _rmsnorm(mixe, npost_ref[...])

    @pl.when((t == nt - 1) & (i < n_tiles))
    def _state_out():
        s = st_ref[...].T
        for hh in range(HEADS):
            ohgrn_ref[hh] = s[hh * DK:(hh + 1) * DK, hh * DK:(hh + 1) * DK]


def _layer_sample_kernel(layer, NS, TS, start_pos,
                         x_ref, sconv_hbm, spool_hbm, shgrn_hbm, sconf_hbm,
                         npre_ref, win_ref, convw_ref, poolbd_ref, pscale_ref, lb_ref,
                         hnorm_ref, cdw_ref, cb_ref, lng_ref, lnb_ref, wout_ref, npost_ref,
                         fpre_ref, wg_hbm, wu_hbm, wd_hbm, fpost_ref,
                         y_ref, oconv_ref, opool_ref, ohgrn_ref, oconf_ref,
                         p_ref, u_ref, pool_ref, qT_ref, fT_ref, kT_ref, vT_ref, oT_ref, cat_ref,
                         hm_ref, wg_ref, wu_ref, wd_ref, a_ref, ff_ref, sems):
    i = pl.program_id(0)
    copies = [
        (sconv_hbm.at[layer], oconv_ref), (spool_hbm.at[layer], opool_ref),
        (sconf_hbm.at[layer], oconf_ref), (shgrn_hbm.at[layer], ohgrn_ref),
        (wg_hbm.at[0], wg_ref), (wu_hbm.at[0], wu_ref), (wd_hbm.at[0], wd_ref),
    ]

    def copy(k):
        return pltpu.make_async_copy(copies[k][0], copies[k][1], sems.at[k])

    def wait_at_first_step(ks):
        @pl.when(i == 0)
        def _wait():
            for k in ks:
                copy(k).wait()

    @pl.when(i == 0)
    def _start_copies():
        for k in range(len(copies)):
            copy(k).start()

    wait_at_first_step([0, 1, 2])

    def slab(t):
        return slice(t * NS, (t + 1) * NS)

    def proj(blk):
        cols = slice(blk * G, (blk + 1) * G)
        p_ref[:, cols] = jnp.dot(hm_ref[...], win_ref[:, cols], preferred_element_type=F32)

    def conv_in(j):
        if j < SC_WIDTH - 1:
            return oconv_ref[j]
        rows = slab(j - (SC_WIDTH - 1))
        return p_ref[rows, G:2 * G] * p_ref[rows, 2 * G:3 * G]

    def m_conv():
        for t in range(TS):
            ya = (convw_ref[0:1, :] * conv_in(t) + convw_ref[1:2, :] * conv_in(t + 1)
                  + convw_ref[2:3, :] * conv_in(t + 2))
            cat_ref[slab(t), 0:G] = (p_ref[slab(t), 0:G] * ya).astype(BF16)
        for j in range(SC_WIDTH - 1):
            oconv_ref[j] = conv_in(j + TS)

    def pool_in(j):
        if j < POOL_BUF:
            return opool_ref[j]
        return p_ref[slab(j - POOL_BUF), 3 * G:4 * G]

    def m_pool():
        for t in range(TS):
            idx = POOL_BUF + t
            run = pool_in(idx)
            sums = {}
            for j in range(1, POOL_BUF + 1):
                run = run + pool_in(idx - j)
                if j + 1 in POOL_WINDOWS:
                    sums[j + 1] = run
            pos = jnp.full((NS, G), start_pos + i * TS + t, jnp.int32)
            mean = _pool_select(sums, pos)
            pool_ref[slab(t), :] = mean - pool_in(idx)
        yb = jnp.dot(pool_ref[...].astype(BF16), poolbd_ref[...], preferred_element_type=F32)
        cat_ref[:, G:2 * G] = (yb * pscale_ref[...]).astype(BF16)
        for j in range(POOL_BUF):
            opool_ref[j] = pool_in(j + TS)

    def m_hgrn_gates():
        lower = _lower_bound(lb_ref[...], layer)
        q = _silu(p_ref[:, 4 * G:5 * G])
        f = lower + (1.0 - lower) * _sigmoid(p_ref[:, 5 * G:6 * G])
        for t in range(TS):
            qT_ref[t] = q[slab(t)].T
            fT_ref[t] = jnp.maximum(f[slab(t)], F_MIN).T
            kT_ref[t] = (1.0 - f[slab(t)]).T
            vT_ref[t] = p_ref[slab(t), 6 * G:7 * G].T

    def m_hgrn_scan():
        wait_at_first_step([3])
        for hh in range(HEADS):
            head = slice(hh * DK, (hh + 1) * DK)
            vts = [vT_ref[t, head, :] for t in range(TS)]

            def body(k, accs, hh=hh, vts=vts):
                c = hh * DK + k
                s = ohgrn_ref[c]
                out = []
                for t in range(TS):
                    s = fT_ref[t, pl.ds(c, 1), :] * s + kT_ref[t, pl.ds(c, 1), :] * vts[t]
                    out.append(accs[t] + qT_ref[t, pl.ds(c, 1), :] * s)
                ohgrn_ref[c] = s
                return tuple(out)

            accs = lax.fori_loop(0, DK, body,
                                 tuple(jnp.zeros((DK, NS), F32) for _ in range(TS)), unroll=2)
            for t in range(TS):
                oT_ref[t, head, :] = accs[t]

    def m_hgrn_out():
        o = jnp.concatenate([oT_ref[t].T for t in range(TS)], axis=0)
        yc = _head_norm_gate(o, p_ref[:, 7 * G:8 * G], hnorm_ref[...])
        cat_ref[:, 2 * G:3 * G] = yc.astype(BF16)

    def m_glu():
        u_ref[...] = p_ref[:, 8 * G:9 * G] * _sigmoid(p_ref[:, 9 * G:10 * G])

    HALF = NS // 2

    def conf_in(j, rows):
        if j < CONF_WIDTH - 1:
            return oconf_ref[j, rows, :]
        base = (j - (CONF_WIDTH - 1)) * NS
        return u_ref[base + rows.start:base + rows.stop, :]

    def m_conf():
        for t in range(TS):
            for hf in range(2):
                rows = slice(hf * HALF, (hf + 1) * HALF)
                acc = None
                for j in range(CONF_WIDTH):
                    term = cdw_ref[j:j + 1, :] * conf_in(t + j, rows)
                    acc = term if acc is None else acc + term
                yd = _conf_tail(acc, cb_ref[...], lng_ref[...], lnb_ref[...])
                cat_ref[t * NS + hf * HALF:t * NS + (hf + 1) * HALF, 3 * G:4 * G] = (
                    yd.astype(BF16))
        for j in range(CONF_WIDTH - 1):
            oconf_ref[j] = conf_in(j + TS, slice(0, NS))

    hm_ref[...] = _rmsnorm(x_ref[...], npre_ref[...]).astype(BF16)
    for blk in range(N_BLOCKS):
        proj(blk)
    m_conv()
    m_pool()
    m_hgrn_gates()
    m_hgrn_scan()
    m_hgrn_out()
    m_glu()
    m_conf()
    mix = jnp.dot(cat_ref[...], wout_ref[...], preferred_element_type=F32)
    y_ref[...] = x_ref[...] + _rmsnorm(mix, npost_ref[...])

    wait_at_first_step([4, 5, 6])
    hm_ref[...] = _rmsnorm(y_ref[...], fpre_ref[...]).astype(BF16)
    for j in range(D_FF // FFN_COLS):
        cols = slice(j * FFN_COLS, (j + 1) * FFN_COLS)
        hf = hm_ref[...]
        g = jnp.dot(hf, wg_ref[:, cols], preferred_element_type=F32)
        u = jnp.dot(hf, wu_ref[:, cols], preferred_element_type=F32)
        a_ref[:, cols] = (_silu(g) * u).astype(BF16)
    for k in range(D_MODEL // G):
        cols = slice(k * G, (k + 1) * G)
        ff_ref[:, cols] = jnp.dot(a_ref[...], wd_ref[:, cols], preferred_element_type=F32)
    y_ref[...] = y_ref[...] + _rmsnorm(ff_ref[...], fpost_ref[...])


def _layer_spec(shape, layer, single_buffer=False):
    nd = len(shape)

    def imap(*_):
        return (layer,) + (0,) * nd

    if single_buffer:
        return pl.BlockSpec((None,) + tuple(shape), imap, pipeline_mode=pl.Buffered(1))
    return pl.BlockSpec((None,) + tuple(shape), imap)


def _mixer_weight_specs(layer):
    return [
        _layer_spec((1, D_MODEL), layer),
        _layer_spec((D_MODEL, D_IN), 0, True),
        _layer_spec((SC_WIDTH, G), layer),
        _layer_spec((G, G), layer),
        _layer_spec((1, G), layer),
        pl.BlockSpec((DEPTH, G), lambda *_: (0, 0)),
        _layer_spec((1, G), layer),
        _layer_spec((CONF_WIDTH, G), layer),
        _layer_spec((1, G), layer),
        _layer_spec((1, G), layer),
        _layer_spec((1, G), layer),
        _layer_spec((D_MODEL, D_MODEL), 0, True),
        _layer_spec((1, D_MODEL), layer),
    ]


def _ffn_weight_specs(layer):
    return [
        _layer_spec((1, D_MODEL), layer),
        _layer_spec((D_MODEL, D_FF), 0, True),
        _layer_spec((D_MODEL, D_FF), 0, True),
        _layer_spec((D_FF, D_MODEL), 0, True),
        _layer_spec((1, D_MODEL), layer),
    ]


def _layer_prompt(layer, x, mixer_wts, ffn_wts, next_f32):
    n, seq, _ = x.shape
    TT = PROMPT_TILE
    nt = seq // TT
    last = n * nt - 1
    n_steps = n * nt + 2

    def mix_tile(i):
        return jnp.minimum(i, last)

    def ffn_tile(i):
        return jnp.maximum(i - 2, 0)

    out_shape = [
        jax.ShapeDtypeStruct((n, seq, D_MODEL), F32),
        jax.ShapeDtypeStruct((n, SC_WIDTH - 1, G), F32),
        jax.ShapeDtypeStruct((n, POOL_BUF, G), F32),
        jax.ShapeDtypeStruct((n, HEADS, DK, DK), F32),
        jax.ShapeDtypeStruct((n, CONF_WIDTH - 1, G), F32),
    ]
    out_specs = [
        pl.BlockSpec((None, TT, D_MODEL), lambda i: (ffn_tile(i) // nt, ffn_tile(i) % nt, 0)),
        pl.BlockSpec((None, SC_WIDTH - 1, G), lambda i: (mix_tile(i) // nt, 0, 0)),
        pl.BlockSpec((None, POOL_BUF, G), lambda i: (mix_tile(i) // nt, 0, 0)),
        pl.BlockSpec((None, HEADS, DK, DK), lambda i: (mix_tile(i) // nt, 0, 0, 0)),
        pl.BlockSpec((None, CONF_WIDTH - 1, G), lambda i: (mix_tile(i) // nt, 0, 0)),
    ]
    cast_in_specs = []
    for w in next_f32:
        _, rows, cols = w.shape
        blk = next(b for b in range(CAST_ROWS, rows + 1, CAST_ROWS)
                   if rows % b == 0 and rows // b <= n_steps)
        n_blk = rows // blk

        def in_map(i, n_blk=n_blk):
            return (layer + 1, jnp.minimum(i, n_blk - 1), 0)

        def out_map(i, n_blk=n_blk):
            return (0, jnp.minimum(i, n_blk - 1), 0)

        cast_in_specs.append(pl.BlockSpec((None, blk, cols), in_map))
        out_specs.append(pl.BlockSpec((None, blk, cols), out_map))
        out_shape.append(jax.ShapeDtypeStruct((1, rows, cols), BF16))
    scratch = [
        pltpu.VMEM((TT, D_IN), F32),
        pltpu.VMEM((CONV_PAD + TT, G), F32),
        pltpu.VMEM((POOL_PAD + TT, G), F32),
        pltpu.VMEM((CONF_PAD + TT, G), F32),
        pltpu.VMEM((SUBLANES - 1, CONF_PAD + TT, G), F32),
        pltpu.VMEM((G, G), F32),
        pltpu.VMEM((TT, G), F32),
        pltpu.VMEM((TT, G), F32),
        pltpu.VMEM((TT, G), F32),
        pltpu.VMEM((TT, G), F32),
        pltpu.VMEM((TT, G), F32),
        pltpu.VMEM((TT, D_MODEL), BF16),
        pltpu.VMEM((3, TT, D_MODEL), F32),
        pltpu.VMEM((TT, D_MODEL), BF16),
        pltpu.VMEM((TT, D_MODEL), BF16),
        pltpu.VMEM((2, TT, D_FF), BF16),
        pltpu.VMEM((TT, D_MODEL), F32),
        pltpu.VMEM((TT, D_MODEL), F32),
    ]
    return pl.pallas_call(
        functools.partial(_layer_prompt_kernel, layer, TT, nt, n * nt, len(next_f32)),
        grid=(n_steps,),
        in_specs=[pl.BlockSpec((None, TT, D_MODEL),
                               lambda i: (mix_tile(i) // nt, mix_tile(i) % nt, 0))]
        + _mixer_weight_specs(layer) + _ffn_weight_specs(layer) + cast_in_specs,
        out_specs=tuple(out_specs),
        out_shape=tuple(out_shape),
        scratch_shapes=scratch,
        compiler_params=pltpu.CompilerParams(
            dimension_semantics=("arbitrary",), vmem_limit_bytes=VMEM_LIMIT),
        name=f"layer_prompt_l{layer}",
    )(x, *mixer_wts, *ffn_wts, *next_f32)


def _layer_sample(layer, x2d, s_conv, s_pool, s_hgrn, s_conf, mixer_wts, ffn_wts, n_seq):
    m = x2d.shape[0]
    NS = n_seq
    TS = SAMPLE_STEPS
    M = TS * NS

    def out_spec(shape):
        nd = len(shape)
        return pl.BlockSpec(tuple(shape), lambda i: (0,) * nd, pipeline_mode=pl.Buffered(1))

    in_hbm = pl.BlockSpec(memory_space=pl.ANY)
    state_shapes = [(SC_WIDTH - 1, NS, G), (POOL_BUF, NS, G), (G, DK, NS), (CONF_WIDTH - 1, NS, G)]
    in_specs = ([pl.BlockSpec((M, D_MODEL), lambda i: (i, 0))]
                + [in_hbm] * len(state_shapes) + _mixer_weight_specs(layer)
                + [_layer_spec((1, D_MODEL), layer), in_hbm, in_hbm, in_hbm,
                   _layer_spec((1, D_MODEL), layer)])
    out_specs = tuple([pl.BlockSpec((M, D_MODEL), lambda i: (i, 0))]
                      + [out_spec(s) for s in state_shapes])
    out_shape = tuple([jax.ShapeDtypeStruct((m, D_MODEL), F32)]
                      + [jax.ShapeDtypeStruct(s, F32) for s in state_shapes])
    scratch = [
        pltpu.VMEM((M, D_IN), F32),
        pltpu.VMEM((M, G), F32),
        pltpu.VMEM((M, G), F32),
        pltpu.VMEM((TS, G, NS), F32),
        pltpu.VMEM((TS, G, NS), F32),
        pltpu.VMEM((TS, G, NS), F32),
        pltpu.VMEM((TS, G, NS), F32),
        pltpu.VMEM((TS, G, NS), F32),
        pltpu.VMEM((M, D_MODEL), BF16),
        pltpu.VMEM((M, D_MODEL), BF16),
        pltpu.VMEM((D_MODEL, D_FF), BF16),
        pltpu.VMEM((D_MODEL, D_FF), BF16),
        pltpu.VMEM((D_FF, D_MODEL), BF16),
        pltpu.VMEM((M, D_FF), BF16),
        pltpu.VMEM((M, D_MODEL), F32),
        pltpu.SemaphoreType.DMA((N_FIRST_STEP_COPIES,)),
    ]
    return pl.pallas_call(
        functools.partial(_layer_sample_kernel, layer, NS, TS, PAST_LEN),
        grid=(m // M,),
        in_specs=in_specs,
        out_specs=out_specs,
        out_shape=out_shape,
        scratch_shapes=scratch,
        compiler_params=pltpu.CompilerParams(
            dimension_semantics=("arbitrary",), vmem_limit_bytes=VMEM_LIMIT),
        name=f"layer_sample_l{layer}",
    )(x2d, s_conv, s_pool, s_hgrn, s_conf, *mixer_wts, *ffn_wts)


def kernel(x_prompt, x_sample, state_conv, state_pool, state_hgrn, state_conf, norm_mix_pre, norm_mix_post, w_in, conv_w, pool_w, pool_scale, hgrn_lb, hgrn_norm, conf_dw, conf_b, conf_ln_g, conf_ln_b, w_out, norm_ffn_pre, norm_ffn_post, w_gate, w_up, w_down):
    def row(a):
        return a.reshape(DEPTH, 1, a.shape[-1])

    eye = jnp.eye(G // POOL_CH, dtype=pool_w.dtype)
    pool_bd = (pool_w[:, :, :, None, :] * eye[None, :, None, :, None]).reshape(DEPTH, G, G)
    pool_bd = pool_bd.astype(BF16)
    big_f32 = (w_in, w_out, w_gate, w_up, w_down)
    big = tuple(w[0:1].astype(BF16) for w in big_f32)

    def mixer_weights(big):
        return (row(norm_mix_pre), big[0], conv_w, pool_bd, row(pool_scale), hgrn_lb,
                row(hgrn_norm), conf_dw, row(conf_b), row(conf_ln_g), row(conf_ln_b), big[1],
                row(norm_mix_post))

    def ffn_weights(big):
        return (row(norm_ffn_pre), big[2], big[3], big[4], row(norm_ffn_post))

    ns, ts, _ = x_sample.shape
    xs = x_sample.transpose(1, 0, 2).reshape(ts * ns, D_MODEL)
    sc_t = state_conv.transpose(1, 2, 0, 3)
    sp_t = state_pool.transpose(1, 2, 0, 3)
    sf_t = state_conf.transpose(1, 2, 0, 3)
    sh_t = state_hgrn.transpose(1, 2, 3, 4, 0).reshape(DEPTH, G, DK, ns)
    xp = x_prompt
    p_states, s_states = [], []
    for layer in range(DEPTH):
        mixer_wts, ffn_wts = mixer_weights(big), ffn_weights(big)
        next_f32 = big_f32 if layer + 1 < DEPTH else ()
        xp, *rest = _layer_prompt(layer, xp, mixer_wts, ffn_wts, next_f32)
        p_states.append(rest[:4])
        big = tuple(rest[4:])
        xs, *sts = _layer_sample(layer, xs, sc_t, sp_t, sh_t, sf_t, mixer_wts, ffn_wts, ns)
        s_states.append(sts)
    xs = xs.reshape(ts, ns, D_MODEL).transpose(1, 0, 2)

    def stack(states, i):
        return jnp.stack([states[layer][i] for layer in range(DEPTH)], axis=1)

    def stack_t(i):
        return jnp.stack([s_states[layer][i] for layer in range(DEPTH)], axis=0).transpose(2, 0, 1, 3)

    hgrn_s = jnp.stack([s_states[layer][2] for layer in range(DEPTH)], axis=0)
    hgrn_s = hgrn_s.reshape(DEPTH, HEADS, DK, DK, ns).transpose(4, 0, 1, 2, 3)

    return (xp, xs,
            stack(p_states, 0), stack(p_states, 1), stack(p_states, 2), stack(p_states, 3),
            stack_t(0), stack_t(1), hgrn_s, stack_t(3))
```

```python
import functools

import jax
import jax.numpy as jnp
from jax import lax
from jax.experimental import pallas as pl
from jax.experimental.pallas import tpu as pltpu

F32 = jnp.float32
BF16 = jnp.bfloat16

D_MODEL = 1024
DEPTH = 2
PAST_LEN = 16384
G = 256
N_BLOCKS = 10
D_IN = N_BLOCKS * G
SC_WIDTH = 3
POOL_WINDOWS = (2, 4, 8, 16)
POOL_BUF = max(POOL_WINDOWS) - 1
POOL_CH = 64
HEADS = 4
DK = 64
CONF_WIDTH = 31
D_FF = 2816
EPS = 1e-6
F_MIN = 1e-20

SUBLANES = 8


def _round_up(n, m):
    return -(-n // m) * m


CONV_PAD = _round_up(SC_WIDTH - 1, SUBLANES)
POOL_PAD = _round_up(POOL_BUF, SUBLANES)
CONF_PAD = _round_up(CONF_WIDTH - 1, SUBLANES)

HGRN_CHUNK = 64
FAST_DECAY_LIMIT = 60.0
PROMPT_TILE = 256
SAMPLE_STEPS = 2
FFN_COLS = 256
N_FIRST_STEP_COPIES = 7
CAST_ROWS = 16
CONV_ROWS = 64
VMEM_LIMIT = 56 * 1024 * 1024


def _sigmoid(x):
    return jax.nn.sigmoid(x)


def _silu(x):
    return x * jax.nn.sigmoid(x)


def _rmsnorm(x, g):
    ms = jnp.mean(x * x, axis=-1, keepdims=True)
    return x * lax.rsqrt(ms + EPS) * g


def _head_block_mask(rows, cols, row_block, col_block):
    r = lax.broadcasted_iota(jnp.int32, (rows, cols), 0) // row_block
    c = lax.broadcasted_iota(jnp.int32, (rows, cols), 1) // col_block
    return r == c


def _cumsum_rows_mxu(x):
    n = x.shape[0]
    tri = (lax.broadcasted_iota(jnp.int32, (n, n), 0)
           >= lax.broadcasted_iota(jnp.int32, (n, n), 1))
    tri = jnp.where(tri, 1.0, 0.0).astype(BF16)
    hi = x.astype(BF16)
    lo = (x - hi.astype(F32)).astype(BF16)
    return (jnp.dot(tri, hi, preferred_element_type=F32)
            + jnp.dot(tri, lo, preferred_element_type=F32))


def _lower_bound(lb_all, layer):
    m = jnp.max(lb_all, axis=0, keepdims=True)
    e = jnp.exp(lb_all - m)
    sm = e / jnp.sum(e, axis=0, keepdims=True)
    cs = sm[0:1]
    for i in range(1, layer + 1):
        cs = cs + sm[i:i + 1]
    return cs - sm[0:1]


def _hgrn_gates(zq, zf, zi, lower):
    q = _silu(zq)
    f = lower + (1.0 - lower) * _sigmoid(zf)
    logf = jnp.log(jnp.maximum(f, F_MIN))
    return q, 1.0 - f, zi, logf


def _hgrn_state_terms(q, kk, v, b, st_ref):
    TT = q.shape[0]
    b_end = b[TT - 1:TT, :]
    st = st_ref[...]
    qs = (q * jnp.exp(b)).astype(BF16)
    o_inter = lax.dot_general(qs, st.astype(BF16), (((1,), (1,)), ((), ())),
                              preferred_element_type=F32)
    kh = (kk * jnp.exp(b_end - b)).astype(BF16)
    upd = lax.dot_general(v.astype(BF16), kh, (((0,), (0,)), ((), ())),
                          preferred_element_type=F32)
    bd = _head_block_mask(G, G, DK, DK)
    st_ref[...] = st * jnp.exp(b_end) + jnp.where(bd, upd, 0.0)
    return o_inter


def _hgrn_refs(b, C):
    refs = []
    span = None
    for j in range(b.shape[0] // C):
        first = b[j * C:j * C + 1, :]
        last = b[(j + 1) * C - 1:(j + 1) * C, :]
        refs.append(0.5 * (first + last))
        half = jnp.max(0.5 * (first - last))
        span = half if span is None else jnp.maximum(span, half)
    return refs, span


def _hgrn_fast_chunk(q_tgt, b_tgt, kk_src, v_src, r, C):
    nt = q_tgt.shape[0]
    rows_mask = _head_block_mask(HEADS * C, G, C, DK)
    qz = (q_tgt * jnp.exp(b_tgt - r)).astype(BF16)
    ke = kk_src * jnp.exp(r - b_tgt[0:C])
    kebd = jnp.where(rows_mask, jnp.concatenate([ke] * HEADS, axis=0), 0.0).astype(BF16)
    attn = lax.dot_general(qz, kebd, (((1,), (1,)), ((), ())),
                           preferred_element_type=F32)
    t_idx = lax.broadcasted_iota(jnp.int32, (nt, HEADS * C), 0)
    s_idx = lax.broadcasted_iota(jnp.int32, (nt, HEADS * C), 1) % C
    attn = jnp.where(t_idx >= s_idx, attn, 0.0).astype(BF16)
    vbd = jnp.where(rows_mask, jnp.concatenate([v_src] * HEADS, axis=0), 0.0).astype(BF16)
    return jnp.dot(attn, vbd, preferred_element_type=F32)


def _hgrn_exact_attn(hb_ref, hq_ref, kk, v, o_ref, row0, TT):
    b = hb_ref[0:TT, :]
    ones_bd = jnp.where(_head_block_mask(G, G, DK, DK), 1.0, 0.0).astype(BF16)
    s_row = lax.broadcasted_iota(jnp.int32, (TT, G), 0)

    def body(t, carry):
        bt = hb_ref[pl.ds(t, 1), :]
        qt = hq_ref[pl.ds(t, 1), :]
        e = jnp.where(s_row <= t, qt * kk * jnp.exp(jnp.minimum(bt - b, 0.0)), 0.0)
        a = jnp.dot(e.astype(BF16), ones_bd, preferred_element_type=F32)
        o_ref[pl.ds(row0 + t, 1), :] = jnp.sum(a * v, axis=0, keepdims=True)
        return carry

    lax.fori_loop(0, TT, body, 0)


def _head_norm_gate(o, zg, hnorm):
    ones_bd = jnp.where(_head_block_mask(G, G, DK, DK), 1.0, 0.0).astype(BF16)
    ssq = jnp.dot((o * o).astype(BF16), ones_bd, preferred_element_type=F32)
    return o * lax.rsqrt(ssq * (1.0 / DK) + EPS) * hnorm * _silu(zg)


def _pool_select(sums, pos):
    shape = sums[POOL_WINDOWS[0]].shape
    grp = lax.broadcasted_iota(jnp.int32, shape, len(shape) - 1) // POOL_CH
    ssum = sums[POOL_WINDOWS[-1]]
    win = jnp.full(shape, POOL_WINDOWS[-1], jnp.int32)
    for gi in range(len(POOL_WINDOWS) - 2, -1, -1):
        ssum = jnp.where(grp == gi, sums[POOL_WINDOWS[gi]], ssum)
        win = jnp.where(grp == gi, POOL_WINDOWS[gi], win)
    cnt = jnp.minimum(pos + 1, win).astype(F32)
    return ssum / cnt


def _conf_tail(z, cb, lng, lnb):
    z = z + cb
    mu = jnp.mean(z, axis=-1, keepdims=True)
    zc = z - mu
    var = jnp.mean(zc * zc, axis=-1, keepdims=True)
    return _silu(zc * lax.rsqrt(var + EPS) * lng + lnb)


def _layer_prompt_kernel(layer, TT, nt, n_tiles, n_cast, *refs):
    (x_ref, npre_ref, win_ref, convw_ref, poolbd_ref, pscale_ref, lb_ref,
     hnorm_ref, cdw_ref, cb_ref, lng_ref, lnb_ref, wout_ref, npost_ref,
     fpre_ref, wg_ref, wu_ref, wd_ref, fpost_ref) = refs[:19]
    cast_in = refs[19:19 + n_cast]
    y_ref, oconv_ref, opool_ref, ohgrn_ref, oconf_ref = refs[19 + n_cast:24 + n_cast]
    cast_out = refs[24 + n_cast:24 + 2 * n_cast]
    (p_ref, ea_ref, eb_ref, ed_ref, sh_ref, st_ref, hb_ref, hq_ref, hk_ref, oi_ref,
     o_ref, cat_ref, x1_ref, hm_ref, hf_ref, a_ref, ff_ref, mix_ref) = refs[24 + 2 * n_cast:]
    i = pl.program_id(0)
    t = i % nt
    slot = i % 3

    @pl.when(i == 0)
    def _first():
        x1_ref[1] = jnp.zeros((TT, D_MODEL), F32)
        x1_ref[2] = jnp.zeros((TT, D_MODEL), F32)
        a_ref[0] = jnp.zeros((TT, D_FF), BF16)

    @pl.when((t == 0) & (i < n_tiles))
    def _new_sequence():
        ea_ref[0:CONV_PAD, :] = jnp.zeros((CONV_PAD, G), F32)
        eb_ref[0:POOL_PAD, :] = jnp.zeros((POOL_PAD, G), F32)
        ed_ref[0:CONF_PAD, :] = jnp.zeros((CONF_PAD, G), F32)
        st_ref[...] = jnp.zeros((G, G), F32)

    lower = _lower_bound(lb_ref[...], layer)
    n_chunks = TT // HGRN_CHUNK
    slot_up = (i + 2) % 3
    slot_down = (i + 1) % 3
    a_new = (i + 1) % 2
    a_old = i % 2

    def f_norm():
        hf_ref[...] = _rmsnorm(x1_ref[slot_up], fpre_ref[...]).astype(BF16)

    def f_gate_up(j):
        cols = slice(j * FFN_COLS, (j + 1) * FFN_COLS)
        hf = hf_ref[...]
        g = jnp.dot(hf, wg_ref[:, cols], preferred_element_type=F32)
        u = jnp.dot(hf, wu_ref[:, cols], preferred_element_type=F32)
        a_ref[a_new, :, cols] = (_silu(g) * u).astype(BF16)

    def f_down(k):
        cols = slice(k * G, (k + 1) * G)
        ff_ref[:, cols] = jnp.dot(a_ref[a_old], wd_ref[:, cols], preferred_element_type=F32)

    def f_out():
        y_ref[...] = x1_ref[slot_down] + _rmsnorm(ff_ref[...], fpost_ref[...])

    def cast_next():
        for src, dst in zip(cast_in, cast_out):
            dst[...] = src[...].astype(BF16)

    def m_norm():
        hm_ref[...] = _rmsnorm(x_ref[...], npre_ref[...]).astype(BF16)

    def m_proj(blk):
        cols = slice(blk * G, (blk + 1) * G)
        p_ref[:, cols] = jnp.dot(hm_ref[...], win_ref[:, cols], preferred_element_type=F32)

    def m_conv():
        cu = p_ref[:, G:2 * G] * p_ref[:, 2 * G:3 * G]
        ea_ref[CONV_PAD:CONV_PAD + TT, :] = cu
        ya = convw_ref[SC_WIDTH - 1:SC_WIDTH, :] * cu
        for back in range(1, SC_WIDTH):
            w = convw_ref[SC_WIDTH - 1 - back:SC_WIDTH - back, :]
            ya = ya + w * ea_ref[CONV_PAD - back:CONV_PAD - back + TT, :]
        cat_ref[:, 0:G] = (p_ref[:, 0:G] * ya).astype(BF16)
        oconv_ref[...] = ea_ref[TT + CONV_PAD - (SC_WIDTH - 1):TT + CONV_PAD, :]
        ea_ref[0:CONV_PAD, :] = ea_ref[TT:TT + CONV_PAD, :]

    def m_pool():
        pp = p_ref[:, 3 * G:4 * G]
        eb_ref[POOL_PAD:POOL_PAD + TT, :] = pp
        run = eb_ref[...]
        sums = {}
        w = 1
        while w < POOL_WINDOWS[-1]:
            run = run + pltpu.roll(run, w, 0)
            w *= 2
            sums[w] = run[POOL_PAD:]
        pos = t * TT + lax.broadcasted_iota(jnp.int32, (TT, G), 0)
        mean = _pool_select(sums, pos)
        yb = jnp.dot((mean - pp).astype(BF16), poolbd_ref[...], preferred_element_type=F32)
        cat_ref[:, G:2 * G] = (yb * pscale_ref[...]).astype(BF16)
        opool_ref[...] = eb_ref[TT + POOL_PAD - POOL_BUF:TT + POOL_PAD, :]
        eb_ref[0:POOL_PAD, :] = eb_ref[TT:TT + POOL_PAD, :]

    hg = {}

    def m_hgrn_decay():
        f = lower + (1.0 - lower) * _sigmoid(p_ref[:, 5 * G:6 * G])
        hk_ref[...] = 1.0 - f
        hb_ref[...] = _cumsum_rows_mxu(jnp.log(jnp.maximum(f, F_MIN)))

    def m_hgrn_gates():
        hq_ref[...] = _silu(p_ref[:, 4 * G:5 * G])

    def m_hgrn_state():
        b = hb_ref[...]
        o_inter = _hgrn_state_terms(hq_ref[...], hk_ref[...], p_ref[:, 6 * G:7 * G], b, st_ref)
        oi_ref[...] = o_inter
        o_ref[...] = o_inter
        hg["refs"], hg["span"] = _hgrn_refs(b, HGRN_CHUNK)

    def m_hgrn_chunk(j):
        lo, hi = j * HGRN_CHUNK, (j + 1) * HGRN_CHUNK
        contrib = _hgrn_fast_chunk(hq_ref[lo:TT, :], hb_ref[lo:TT, :], hk_ref[lo:hi, :],
                                   p_ref[lo:hi, 6 * G:7 * G], hg["refs"][j], HGRN_CHUNK)
        o_ref[lo:TT, :] = o_ref[lo:TT, :] + contrib

    def m_hgrn_out():
        yc = _head_norm_gate(o_ref[...], p_ref[:, 7 * G:8 * G], hnorm_ref[...])
        cat_ref[:, 2 * G:3 * G] = yc.astype(BF16)

    def m_glu():
        ed_ref[CONF_PAD:CONF_PAD + TT, :] = (p_ref[:, 8 * G:9 * G]
                                             * _sigmoid(p_ref[:, 9 * G:10 * G]))
        ed = ed_ref[...]
        for r in range(1, SUBLANES):
            sh_ref[r - 1] = pltpu.roll(ed, TT + CONF_PAD - r, 0)

    def m_conf(rb):
        base = rb * CONV_ROWS
        first = CONF_PAD - (CONF_WIDTH - 1)
        acc = None
        for j in range(CONF_WIDTH):
            tiles, r = divmod(first + j, SUBLANES)
            lo = base + SUBLANES * tiles
            src = ed_ref[lo:lo + CONV_ROWS, :] if r == 0 else sh_ref[r - 1, lo:lo + CONV_ROWS, :]
            term = cdw_ref[j:j + 1, :] * src
            acc = term if acc is None else acc + term
        yd = _conf_tail(acc, cb_ref[...], lng_ref[...], lnb_ref[...])
        cat_ref[base:base + CONV_ROWS, 3 * G:4 * G] = yd.astype(BF16)

    def m_conf_tail():
        oconf_ref[...] = ed_ref[TT + CONF_PAD - (CONF_WIDTH - 1):TT + CONF_PAD, :]
        ed_ref[0:CONF_PAD, :] = ed_ref[TT:TT + CONF_PAD, :]

    def m_out():
        mix_ref[...] = jnp.dot(cat_ref[...], wout_ref[...], preferred_element_type=F32)

    def m_out_norm():
        x1_ref[slot] = x_ref[...] + _rmsnorm(mix_ref[...], npost_ref[...])

    n_gu = D_FF // FFN_COLS
    gate_up = [functools.partial(f_gate_up, j) for j in range(n_gu)]
    proj = [functools.partial(m_proj, blk) for blk in range(N_BLOCKS)]
    down = [functools.partial(f_down, k) for k in range(D_MODEL // G)]
    conf_all = [functools.partial(m_conf, rb) for rb in range(TT // CONV_ROWS)]
    per = len(conf_all) // 4
    conf = [conf_all[k * per:(k + 1) * per] for k in range(4)]
    chunk = [functools.partial(m_hgrn_chunk, j) for j in range(n_chunks)]
    g = gate_up
    schedule = [
        down[0], f_norm, down[1], m_norm, down[2], cast_next, down[3],
        proj[8], proj[9], f_out,
        g[0], m_glu, proj[3],
        g[1], *conf[0], proj[0], proj[1], proj[2],
        g[2], *conf[1], proj[4], proj[5], proj[6],
        g[3], m_hgrn_decay, *conf[2], proj[7],
        g[4], *conf[3], m_conf_tail,
        g[5], m_conv, m_pool,
        g[6], m_hgrn_gates,
        g[7], m_hgrn_state,
        g[8], chunk[0], chunk[1],
        g[9], chunk[2], chunk[3],
        m_hgrn_out, m_out, g[10], m_out_norm,
    ]
    assert n_gu == 11 and n_chunks == 4 and len(conf_all) == 4 * per and len(down) == 4
    for piece in schedule:
        piece()

    @pl.when(hg["span"] >= FAST_DECAY_LIMIT)
    def _redo_exact():
        f = lower + (1.0 - lower) * _sigmoid(p_ref[:, 5 * G:6 * G])
        _hgrn_exact_attn(hb_ref, hq_ref, 1.0 - f, p_ref[:, 6 * G:7 * G], o_ref, 0, TT)
        yce = _head_norm_gate(o_ref[...] + oi_ref[...], p_ref[:, 7 * G:8 * G], hnorm_ref[...])
        cat_ref[:, 2 * G:3 * G] = yce.astype(BF16)
        mixe = jnp.dot(cat_ref[...], wout_ref[...], preferred_element_type=F32)
        x1_ref[slot] = x_ref[...] + _rmsnorm(mixe, npost_ref[...])

    @pl.when((t == nt - 1) & (i < n_tiles))
    def _state_out():
        s = st_ref[...].T
        for hh in range(HEADS):
            ohgrn_ref[hh] = s[hh * DK:(hh + 1) * DK, hh * DK:(hh + 1) * DK]


def _layer_sample_kernel(layer, NS, TS, start_pos,
                         x_ref, sconv_hbm, spool_hbm, shgrn_hbm, sconf_hbm,
                         npre_ref, win_ref, convw_ref, poolbd_ref, pscale_ref, lb_ref,
                         hnorm_ref, cdw_ref, cb_ref, lng_ref, lnb_ref, wout_ref, npost_ref,
                         fpre_ref, wg_hbm, wu_hbm, wd_hbm, fpost_ref,
                         y_ref, oconv_ref, opool_ref, ohgrn_ref, oconf_ref,
                         p_ref, u_ref, pool_ref, qT_ref, fT_ref, kT_ref, vT_ref, oT_ref, cat_ref,
                         hm_ref, wg_ref, wu_ref, wd_ref, a_ref, ff_ref, sems):
    i = pl.program_id(0)
    copies = [
        (sconv_hbm.at[layer], oconv_ref), (spool_hbm.at[layer], opool_ref),
        (sconf_hbm.at[layer], oconf_ref), (shgrn_hbm.at[layer], ohgrn_ref),
        (wg_hbm.at[0], wg_ref), (wu_hbm.at[0], wu_ref), (wd_hbm.at[0], wd_ref),
    ]

    def copy(k):
        return pltpu.make_async_copy(copies[k][0], copies[k][1], sems.at[k])

    def wait_at_first_step(ks):
        @pl.when(i == 0)
        def _wait():
            for k in ks:
                copy(k).wait()

    @pl.when(i == 0)
    def _start_copies():
        for k in range(len(copies)):
            copy(k).start()

    wait_at_first_step([0, 1, 2])

    def slab(t):
        return slice(t * NS, (t + 1) * NS)

    def proj(blk):
        cols = slice(blk * G, (blk + 1) * G)
        p_ref[:, cols] = jnp.dot(hm_ref[...], win_ref[:, cols], preferred_element_type=F32)

    def conv_in(j):
        if j < SC_WIDTH - 1:
            return oconv_ref[j]
        rows = slab(j - (SC_WIDTH - 1))
        return p_ref[rows, G:2 * G] * p_ref[rows, 2 * G:3 * G]

    def m_conv():
        for t in range(TS):
            ya = (convw_ref[0:1, :] * conv_in(t) + convw_ref[1:2, :] * conv_in(t + 1)
                  + convw_ref[2:3, :] * conv_in(t + 2))
            cat_ref[slab(t), 0:G] = (p_ref[slab(t), 0:G] * ya).astype(BF16)
        for j in range(SC_WIDTH - 1):
            oconv_ref[j] = conv_in(j + TS)

    def pool_in(j):
        if j < POOL_BUF:
            return opool_ref[j]
        return p_ref[slab(j - POOL_BUF), 3 * G:4 * G]

    def m_pool():
        for t in range(TS):
            idx = POOL_BUF + t
            run = pool_in(idx)
            sums = {}
            for j in range(1, POOL_BUF + 1):
                run = run + pool_in(idx - j)
                if j + 1 in POOL_WINDOWS:
                    sums[j + 1] = run
            pos = jnp.full((NS, G), start_pos + i * TS + t, jnp.int32)
            mean = _pool_select(sums, pos)
            pool_ref[slab(t), :] = mean - pool_in(idx)
        yb = jnp.dot(pool_ref[...].astype(BF16), poolbd_ref[...], preferred_element_type=F32)
        cat_ref[:, G:2 * G] = (yb * pscale_ref[...]).astype(BF16)
        for j in range(POOL_BUF):
            opool_ref[j] = pool_in(j + TS)

    def m_hgrn_gates():
        lower = _lower_bound(lb_ref[...], layer)
        q = _silu(p_ref[:, 4 * G:5 * G])
        f = lower + (1.0 - lower) * _sigmoid(p_ref[:, 5 * G:6 * G])
        for t in range(TS):
            qT_ref[t] = q[slab(t)].T
            fT_ref[t] = jnp.maximum(f[slab(t)], F_MIN).T
            kT_ref[t] = (1.0 - f[slab(t)]).T
            vT_ref[t] = p_ref[slab(t), 6 * G:7 * G].T

    def m_hgrn_scan():
        wait_at_first_step([3])
        for hh in range(HEADS):
            head = slice(hh * DK, (hh + 1) * DK)
            vts = [vT_ref[t, head, :] for t in range(TS)]

            def body(k, accs, hh=hh, vts=vts):
                c = hh * DK + k
                s = ohgrn_ref[c]
                out = []
                for t in range(TS):
                    s = fT_ref[t, pl.ds(c, 1), :] * s + kT_ref[t, pl.ds(c, 1), :] * vts[t]
                    out.append(accs[t] + qT_ref[t, pl.ds(c, 1), :] * s)
                ohgrn_ref[c] = s
                return tuple(out)

            accs = lax.fori_loop(0, DK, body,
                                 tuple(jnp.zeros((DK, NS), F32) for _ in range(TS)), unroll=2)
            for t in range(TS):
                oT_ref[t, head, :] = accs[t]

    def m_hgrn_out():
        o = jnp.concatenate([oT_ref[t].T for t in range(TS)], axis=0)
        yc = _head_norm_gate(o, p_ref[:, 7 * G:8 * G], hnorm_ref[...])
        cat_ref[:, 2 * G:3 * G] = yc.astype(BF16)

    def m_glu():
        u_ref[...] = p_ref[:, 8 * G:9 * G] * _sigmoid(p_ref[:, 9 * G:10 * G])

    HALF = NS // 2

    def conf_in(j, rows):
        if j < CONF_WIDTH - 1:
            return oconf_ref[j, rows, :]
        base = (j - (CONF_WIDTH - 1)) * NS
        return u_ref[base + rows.start:base + rows.stop, :]

    def m_conf():
        for t in range(TS):
            for hf in range(2):
                rows = slice(hf * HALF, (hf + 1) * HALF)
                acc = None
                for j in range(CONF_WIDTH):
                    term = cdw_ref[j:j + 1, :] * conf_in(t + j, rows)
                    acc = term if acc is None else acc + term
                yd = _conf_tail(acc, cb_ref[...], lng_ref[...], lnb_ref[...])
                cat_ref[t * NS + hf * HALF:t * NS + (hf + 1) * HALF, 3 * G:4 * G] = (
                    yd.astype(BF16))
        for j in range(CONF_WIDTH - 1):
            oconf_ref[j] = conf_in(j + TS, slice(0, NS))

    hm_ref[...] = _rmsnorm(x_ref[...], npre_ref[...]).astype(BF16)
    for blk in range(N_BLOCKS):
        proj(blk)
    m_conv()
    m_pool()
    m_hgrn_gates()
    m_hgrn_scan()
    m_hgrn_out()
    m_glu()
    m_conf()
    mix = jnp.dot(cat_ref[...], wout_ref[...], preferred_element_type=F32)
    y_ref[...] = x_ref[...] + _rmsnorm(mix, npost_ref[...])

    wait_at_first_step([4, 5, 6])
    hm_ref[...] = _rmsnorm(y_ref[...], fpre_ref[...]).astype(BF16)
    for j in range(D_FF // FFN_COLS):
        cols = slice(j * FFN_COLS, (j + 1) * FFN_COLS)
        hf = hm_ref[...]
        g = jnp.dot(hf, wg_ref[:, cols], preferred_element_type=F32)
        u = jnp.dot(hf, wu_ref[:, cols], preferred_element_type=F32)
        a_ref[:, cols] = (_silu(g) * u).astype(BF16)
    for k in range(D_MODEL // G):
        cols = slice(k * G, (k + 1) * G)
        ff_ref[:, cols] = jnp.dot(a_ref[...], wd_ref[:, cols], preferred_element_type=F32)
    y_ref[...] = y_ref[...] + _rmsnorm(ff_ref[...], fpost_ref[...])


def _layer_spec(shape, layer, single_buffer=False):
    nd = len(shape)

    def imap(*_):
        return (layer,) + (0,) * nd

    if single_buffer:
        return pl.BlockSpec((None,) + tuple(shape), imap, pipeline_mode=pl.Buffered(1))
    return pl.BlockSpec((None,) + tuple(shape), imap)


def _mixer_weight_specs(layer):
    return [
        _layer_spec((1, D_MODEL), layer),
        _layer_spec((D_MODEL, D_IN), 0, True),
        _layer_spec((SC_WIDTH, G), layer),
        _layer_spec((G, G), layer),
        _layer_spec((1, G), layer),
        pl.BlockSpec((DEPTH, G), lambda *_: (0, 0)),
        _layer_spec((1, G), layer),
        _layer_spec((CONF_WIDTH, G), layer),
        _layer_spec((1, G), layer),
        _layer_spec((1, G), layer),
        _layer_spec((1, G), layer),
        _layer_spec((D_MODEL, D_MODEL), 0, True),
        _layer_spec((1, D_MODEL), layer),
    ]


def _ffn_weight_specs(layer):
    return [
        _layer_spec((1, D_MODEL), layer),
        _layer_spec((D_MODEL, D_FF), 0, True),
        _layer_spec((D_MODEL, D_FF), 0, True),
        _layer_spec((D_FF, D_MODEL), 0, True),
        _layer_spec((1, D_MODEL), layer),
    ]


def _layer_prompt(layer, x, mixer_wts, ffn_wts, next_f32):
    n, seq, _ = x.shape
    TT = PROMPT_TILE
    nt = seq // TT
    last = n * nt - 1
    n_steps = n * nt + 2

    def mix_tile(i):
        return jnp.minimum(i, last)

    def ffn_tile(i):
        return jnp.maximum(i - 2, 0)

    out_shape = [
        jax.ShapeDtypeStruct((n, seq, D_MODEL), F32),
        jax.ShapeDtypeStruct((n, SC_WIDTH - 1, G), F32),
        jax.ShapeDtypeStruct((n, POOL_BUF, G), F32),
        jax.ShapeDtypeStruct((n, HEADS, DK, DK), F32),
        jax.ShapeDtypeStruct((n, CONF_WIDTH - 1, G), F32),
    ]
    out_specs = [
        pl.BlockSpec((None, TT, D_MODEL), lambda i: (ffn_tile(i) // nt, ffn_tile(i) % nt, 0)),
        pl.BlockSpec((None, SC_WIDTH - 1, G), lambda i: (mix_tile(i) // nt, 0, 0)),
        pl.BlockSpec((None, POOL_BUF, G), lambda i: (mix_tile(i) // nt, 0, 0)),
        pl.BlockSpec((None, HEADS, DK, DK), lambda i: (mix_tile(i) // nt, 0, 0, 0)),
        pl.BlockSpec((None, CONF_WIDTH - 1, G), lambda i: (mix_tile(i) // nt, 0, 0)),
    ]
    cast_in_specs = []
    for w in next_f32:
        _, rows, cols = w.shape
        blk = next(b for b in range(CAST_ROWS, rows + 1, CAST_ROWS)
                   if rows % b == 0 and rows // b <= n_steps)
        n_blk = rows // blk

        def in_map(i, n_blk=n_blk):
            return (layer + 1, jnp.minimum(i, n_blk - 1), 0)

        def out_map(i, n_blk=n_blk):
            return (0, jnp.minimum(i, n_blk - 1), 0)

        cast_in_specs.append(pl.BlockSpec((None, blk, cols), in_map))
        out_specs.append(pl.BlockSpec((None, blk, cols), out_map))
        out_shape.append(jax.ShapeDtypeStruct((1, rows, cols), BF16))
    scratch = [
        pltpu.VMEM((TT, D_IN), F32),
        pltpu.VMEM((CONV_PAD + TT, G), F32),
        pltpu.VMEM((POOL_PAD + TT, G), F32),
        pltpu.VMEM((CONF_PAD + TT, G), F32),
        pltpu.VMEM((SUBLANES - 1, CONF_PAD + TT, G), F32),
        pltpu.VMEM((G, G), F32),
        pltpu.VMEM((TT, G), F32),
        pltpu.VMEM((TT, G), F32),
        pltpu.VMEM((TT, G), F32),
        pltpu.VMEM((TT, G), F32),
        pltpu.VMEM((TT, G), F32),
        pltpu.VMEM((TT, D_MODEL), BF16),
        pltpu.VMEM((3, TT, D_MODEL), F32),
        pltpu.VMEM((TT, D_MODEL), BF16),
        pltpu.VMEM((TT, D_MODEL), BF16),
        pltpu.VMEM((2, TT, D_FF), BF16),
        pltpu.VMEM((TT, D_MODEL), F32),
        pltpu.VMEM((TT, D_MODEL), F32),
    ]
    return pl.pallas_call(
        functools.partial(_layer_prompt_kernel, layer, TT, nt, n * nt, len(next_f32)),
        grid=(n_steps,),
        in_specs=[pl.BlockSpec((None, TT, D_MODEL),
                               lambda i: (mix_tile(i) // nt, mix_tile(i) % nt, 0))]
        + _mixer_weight_specs(layer) + _ffn_weight_specs(layer) + cast_in_specs,
        out_specs=tuple(out_specs),
        out_shape=tuple(out_shape),
        scratch_shapes=scratch,
        compiler_params=pltpu.CompilerParams(
            dimension_semantics=("arbitrary",), vmem_limit_bytes=VMEM_LIMIT),
        name=f"layer_prompt_l{layer}",
    )(x, *mixer_wts, *ffn_wts, *next_f32)


def _layer_sample(layer, x2d, s_conv, s_pool, s_hgrn, s_conf, mixer_wts, ffn_wts, n_seq):
    m = x2d.shape[0]
    NS = n_seq
    TS = SAMPLE_STEPS
    M = TS * NS

    def out_spec(shape):
        nd = len(shape)
        return pl.BlockSpec(tuple(shape), lambda i: (0,) * nd, pipeline_mode=pl.Buffered(1))

    in_hbm = pl.BlockSpec(memory_space=pl.ANY)
    state_shapes = [(SC_WIDTH - 1, NS, G), (POOL_BUF, NS, G), (G, DK, NS), (CONF_WIDTH - 1, NS, G)]
    in_specs = ([pl.BlockSpec((M, D_MODEL), lambda i: (i, 0))]
                + [in_hbm] * len(state_shapes) + _mixer_weight_specs(layer)
                + [_layer_spec((1, D_MODEL), layer), in_hbm, in_hbm, in_hbm,
                   _layer_spec((1, D_MODEL), layer)])
    out_specs = tuple([pl.BlockSpec((M, D_MODEL), lambda i: (i, 0))]
                      + [out_spec(s) for s in state_shapes])
    out_shape = tuple([jax.ShapeDtypeStruct((m, D_MODEL), F32)]
                      + [jax.ShapeDtypeStruct(s, F32) for s in state_shapes])
    scratch = [
        pltpu.VMEM((M, D_IN), F32),
        pltpu.VMEM((M, G), F32),
        pltpu.VMEM((M, G), F32),
        pltpu.VMEM((TS, G, NS), F32),
        pltpu.VMEM((TS, G, NS), F32),
        pltpu.VMEM((TS, G, NS), F32),
        pltpu.VMEM((TS, G, NS), F32),
        pltpu.VMEM((TS, G, NS), F32),
        pltpu.VMEM((M, D_MODEL), BF16),
        pltpu.VMEM((M, D_MODEL), BF16),
        pltpu.VMEM((D_MODEL, D_FF), BF16),
        pltpu.VMEM((D_MODEL, D_FF), BF16),
        pltpu.VMEM((D_FF, D_MODEL), BF16),
        pltpu.VMEM((M, D_FF), BF16),
        pltpu.VMEM((M, D_MODEL), F32),
        pltpu.SemaphoreType.DMA((N_FIRST_STEP_COPIES,)),
    ]
    return pl.pallas_call(
        functools.partial(_layer_sample_kernel, layer, NS, TS, PAST_LEN),
        grid=(m // M,),
        in_specs=in_specs,
        out_specs=out_specs,
        out_shape=out_shape,
        scratch_shapes=scratch,
        compiler_params=pltpu.CompilerParams(
            dimension_semantics=("arbitrary",), vmem_limit_bytes=VMEM_LIMIT),
        name=f"layer_sample_l{layer}",
    )(x2d, s_conv, s_pool, s_hgrn, s_conf, *mixer_wts, *ffn_wts)


def kernel(x_prompt, x_sample, state_conv, state_pool, state_hgrn, state_conf, norm_mix_pre, norm_mix_post, w_in, conv_w, pool_w, pool_scale, hgrn_lb, hgrn_norm, conf_dw, conf_b, conf_ln_g, conf_ln_b, w_out, norm_ffn_pre, norm_ffn_post, w_gate, w_up, w_down):
    def row(a):
        return a.reshape(DEPTH, 1, a.shape[-1])

    eye = jnp.eye(G // POOL_CH, dtype=pool_w.dtype)
    pool_bd = (pool_w[:, :, :, None, :] * eye[None, :, None, :, None]).reshape(DEPTH, G, G)
    pool_bd = pool_bd.astype(BF16)
    big_f32 = (w_in, w_out, w_gate, w_up, w_down)
    big = tuple(w[0:1].astype(BF16) for w in big_f32)

    def mixer_weights(big):
        return (row(norm_mix_pre), big[0], conv_w, pool_bd, row(pool_scale), hgrn_lb,
                row(hgrn_norm), conf_dw, row(conf_b), row(conf_ln_g), row(conf_ln_b), big[1],
                row(norm_mix_post))

    def ffn_weights(big):
        return (row(norm_ffn_pre), big[2], big[3], big[4], row(norm_ffn_post))

    ns, ts, _ = x_sample.shape
    xs = x_sample.transpose(1, 0, 2).reshape(ts * ns, D_MODEL)
    sc_t = state_conv.transpose(1, 2, 0, 3)
    sp_t = state_pool.transpose(1, 2, 0, 3)
    sf_t = state_conf.transpose(1, 2, 0, 3)
    sh_t = state_hgrn.transpose(1, 2, 3, 4, 0).reshape(DEPTH, G, DK, ns)
    xp = x_prompt
    p_states, s_states = [], []
    for layer in range(DEPTH):
        mixer_wts, ffn_wts = mixer_weights(big), ffn_weights(big)
        next_f32 = big_f32 if layer + 1 < DEPTH else ()
        xp, *rest = _layer_prompt(layer, xp, mixer_wts, ffn_wts, next_f32)
        p_states.append(rest[:4])
        big = tuple(rest[4:])
        xs, *sts = _layer_sample(layer, xs, sc_t, sp_t, sh_t, sf_t, mixer_wts, ffn_wts, ns)
        s_states.append(sts)
    xs = xs.reshape(ts, ns, D_MODEL).transpose(1, 0, 2)

    def stack(states, i):
        return jnp.stack([states[layer][i] for layer in range(DEPTH)], axis=1)

    def stack_t(i):
        return jnp.stack([s_states[layer][i] for layer in range(DEPTH)], axis=0).transpose(2, 0, 1, 3)

    hgrn_s = jnp.stack([s_states[layer][2] for layer in range(DEPTH)], axis=0)
    hgrn_s = hgrn_s.reshape(DEPTH, HEADS, DK, DK, ns).transpose(4, 0, 1, 2, 3)

    return (xp, xs,
            stack(p_states, 0), stack(p_states, 1), stack(p_states, 2), stack(p_states, 3),
            stack_t(0), stack_t(1), hgrn_s, stack_t(3))
```
